```python
import jax, jax.numpy as jnp
from jax import lax
import numpy as np


D_MODEL = 2048
BATCH = 1
SEQ = 16384
DEPTH = 1
DEC_BATCH = 32
DEC_SEQ = 4
PAST_LEN = 16384
PAGE_SIZE = 128

D_RNN = D_MODEL
RNN_BLOCKS = 8
RNN_BLOCK_W = D_RNN // RNN_BLOCKS
CONV_W = 4
LRU_C = 8.0
ATT_GROUPS = ((128, 1), (512, 4), (2048, 16))
N_GROUPS = len(ATT_GROUPS)
HEADS_PER_GROUP = 8
HEAD_DIM = 128
D_QKV = N_GROUPS * HEADS_PER_GROUP * HEAD_DIM
D_ATT_OUT = HEADS_PER_GROUP * HEAD_DIM
PEER_HEADS = 8
N_KEYS = 128
N_EXPERTS = N_KEYS * N_KEYS
PEER_TOPK = 16
D_KEY = 256
D_HALF = D_KEY // 2
PEER_CHUNK = 128
D_IN = 2 * D_RNN + 3 * D_QKV + 2 * D_MODEL
NORM_EPS = 1e-6

kernel_name = 'hawk_dilated_peer_decode_step'


def rms_norm(x, g):
    xf = x.astype(jnp.float32)
    y = xf * lax.rsqrt(jnp.mean(xf * xf, axis=-1, keepdims=True) + NORM_EPS)
    return (y * g.astype(jnp.float32)).astype(x.dtype)


def split_proj(z):
    sizes = (D_RNN, D_RNN, D_QKV, D_QKV, D_QKV, D_MODEL, D_MODEL)
    idx, acc = [], 0
    for s in sizes[:-1]:
        acc += s
        idx.append(acc)
    return jnp.split(z, idx, axis=-1)


def causal_conv(x, past, w, b):
    T = x.shape[1]
    xp = jnp.concatenate([past.astype(x.dtype), x], axis=1)
    out = b
    for j in range(CONV_W):
        out = out + w[j] * xp[:, CONV_W - 1 - j: CONV_W - 1 - j + T]
    return out.astype(x.dtype), xp[:, -(CONV_W - 1):]


def rg_lru(xc, h0, w_gate_x, b_gate_x, w_gate_a, b_gate_a, lru_lambda):
    B, T, _ = xc.shape
    xb = xc.reshape(B, T, RNN_BLOCKS, RNN_BLOCK_W)
    gx = jax.nn.sigmoid((jnp.einsum('btnc,ncd->btnd', xb, w_gate_x) + b_gate_x).astype(jnp.float32)).reshape(B, T, D_RNN)
    ga = jax.nn.sigmoid((jnp.einsum('btnc,ncd->btnd', xb, w_gate_a) + b_gate_a).astype(jnp.float32)).reshape(B, T, D_RNN)
    log_a = LRU_C * ga * jax.nn.log_sigmoid(lru_lambda.astype(jnp.float32))
    a = jnp.exp(log_a)
    bvals = jnp.sqrt(-jnp.expm1(2.0 * log_a)) * gx * xc.astype(jnp.float32)
    bvals = bvals.at[:, 0].add(a[:, 0] * h0.astype(jnp.float32))

    def combine(l, r):
        return (l[0] * r[0], r[0] * l[1] + r[1])

    _, h = lax.associative_scan(combine, (a, bvals), axis=1)
    return h.astype(xc.dtype), h[:, -1].astype(h0.dtype)


def softmax_parts(s, v_einsum, vals):
    m = jnp.max(s, axis=-1, keepdims=True)
    p = jnp.exp(s - m)
    den = jnp.sum(p, axis=-1)
    o = jnp.einsum(v_einsum, p, vals.astype(jnp.float32))
    return o, den, m[..., 0] + jnp.log(den)


def dilated_attn_prompt(q, k, v, window, dil):
    B, T, H, Dh = q.shape
    band = window // dil
    L = T // dil
    nb = -(-L // band)
    Lp = nb * band

    def sub(a):
        a = a.reshape(B, L, dil, H, Dh).transpose(0, 2, 1, 3, 4)
        a = jnp.pad(a, ((0, 0), (0, 0), (0, Lp - L), (0, 0), (0, 0)))
        return a.reshape(B, dil, nb, band, H, Dh)

    def with_prev(a):
        prev = jnp.pad(a, ((0, 0), (0, 0), (1, 0), (0, 0), (0, 0), (0, 0)))[:, :, :-1]
        return jnp.concatenate([prev, a], axis=3)

    qb = sub(q)
    kk = with_prev(sub(k))
    vv = with_prev(sub(v))
    s = jnp.einsum('brnqhd,brnkhd->brnhqk', qb, kk, preferred_element_type=jnp.float32)
    qi = jnp.arange(band)[:, None]
    ki = jnp.arange(2 * band)[None, :]
    dist = qi + band - ki
    kpos = jnp.arange(nb)[:, None, None] * band + ki - band
    mask = (dist >= 0) & (dist <= band) & (kpos >= 0)
    s = jnp.where(mask[None, None, :, None], s, -jnp.inf)
    o, den, lse = softmax_parts(s, 'brnhqk,brnkhd->brnqhd', vv)
    o = o / den.transpose(0, 1, 2, 4, 3)[..., None]
    o = o.reshape(B, dil, Lp, H, Dh)[:, :, :L].transpose(0, 2, 1, 3, 4).reshape(B, T, H, Dh)
    lse = lse.transpose(0, 1, 2, 4, 3).reshape(B, dil, Lp, H)[:, :, :L].transpose(0, 2, 1, 3).reshape(B, T, H)
    new_kv = jnp.stack([k, v], axis=2)[:, -min(window, T):]
    return o, lse, new_kv


def dilated_attn_sample(q, k, v, buf, window, dil):
    DB, S, H, Dh = q.shape
    Lg = buf.shape[1]
    k_all = jnp.concatenate([buf[:, :, 0], k.astype(buf.dtype)], axis=1)
    v_all = jnp.concatenate([buf[:, :, 1], v.astype(buf.dtype)], axis=1)
    n_k = window // dil + 1
    idx = Lg + jnp.arange(S)[:, None] - dil * jnp.arange(n_k)[None, :]
    valid = idx >= 0
    idxc = jnp.maximum(idx, 0)
    kg = k_all[:, idxc]
    vg = v_all[:, idxc]
    s = jnp.einsum('bshd,bsjhd->bshj', q.astype(kg.dtype), kg, preferred_element_type=jnp.float32)
    s = jnp.where(valid[None, :, None, :], s, -jnp.inf)
    o, den, lse = softmax_parts(s, 'bshj,bsjhd->bshd', vg)
    o = o / den[..., None]
    new_buf = jnp.concatenate([buf, jnp.stack([k, v], axis=2).astype(buf.dtype)], axis=1)[:, -Lg:]
    return o, lse, new_buf


def peer_ffn(x2d, w_query, sub_keys, expert_u, expert_v):
    R = x2d.shape[0]
    n_chunks = -(-R // PEER_CHUNK)
    xp = jnp.pad(x2d, ((0, n_chunks * PEER_CHUNK - R), (0, 0))).reshape(n_chunks, PEER_CHUNK, D_MODEL)
    sk = sub_keys.astype(jnp.float32)

    def one_chunk(xb):
        q = jnp.dot(xb, w_query, preferred_element_type=jnp.float32).reshape(PEER_CHUNK, PEER_HEADS, 2, D_HALF)
        s = jnp.einsum('chpd,hpnd->chpn', q, sk)
        sv, si = lax.top_k(s, PEER_TOPK)
        cand = (sv[:, :, 0, :, None] + sv[:, :, 1, None, :]).reshape(PEER_CHUNK, PEER_HEADS, PEER_TOPK * PEER_TOPK)
        cidx = (si[:, :, 0, :, None] * N_KEYS + si[:, :, 1, None, :]).reshape(PEER_CHUNK, PEER_HEADS, PEER_TOPK * PEER_TOPK)
        fv, fpos = lax.top_k(cand, PEER_TOPK)
        eidx = jnp.take_along_axis(cidx, fpos, axis=-1)
        g = jax.nn.softmax(fv, axis=-1)
        act = jnp.einsum('cd,chkd->chk', xb, expert_u[eidx], preferred_element_type=jnp.float32)
        coef = (jax.nn.gelu(act, approximate=False) * g).astype(xb.dtype)
        return jnp.einsum('chk,chkd->cd', coef, expert_v[eidx])

    return lax.map(one_chunk, xp).reshape(n_chunks * PEER_CHUNK, D_MODEL)[:R]


def decoder_layer(x, conv_past, h_past, kv_past, norm_mix, w_in, conv_w, conv_b, w_gate_x, b_gate_x,
                  w_gate_a, b_gate_a, lru_lambda, w_rnn_out, w_att_out, w_o, norm_ffn, w_query,
                  sub_keys, expert_u, expert_v):
    B, T, _ = x.shape
    z = rms_norm(x, norm_mix) @ w_in
    xr, yr, q, k, v, gate_a, gate_b = split_proj(z)
    xc, conv_new = causal_conv(xr, conv_past, conv_w, conv_b)
    hseq, h_new = rg_lru(xc, h_past, w_gate_x, b_gate_x, w_gate_a, b_gate_a, lru_lambda)
    rnn_out = (jax.nn.gelu(yr, approximate=False) * hseq) @ w_rnn_out
    q = q.reshape(B, T, N_GROUPS, HEADS_PER_GROUP, HEAD_DIM) * (HEAD_DIM ** -0.5)
    k = k.reshape(B, T, N_GROUPS, HEADS_PER_GROUP, HEAD_DIM)
    v = v.reshape(B, T, N_GROUPS, HEADS_PER_GROUP, HEAD_DIM)
    outs, lses, kv_new = [], [], []
    for gi, (win, dil) in enumerate(ATT_GROUPS):
        if kv_past is None:
            o, lse, nkv = dilated_attn_prompt(q[:, :, gi], k[:, :, gi], v[:, :, gi], win, dil)
        else:
            o, lse, nkv = dilated_attn_sample(q[:, :, gi], k[:, :, gi], v[:, :, gi], kv_past[gi], win, dil)
        outs.append(o)
        lses.append(lse)
        kv_new.append(nkv)
    alpha = jax.nn.softmax(jnp.stack(lses, axis=0), axis=0)
    att = jnp.einsum('gbth,gbthd->bthd', alpha, jnp.stack(outs, axis=0)).astype(x.dtype).reshape(B, T, D_ATT_OUT)
    att_out = att @ w_att_out
    mixed = (jax.nn.sigmoid(gate_a.astype(jnp.float32)) * rnn_out + jax.nn.sigmoid(gate_b.astype(jnp.float32)) * att_out).astype(x.dtype)
    h = x + mixed @ w_o
    hn = rms_norm(h, norm_ffn)
    h = h + peer_ffn(hn.reshape(B * T, D_MODEL), w_query, sub_keys, expert_u, expert_v).reshape(B, T, D_MODEL)
    return h, conv_new, h_new, kv_new


def setup_inputs(seed: int = 0) -> dict:
    key = jax.random.key(seed)
    ks = jax.random.split(key, 24)

    def nrm(k, shape, scale):
        return jax.random.normal(k, shape, jnp.float32) * scale

    a_c = jax.random.uniform(ks[0], (DEPTH, D_RNN), jnp.float32, minval=0.9, maxval=0.999)
    sig_l = a_c ** (1.0 / LRU_C)
    lru_lambda = jnp.log(sig_l) - jnp.log1p(-sig_l)
    cache_shape = lambda w: (DEPTH, DEC_BATCH, min(w, PAST_LEN), 2, HEADS_PER_GROUP, HEAD_DIM)
    return {
        'x_prompt': nrm(ks[1], (BATCH, SEQ, D_MODEL), 1.0),
        'x_sample': nrm(ks[2], (DEC_BATCH, DEC_SEQ, D_MODEL), 1.0),
        'state_conv': nrm(ks[3], (DEPTH, DEC_BATCH, CONV_W - 1, D_RNN), 1.0),
        'state_h': nrm(ks[4], (DEPTH, DEC_BATCH, D_RNN), 0.5),
        'cache_kv_w128': nrm(ks[5], cache_shape(ATT_GROUPS[0][0]), 1.0),
        'cache_kv_w512': nrm(ks[6], cache_shape(ATT_GROUPS[1][0]), 1.0),
        'cache_kv_w2048': nrm(ks[7], cache_shape(ATT_GROUPS[2][0]), 1.0),
        'norm_mix': 1.0 + nrm(ks[8], (DEPTH, D_MODEL), 0.02),
        'w_in': nrm(ks[9], (DEPTH, D_MODEL, D_IN), D_MODEL ** -0.5),
        'conv_w': nrm(ks[10], (DEPTH, CONV_W, D_RNN), CONV_W ** -0.5),
        'conv_b': nrm(ks[11], (DEPTH, D_RNN), 0.01),
        'w_gate_x': nrm(ks[12], (DEPTH, RNN_BLOCKS, RNN_BLOCK_W, RNN_BLOCK_W), RNN_BLOCK_W ** -0.5),
        'b_gate_x': nrm(ks[13], (DEPTH, RNN_BLOCKS, RNN_BLOCK_W), 0.01),
        'w_gate_a': nrm(ks[14], (DEPTH, RNN_BLOCKS, RNN_BLOCK_W, RNN_BLOCK_W), RNN_BLOCK_W ** -0.5),
        'b_gate_a': nrm(ks[15], (DEPTH, RNN_BLOCKS, RNN_BLOCK_W), 0.01),
        'lru_lambda': lru_lambda,
        'w_rnn_out': nrm(ks[16], (DEPTH, D_RNN, D_MODEL), D_RNN ** -0.5),
        'w_att_out': nrm(ks[17], (DEPTH, D_ATT_OUT, D_MODEL), D_ATT_OUT ** -0.5),
        'w_o': nrm(ks[18], (DEPTH, D_MODEL, D_MODEL), D_MODEL ** -0.5),
        'norm_ffn': 1.0 + nrm(ks[19], (DEPTH, D_MODEL), 0.02),
        'w_query': nrm(ks[20], (DEPTH, D_MODEL, PEER_HEADS * D_KEY), D_MODEL ** -0.5),
        'sub_keys': nrm(ks[21], (DEPTH, PEER_HEADS, 2, N_KEYS, D_HALF), D_HALF ** -0.5),
        'expert_u': nrm(ks[22], (DEPTH, N_EXPERTS, D_MODEL), D_MODEL ** -0.5),
        'expert_v': nrm(ks[23], (DEPTH, N_EXPERTS, D_MODEL), PEER_HEADS ** -0.5),
        'norm_final': 1.0 + nrm(jax.random.fold_in(key, 99), (D_MODEL,), 0.02),
    }


def reference(x_prompt, x_sample, state_conv, state_h, cache_kv_w128, cache_kv_w512, cache_kv_w2048,
              norm_mix, w_in, conv_w, conv_b, w_gate_x, b_gate_x, w_gate_a, b_gate_a, lru_lambda,
              w_rnn_out, w_att_out, w_o, norm_ffn, w_query, sub_keys, expert_u, expert_v, norm_final):
    hp, hs = x_prompt, x_sample
    pc, ph, pk0, pk1, pk2 = [], [], [], [], []
    sc, sh, sk0, sk1, sk2 = [], [], [], [], []
    for l in range(DEPTH):
        lw = (norm_mix[l], w_in[l], conv_w[l], conv_b[l], w_gate_x[l], b_gate_x[l], w_gate_a[l],
              b_gate_a[l], lru_lambda[l], w_rnn_out[l], w_att_out[l], w_o[l], norm_ffn[l],
              w_query[l], sub_keys[l], expert_u[l], expert_v[l])
        conv0 = jnp.zeros((hp.shape[0], CONV_W - 1, D_RNN), hp.dtype)
        h0 = jnp.zeros((hp.shape[0], D_RNN), state_h.dtype)
        hp, c_p, h_p, kv_p = decoder_layer(hp, conv0, h0, None, *lw)
        hs, c_s, h_s, kv_s = decoder_layer(hs, state_conv[l], state_h[l],
                                           (cache_kv_w128[l], cache_kv_w512[l], cache_kv_w2048[l]), *lw)
        pc.append(c_p); ph.append(h_p); pk0.append(kv_p[0]); pk1.append(kv_p[1]); pk2.append(kv_p[2])
        sc.append(c_s); sh.append(h_s); sk0.append(kv_s[0]); sk1.append(kv_s[1]); sk2.append(kv_s[2])
    y_prompt = rms_norm(hp, norm_final)
    y_sample = rms_norm(hs, norm_final)
    conv_prompt = jnp.stack(pc, axis=0)
    h_prompt = jnp.stack(ph, axis=0)
    kv128_prompt = jnp.stack(pk0, axis=0)
    kv512_prompt = jnp.stack(pk1, axis=0)
    kv2048_prompt = jnp.stack(pk2, axis=0)
    conv_sample = jnp.stack(sc, axis=0)
    h_sample = jnp.stack(sh, axis=0)
    kv128_sample = jnp.stack(sk0, axis=0)
    kv512_sample = jnp.stack(sk1, axis=0)
    kv2048_sample = jnp.stack(sk2, axis=0)
    return (y_prompt, y_sample, conv_prompt, h_prompt, kv128_prompt, kv512_prompt, kv2048_prompt,
            conv_sample, h_sample, kv128_sample, kv512_sample, kv2048_sample)
```

```python
import functools
import math

import jax
import jax.numpy as jnp
from jax import lax
from jax.experimental import pallas as pl
from jax.experimental.pallas import tpu as pltpu

F32 = jnp.float32
MXU_DTYPE = jnp.bfloat16

RNN_BLOCKS = 8
CONV_W = 4
LRU_C = 8.0
ATT_GROUPS = ((128, 1), (512, 4), (2048, 16))
N_GROUPS = len(ATT_GROUPS)
HEADS = 8
HEAD_DIM = 128
BAND = 128
PEER_HEADS = 8
N_KEYS = 128
PEER_TOPK = 16
NORM_EPS = 1e-6

LANES = 128
SUBLANES = 8
VMEM_LIMIT_BYTES = 56 * 1024 * 1024

NEG_INF = float("-inf")
POS_INF = float("inf")


def _params(*sem):
    return pltpu.CompilerParams(dimension_semantics=sem, vmem_limit_bytes=VMEM_LIMIT_BYTES)


def _gelu(x):
    return 0.5 * x * (1.0 + lax.erf(x * math.sqrt(0.5)))


def _rms(x, gain):
    ms = jnp.mean(x * x, axis=-1, keepdims=True)
    return x * lax.rsqrt(ms + NORM_EPS) * gain


def _inproj_kernel(x_ref, g_ref, w_ref, z_ref, xn_sc):
    @pl.when(pl.program_id(1) == 0)
    def _():
        xn_sc[...] = _rms(x_ref[...], g_ref[...]).astype(MXU_DTYPE)

    z_ref[...] = jnp.dot(xn_sc[...], w_ref[...], preferred_element_type=F32)


def _inproj(x, gain, w, *, tm, tn):
    m, k = x.shape
    n = w.shape[1]
    return pl.pallas_call(
        _inproj_kernel,
        grid=(m // tm, n // tn),
        in_specs=[
            pl.BlockSpec((tm, k), lambda i, j: (i, 0)),
            pl.BlockSpec((1, k), lambda i, j: (0, 0)),
            pl.BlockSpec((k, tn), lambda i, j: (0, j)),
        ],
        out_specs=pl.BlockSpec((tm, tn), lambda i, j: (i, j)),
        out_shape=jax.ShapeDtypeStruct((m, n), F32),
        scratch_shapes=[pltpu.VMEM((tm, k), MXU_DTYPE)],
        compiler_params=_params("parallel", "arbitrary"),
        name="inproj",
    )(x, gain, w)


def _lru_gates(xc, wgx, bgx, wga, bga, log_sig):
    xb = xc.astype(MXU_DTYPE)
    gx = jax.nn.sigmoid(jnp.dot(xb, wgx, preferred_element_type=F32) + bgx)
    ga = jax.nn.sigmoid(jnp.dot(xb, wga, preferred_element_type=F32) + bga)
    a = jnp.exp(LRU_C * ga * log_sig)
    b = jnp.sqrt(1.0 - a * a) * gx * xc
    return a, b


def _log_sigmoid(x):
    return -(jnp.maximum(-x, 0.0) + jnp.log1p(jnp.exp(-jnp.abs(x))))


def _rnn_prompt_kernel(xr_ref, yr_ref, cw_ref, cb_ref, wgx_ref, bgx_ref, wga_ref, bga_ref, lam_ref,
                       p0_ref, h0_ref, g_ref, pn_ref, hn_ref, p_sc, h_sc, a_sc, b_sc):
    tt, d = xr_ref.shape
    bw = d // RNN_BLOCKS

    @pl.when(pl.program_id(0) == 0)
    def _():
        p_sc[...] = p0_ref[...]
        h_sc[...] = h0_ref[...]

    x = xr_ref[...]
    past = p_sc[...]
    row8 = lax.broadcasted_iota(jnp.int32, (SUBLANES, d), 0)
    xc = cb_ref[...] + cw_ref[0:1, :] * x
    for j in range(1, CONV_W):
        xs = pltpu.roll(x, j, 0)
        head = jnp.where(row8 < j, pltpu.roll(past, j, 0), xs[:SUBLANES])
        xs = jnp.concatenate([head, xs[SUBLANES:]], axis=0)
        xc = xc + cw_ref[j:j + 1, :] * xs
    p_sc[...] = x[tt - SUBLANES:]
    pn_ref[...] = x[tt - SUBLANES:]

    log_sig = _log_sigmoid(lam_ref[...])
    rowm = lax.broadcasted_iota(jnp.int32, (tt, bw), 0) & (SUBLANES - 1)
    for n in range(RNN_BLOCKS):
        sl = slice(n * bw, (n + 1) * bw)
        a, b = _lru_gates(xc[:, sl], wgx_ref[n], bgx_ref[:, sl], wga_ref[n], bga_ref[:, sl],
                          log_sig[:, sl])
        for sh in (1, 2, 4):
            ok = rowm >= sh
            a_sh = pltpu.roll(a, sh, 0)
            b_sh = pltpu.roll(b, sh, 0)
            b = jnp.where(ok, a * b_sh + b, b)
            a = jnp.where(ok, a * a_sh, a)
        a_sc[:, sl] = a
        b_sc[:, sl] = b

    def group(gi, h):
        rows = pl.ds(pl.multiple_of(gi * SUBLANES, SUBLANES), SUBLANES)
        hh = a_sc[rows, :] * h + b_sc[rows, :]
        b_sc[rows, :] = hh
        return hh[SUBLANES - 1:SUBLANES, :]

    h_last = lax.fori_loop(0, tt // SUBLANES, group, h_sc[...])
    h_sc[...] = h_last
    hn_ref[...] = h_last
    g_ref[...] = (_gelu(yr_ref[...]) * b_sc[...]).astype(g_ref.dtype)


def _rnn_prompt(z, conv_w, conv_b, wgx, bgx, wga, bga, lam, past8, h0, *, d, tt):
    t = z.shape[0]
    full2 = lambda i: (0, 0)
    full3 = lambda i: (0, 0, 0)
    bw = d // RNN_BLOCKS
    return pl.pallas_call(
        _rnn_prompt_kernel,
        grid=(t // tt,),
        in_specs=[
            pl.BlockSpec((tt, d), lambda i: (i, 0)),
            pl.BlockSpec((tt, d), lambda i: (i, 1)),
            pl.BlockSpec((CONV_W, d), full2),
            pl.BlockSpec((1, d), full2),
            pl.BlockSpec((RNN_BLOCKS, bw, bw), full3),
            pl.BlockSpec((1, d), full2),
            pl.BlockSpec((RNN_BLOCKS, bw, bw), full3),
            pl.BlockSpec((1, d), full2),
            pl.BlockSpec((1, d), full2),
            pl.BlockSpec((SUBLANES, d), full2),
            pl.BlockSpec((1, d), full2),
        ],
        out_specs=[
            pl.BlockSpec((tt, d), lambda i: (i, 0)),
            pl.BlockSpec((SUBLANES, d), full2),
            pl.BlockSpec((1, d), full2),
        ],
        out_shape=[
            jax.ShapeDtypeStruct((t, d), MXU_DTYPE),
            jax.ShapeDtypeStruct((SUBLANES, d), F32),
            jax.ShapeDtypeStruct((1, d), F32),
        ],
        scratch_shapes=[pltpu.VMEM((SUBLANES, d), F32), pltpu.VMEM((1, d), F32),
                        pltpu.VMEM((tt, d), F32), pltpu.VMEM((tt, d), F32)],
        compiler_params=_params("arbitrary"),
        name="rnn_prompt",
    )(z, z, conv_w, conv_b, wgx, bgx, wga, bga, lam, past8, h0)


def _rnn_sample_kernel(z_ref, cw_ref, cb_ref, wgx_ref, bgx_ref, wga_ref, bga_ref, lam_ref,
                       cs_ref, h0_ref, g_ref, cn_ref, hn_ref, *, d, d_in, steps):
    bw = d // RNN_BLOCKS
    log_sig = _log_sigmoid(lam_ref[...])
    hist = [cs_ref[:, j * d:(j + 1) * d] for j in range(CONV_W - 1)]
    h = [h0_ref[:, n * bw:(n + 1) * bw] for n in range(RNN_BLOCKS)]
    for s in range(steps):
        x = z_ref[:, s * d_in:s * d_in + d]
        hist.append(x)
        xc = cb_ref[...]
        for j in range(CONV_W):
            xc = xc + cw_ref[j:j + 1, :] * hist[-1 - j]
        for n in range(RNN_BLOCKS):
            sl = slice(n * bw, (n + 1) * bw)
            a, b = _lru_gates(xc[:, sl], wgx_ref[n], bgx_ref[:, sl], wga_ref[n], bga_ref[:, sl],
                              log_sig[:, sl])
            h[n] = a * h[n] + b
            y = z_ref[:, s * d_in + d + n * bw:s * d_in + d + (n + 1) * bw]
            g_ref[:, s * d + n * bw:s * d + (n + 1) * bw] = (_gelu(y) * h[n]).astype(g_ref.dtype)
    for j in range(CONV_W - 1):
        cn_ref[:, j * d:(j + 1) * d] = hist[len(hist) - (CONV_W - 1) + j]
    for n in range(RNN_BLOCKS):
        hn_ref[:, n * bw:(n + 1) * bw] = h[n]


def _rnn_sample(z2, conv_w, conv_b, wgx, bgx, wga, bga, lam, conv_state, h0, *, d, steps):
    nb = z2.shape[0]
    d_in = z2.shape[1] // steps
    return pl.pallas_call(
        functools.partial(_rnn_sample_kernel, d=d, d_in=d_in, steps=steps),
        out_shape=[
            jax.ShapeDtypeStruct((nb, steps * d), MXU_DTYPE),
            jax.ShapeDtypeStruct((nb, (CONV_W - 1) * d), F32),
            jax.ShapeDtypeStruct((nb, d), F32),
        ],
        compiler_params=pltpu.CompilerParams(vmem_limit_bytes=VMEM_LIMIT_BYTES),
        name="rnn_sample",
    )(z2, conv_w, conv_b, wgx, bgx, wga, bga, lam, conv_state, h0)


def _attn_prompt_kernel(q_ref, kc_ref, kp_ref, vc_ref, vp_ref, o_ref, lse_ref):
    nblk = pl.program_id(1)
    qi = lax.broadcasted_iota(jnp.int32, (BAND, 2 * BAND), 0)
    ki = lax.broadcasted_iota(jnp.int32, (BAND, 2 * BAND), 1)
    first_key = jnp.where(nblk > 0, 0, BAND)
    mask = (ki >= qi) & (ki <= qi + BAND) & (ki >= first_key)
    lane = lax.broadcasted_iota(jnp.int32, (BAND, LANES), 1)
    lse_tile = jnp.zeros((BAND, LANES), F32)
    scale = HEAD_DIM ** -0.5
    for h in range(HEADS):
        sl = slice(h * HEAD_DIM, (h + 1) * HEAD_DIM)
        qh = (q_ref[:, sl] * scale).astype(MXU_DTYPE)
        kk = jnp.concatenate([kp_ref[:, sl], kc_ref[:, sl]], axis=0).astype(MXU_DTYPE)
        vv = jnp.concatenate([vp_ref[:, sl], vc_ref[:, sl]], axis=0).astype(MXU_DTYPE)
        s = lax.dot_general(qh, kk, (((1,), (1,)), ((), ())), preferred_element_type=F32)
        s = jnp.where(mask, s, NEG_INF)
        m = jnp.max(s, axis=-1, keepdims=True)
        p = jnp.exp(s - m)
        den = jnp.sum(p, axis=-1, keepdims=True)
        o = jnp.dot(p.astype(MXU_DTYPE), vv, preferred_element_type=F32)
        o_ref[:, sl] = o / den
        lse_tile = jnp.where(lane == h, m + jnp.log(den), lse_tile)
    lse_ref[...] = lse_tile


def _attn_prompt(z, gi, dil, *, d_in, col_q, col_k, col_v):
    t = z.shape[0]
    gw = HEADS * HEAD_DIM
    rows = t // dil
    nb = rows // BAND
    zr = z.reshape(rows, dil * d_in)
    per = d_in // gw
    cur = lambda c: (lambda r, n: (n, r * per + c + gi))
    prev = lambda c: (lambda r, n: (jnp.maximum(n - 1, 0), r * per + c + gi))
    o, lse = pl.pallas_call(
        _attn_prompt_kernel,
        grid=(dil, nb),
        in_specs=[
            pl.BlockSpec((BAND, gw), cur(col_q)),
            pl.BlockSpec((BAND, gw), cur(col_k)),
            pl.BlockSpec((BAND, gw), prev(col_k)),
            pl.BlockSpec((BAND, gw), cur(col_v)),
            pl.BlockSpec((BAND, gw), prev(col_v)),
        ],
        out_specs=[
            pl.BlockSpec((BAND, gw), lambda r, n: (n, r)),
            pl.BlockSpec((BAND, LANES), lambda r, n: (n, r)),
        ],
        out_shape=[
            jax.ShapeDtypeStruct((rows, dil * gw), F32),
            jax.ShapeDtypeStruct((rows, dil * LANES), F32),
        ],
        compiler_params=_params("parallel", "arbitrary"),
        name=f"attn_prompt_g{gi}",
    )(zr, zr, zr, zr, zr)
    return o.reshape(t, gw), lse.reshape(t, LANES)


def _attn_sample_kernel(q_ref, kn_ref, vn_ref, c_ref, nx_ref, o_ref, lse_ref, new_ref,
                        m_sc, den_sc, acc_sc, *, dil, steps):
    ci = pl.program_id(1)
    nchunks = pl.num_programs(1)
    rch = c_ref.shape[1]

    def attend(s, krows, vrows, valid, first):
        q = q_ref[0, s] * (HEAD_DIM ** -0.5)
        sc = jnp.sum(q[None] * krows, axis=-1, keepdims=True)
        if valid is not None:
            sc = jnp.where(valid, sc, NEG_INF)
        m_blk = jnp.max(sc, axis=0)
        if first:
            m_new = m_blk
            p = jnp.exp(sc - m_new[None])
            den = jnp.sum(p, axis=0)
            acc = jnp.sum(p * vrows, axis=0)
        else:
            m_old = m_sc[s][:, 0:1]
            m_new = jnp.maximum(m_old, m_blk)
            alpha = jnp.exp(m_old - m_new)
            p = jnp.exp(sc - m_new[None])
            den = alpha * den_sc[s][:, 0:1] + jnp.sum(p, axis=0)
            acc = alpha * acc_sc[s] + jnp.sum(p * vrows, axis=0)
        m_sc[s] = jnp.broadcast_to(m_new, (HEADS, HEAD_DIM))
        den_sc[s] = jnp.broadcast_to(den, (HEADS, HEAD_DIM))
        acc_sc[s] = acc

    @pl.when(ci == 0)
    def _():
        for s in range(steps):
            lo = 0 if dil == 1 else s
            attend(s, kn_ref[0, lo:s + 1], vn_ref[0, lo:s + 1], None, True)

    for s in range(steps):
        if dil == 1:
            ridx = lax.broadcasted_iota(jnp.int32, (rch, 1, 1), 0) + ci * rch
            attend(s, c_ref[0, :, 0], c_ref[0, :, 1], ridx >= s, False)
        else:
            n = rch // dil
            attend(s, c_ref[0, pl.ds(s, n, stride=dil), 0], c_ref[0, pl.ds(s, n, stride=dil), 1],
                   None, False)

    @pl.when(ci == nchunks - 1)
    def _():
        for s in range(steps):
            o_ref[0, s] = acc_sc[s] / den_sc[s]
            lse_ref[0, s] = m_sc[s] + jnp.log(den_sc[s])

    new_ref[0, 0:rch - steps] = c_ref[0, steps:rch]

    @pl.when(ci < nchunks - 1)
    def _():
        new_ref[0, rch - steps:rch] = nx_ref[0]

    @pl.when(ci == nchunks - 1)
    def _():
        new_ref[0, rch - steps:rch, 0] = kn_ref[0]
        new_ref[0, rch - steps:rch, 1] = vn_ref[0]


def _attn_sample(q, kn, vn, cache, dil, *, rch):
    nb, steps = q.shape[0], q.shape[1]
    lg = cache.shape[1]
    nchunks = lg // rch
    per = rch // steps
    last = lg // steps - 1
    small = pl.BlockSpec((1, steps, HEADS, HEAD_DIM), lambda b, c: (b, 0, 0, 0))
    big = pl.BlockSpec((1, rch, 2, HEADS, HEAD_DIM), lambda b, c: (b, c, 0, 0, 0))
    nxt = pl.BlockSpec((1, steps, 2, HEADS, HEAD_DIM),
                       lambda b, c: (b, jnp.minimum((c + 1) * per, last), 0, 0, 0))
    return pl.pallas_call(
        functools.partial(_attn_sample_kernel, dil=dil, steps=steps),
        grid=(nb, nchunks),
        in_specs=[small, small, small, big, nxt],
        out_specs=[small, small, big],
        out_shape=[
            jax.ShapeDtypeStruct(q.shape, F32),
            jax.ShapeDtypeStruct(q.shape, F32),
            jax.ShapeDtypeStruct(cache.shape, cache.dtype),
        ],
        scratch_shapes=[pltpu.VMEM((steps, HEADS, HEAD_DIM), F32)] * 3,
        compiler_params=_params("parallel", "arbitrary"),
        name=f"attn_sample_d{dil}",
    )(q, kn, vn, cache, cache)


def _merge_kernel(g_ref, o0_ref, o1_ref, o2_ref, l0_ref, l1_ref, l2_ref, ga_ref, gb_ref,
                  wr_ref, wa_ref, out_ref, att_sc):
    @pl.when(pl.program_id(1) == 0)
    def _():
        l0, l1, l2 = l0_ref[...], l1_ref[...], l2_ref[...]
        mx = jnp.maximum(jnp.maximum(l0, l1), l2)
        e0, e1, e2 = jnp.exp(l0 - mx), jnp.exp(l1 - mx), jnp.exp(l2 - mx)
        inv = 1.0 / (e0 + e1 + e2)
        w0, w1, w2 = e0 * inv, e1 * inv, e2 * inv
        for h in range(HEADS):
            sl = slice(h * HEAD_DIM, (h + 1) * HEAD_DIM)
            att = (w0[:, h:h + 1] * o0_ref[:, sl] + w1[:, h:h + 1] * o1_ref[:, sl]
                   + w2[:, h:h + 1] * o2_ref[:, sl])
            att_sc[:, sl] = att.astype(att_sc.dtype)

    rnn_out = jnp.dot(g_ref[...], wr_ref[...], preferred_element_type=F32)
    att_out = jnp.dot(att_sc[...], wa_ref[...], preferred_element_type=F32)
    mixed = jax.nn.sigmoid(ga_ref[...]) * rnn_out + jax.nn.sigmoid(gb_ref[...]) * att_out
    out_ref[...] = mixed.astype(out_ref.dtype)


def _merge(g, outs, lses, z, w_rnn, w_att, *, col_ga, col_gb, tm, tn):
    t, d = g.shape
    gw = HEADS * HEAD_DIM
    nj = d // tn
    row = lambda i, j: (i, 0)
    return pl.pallas_call(
        _merge_kernel,
        grid=(t // tm, nj),
        in_specs=[
            pl.BlockSpec((tm, d), row),
            pl.BlockSpec((tm, gw), row), pl.BlockSpec((tm, gw), row), pl.BlockSpec((tm, gw), row),
            pl.BlockSpec((tm, LANES), row), pl.BlockSpec((tm, LANES), row),
            pl.BlockSpec((tm, LANES), row),
            pl.BlockSpec((tm, tn), lambda i, j: (i, col_ga + j)),
            pl.BlockSpec((tm, tn), lambda i, j: (i, col_gb + j)),
            pl.BlockSpec((d, tn), lambda i, j: (0, j)),
            pl.BlockSpec((gw, tn), lambda i, j: (0, j)),
        ],
        out_specs=pl.BlockSpec((tm, tn), lambda i, j: (i, j)),
        out_shape=jax.ShapeDtypeStruct((t, d), MXU_DTYPE),
        scratch_shapes=[pltpu.VMEM((tm, gw), MXU_DTYPE)],
        compiler_params=_params("parallel", "arbitrary"),
        name="merge",
    )(g, *outs, *lses, z, z, w_rnn, w_att)


def _oproj_kernel(x_ref, mx_ref, wo_ref, gain_ref, h_ref, hn_ref, hnt_ref):
    h = x_ref[...] + jnp.dot(mx_ref[...], wo_ref[...], preferred_element_type=F32)
    h_ref[...] = h
    hn = _rms(h, gain_ref[...])
    hn_ref[...] = hn.astype(hn_ref.dtype)
    hnt_ref[...] = hn.T.astype(hnt_ref.dtype)


def _oproj(x, mixed, w_o, gain, *, tm):
    t, d = x.shape
    row = lambda i: (i, 0)
    return pl.pallas_call(
        _oproj_kernel,
        grid=(t // tm,),
        in_specs=[
            pl.BlockSpec((tm, d), row),
            pl.BlockSpec((tm, d), row),
            pl.BlockSpec((d, d), lambda i: (0, 0)),
            pl.BlockSpec((1, d), lambda i: (0, 0)),
        ],
        out_specs=[pl.BlockSpec((tm, d), row), pl.BlockSpec((tm, d), row),
                   pl.BlockSpec((d, tm), lambda i: (0, i))],
        out_shape=[jax.ShapeDtypeStruct((t, d), F32), jax.ShapeDtypeStruct((t, d), MXU_DTYPE),
                   jax.ShapeDtypeStruct((d, t), MXU_DTYPE)],
        compiler_params=_params("parallel"),
        name="oproj",
    )(x, mixed, w_o, gain)


def _take_top(vals, count):
    n, tm = vals.shape
    ridx = lax.broadcasted_iota(jnp.int32, (n, tm), 0).astype(F32)
    kidx = lax.broadcasted_iota(jnp.int32, (count, tm), 0)

    def body(k, carry):
        rest, top, member = carry
        m = jnp.max(rest, axis=0, keepdims=True)
        first = jnp.min(jnp.where(rest == m, ridx, float(n)), axis=0, keepdims=True)
        hit = ridx == first
        rest = jnp.where(hit, NEG_INF, rest)
        member = jnp.where(hit, 1.0, member)
        top = jnp.where(kidx == k, m, top)
        return rest, top, member

    init = (vals, jnp.zeros((count, tm), F32), jnp.zeros((n, tm), F32))
    _, top, member = lax.fori_loop(0, count, body, init)
    return top, member


def _route_kernel(hn_ref, wq_ref, sk_ref, th_ref, c_ref, s1_ref, e1_ref, q_sc):
    q = jnp.dot(hn_ref[...], wq_ref[...], preferred_element_type=F32)
    d_half = sk_ref.shape[2]
    for hp in range(2 * PEER_HEADS):
        q_sc[hp] = q[:, hp * d_half:(hp + 1) * d_half]
    nt = PEER_TOPK

    def head(h, _):
        def scores(half):
            return lax.dot_general(sk_ref[2 * h + half], q_sc[2 * h + half],
                                   (((1,), (1,)), ((), ())),
                                   preferred_element_type=F32)

        s0, s1 = scores(0), scores(1)
        top0, in0 = _take_top(s0, nt)
        top1, in1 = _take_top(s1, nt)
        cand = jnp.concatenate([top0[a:a + 1] + top1 for a in range(nt)], axis=0)
        best, _ = _take_top(cand, nt + 1)
        fmax = best[0:1]
        zsum = jnp.sum(jnp.exp(best[0:nt] - fmax), axis=0, keepdims=True)
        tau = 0.5 * (best[nt - 1:nt] + best[nt:nt + 1])
        th_ref[h] = jnp.where(in0 > 0.0, tau - s0, POS_INF)
        c_ref[h] = jnp.exp(s0 - top0[0:1]) / zsum
        s1_ref[h] = jnp.where(in1 > 0.0, s1, NEG_INF)
        e1_ref[h] = jnp.where(in1 > 0.0, jnp.exp(s1 - top1[0:1]), 0.0)
        return 0

    lax.fori_loop(0, PEER_HEADS, head, 0)


def _route(hn, w_query, sub_keys, *, tm):
    t, d = hn.shape
    dq = w_query.shape[1]
    slab = jax.ShapeDtypeStruct((PEER_HEADS, N_KEYS, t), F32)
    slab_spec = pl.BlockSpec((PEER_HEADS, N_KEYS, tm), lambda i: (0, 0, i))
    return pl.pallas_call(
        _route_kernel,
        grid=(t // tm,),
        in_specs=[
            pl.BlockSpec((tm, d), lambda i: (i, 0)),
            pl.BlockSpec((d, dq), lambda i: (0, 0)),
            pl.BlockSpec(sub_keys.shape, lambda i: (0, 0, 0)),
        ],
        out_specs=[slab_spec] * 4,
        out_shape=[slab] * 4,
        scratch_shapes=[pltpu.VMEM((2 * PEER_HEADS, tm, sub_keys.shape[2]), F32)],
        compiler_params=_params("parallel"),
        name="peer_route",
    )(hn, w_query, sub_keys)


def _peer_kernel(x_ref, u_ref, vt_ref, th_ref, c_ref, s1_ref, e1_ref, y_ref, coef_sc, *, ib):
    e = pl.program_id(1)
    tm = x_ref.shape[1]

    @pl.when(e == 0)
    def _():
        y_ref[...] = jnp.zeros_like(y_ref)

    act = jnp.dot(u_ref[...], x_ref[...], preferred_element_type=F32)
    assert ib == SUBLANES and tm % LANES == 0
    rows = pl.ds(pl.multiple_of(e * ib, SUBLANES), SUBLANES)
    for ii in range(ib):
        for lc in range(tm // LANES):
            ls = slice(lc * LANES, (lc + 1) * LANES)
            gate = jnp.zeros((N_KEYS, LANES), F32)
            for h in range(PEER_HEADS):
                th = th_ref[h, rows, ls][ii:ii + 1]
                ch = c_ref[h, rows, ls][ii:ii + 1]
                gate = gate + jnp.where(s1_ref[h, :, ls] >= th, e1_ref[h, :, ls] * ch, 0.0)
            a = act[ii * N_KEYS:(ii + 1) * N_KEYS, ls]
            coef_sc[ii * N_KEYS:(ii + 1) * N_KEYS, ls] = (_gelu(a) * gate).astype(coef_sc.dtype)
    y_ref[...] += jnp.dot(vt_ref[...], coef_sc[...], preferred_element_type=F32)


def _peer(hnt, u, vt, th, c, s1, e1, *, tm, ib):
    d, t = hnt.shape
    ne = u.shape[0]
    te = ib * N_KEYS
    slab_spec = pl.BlockSpec((PEER_HEADS, N_KEYS, tm), lambda i, e: (0, 0, i))
    return pl.pallas_call(
        functools.partial(_peer_kernel, ib=ib),
        grid=(t // tm, ne // te),
        in_specs=[
            pl.BlockSpec((d, tm), lambda i, e: (0, i)),
            pl.BlockSpec((te, d), lambda i, e: (e, 0)),
            pl.BlockSpec((d, te), lambda i, e: (0, e)),
            slab_spec, slab_spec, slab_spec, slab_spec,
        ],
        out_specs=pl.BlockSpec((d, tm), lambda i, e: (0, i)),
        out_shape=jax.ShapeDtypeStruct((d, t), F32),
        scratch_shapes=[pltpu.VMEM((te, tm), MXU_DTYPE)],
        compiler_params=_params("parallel", "arbitrary"),
        name="peer_experts",
    )(hnt, u, vt, th, c, s1, e1)


def _final_kernel(h_ref, yt_ref, gain_ref, out_ref):
    out_ref[...] = _rms(h_ref[...] + yt_ref[...].T, gain_ref[...])


def _final(h, yt, gain, *, tm):
    t, d = h.shape
    return pl.pallas_call(
        _final_kernel,
        grid=(t // tm,),
        in_specs=[pl.BlockSpec((tm, d), lambda i: (i, 0)), pl.BlockSpec((d, tm), lambda i: (0, i)),
                  pl.BlockSpec((1, d), lambda i: (0, 0))],
        out_specs=pl.BlockSpec((tm, d), lambda i: (i, 0)),
        out_shape=jax.ShapeDtypeStruct((t, d), F32),
        compiler_params=_params("parallel"),
        name="final_norm",
    )(h, yt, gain)


def _tile(n, want):
    return min(n, want)


def _token_stages(x2, z, g, outs, lses, w, cols):
    t = x2.shape[0]
    mixed = _merge(g, outs, lses, z, w["w_rnn_out"], w["w_att_out"], col_ga=cols["ga"],
                   col_gb=cols["gb"], tm=_tile(t, 256), tn=HEADS * HEAD_DIM)
    h, hn, hnt = _oproj(x2, mixed, w["w_o"], w["norm_ffn"], tm=_tile(t, 256))
    th, c, s1, e1 = _route(hn, w["w_query"], w["sub_keys"], tm=_tile(t, 256))
    yt = _peer(hnt, w["expert_u"], w["expert_vt"], th, c, s1, e1, tm=_tile(t, 512), ib=SUBLANES)
    return _final(h, yt, w["norm_final"], tm=_tile(t, 256))


def kernel(x_prompt, x_sample, state_conv, state_h, cache_kv_w128, cache_kv_w512, cache_kv_w2048,
           norm_mix, w_in, conv_w, conv_b, w_gate_x, b_gate_x, w_gate_a, b_gate_a, lru_lambda,
           w_rnn_out, w_att_out, w_o, norm_ffn, w_query, sub_keys, expert_u, expert_v, norm_final):
    depth = w_in.shape[0]
    assert depth == 1 and x_prompt.shape[0] == 1
    batch, seq, d = x_prompt.shape
    nb, steps, _ = x_sample.shape
    d_in = w_in.shape[2]
    gw = HEADS * HEAD_DIM
    d_qkv = N_GROUPS * gw
    assert d_in == 4 * d + 3 * d_qkv and d % gw == 0
    caches = (cache_kv_w128, cache_kv_w512, cache_kv_w2048)
    col_q, col_k, col_v = 2 * d // gw, (2 * d + d_qkv) // gw, (2 * d + 2 * d_qkv) // gw
    cols = {"ga": (2 * d + 3 * d_qkv) // gw, "gb": (3 * d + 3 * d_qkv) // gw}

    mx = lambda a: a.astype(MXU_DTYPE)
    row = lambda a: a.reshape(1, -1).astype(F32)
    w = {
        "w_in": mx(w_in[0]), "w_rnn_out": mx(w_rnn_out[0]), "w_att_out": mx(w_att_out[0]),
        "w_o": mx(w_o[0]), "w_query": mx(w_query[0]),
        "sub_keys": sub_keys[0].reshape(2 * PEER_HEADS, N_KEYS, -1),
        "expert_u": mx(expert_u[0]), "expert_vt": mx(expert_v[0]).T,
        "norm_ffn": row(norm_ffn[0]), "norm_final": row(norm_final),
    }
    rnn_w = (conv_w[0], row(conv_b[0]), mx(w_gate_x[0]), row(b_gate_x[0]), mx(w_gate_a[0]),
             row(b_gate_a[0]), row(lru_lambda[0]))
    gain_mix = row(norm_mix[0])

    xp = x_prompt.reshape(seq, d)
    zp = _inproj(xp, gain_mix, w["w_in"], tm=_tile(seq, 1024), tn=gw)
    g_p, past_p, h_p = _rnn_prompt(zp, *rnn_w, jnp.zeros((SUBLANES, d), F32),
                                   jnp.zeros((1, d), F32), d=d, tt=_tile(seq, 256))
    outs, lses, kv_p = [], [], []
    for gi, (win, dil) in enumerate(ATT_GROUPS):
        o, lse = _attn_prompt(zp, gi, dil, d_in=d_in, col_q=col_q, col_k=col_k, col_v=col_v)
        outs.append(o)
        lses.append(lse)
        keep = min(win, seq)
        kcol = (col_k + gi) * gw
        vcol = (col_v + gi) * gw
        kv = jnp.stack([zp[seq - keep:, kcol:kcol + gw], zp[seq - keep:, vcol:vcol + gw]], axis=1)
        kv_p.append(kv.reshape(1, 1, keep, 2, HEADS, HEAD_DIM))
    y_p = _token_stages(xp, zp, g_p, outs, lses, w, cols).reshape(batch, seq, d)
    conv_p = past_p[SUBLANES - (CONV_W - 1):].reshape(1, 1, CONV_W - 1, d)
    h_p = h_p.reshape(1, 1, d)

    ts = nb * steps
    xs = x_sample.reshape(ts, d)
    zs = _inproj(xs, gain_mix, w["w_in"], tm=ts, tn=gw)
    g_s, conv_s, h_s = _rnn_sample(zs.reshape(nb, steps * d_in), *rnn_w,
                                   state_conv[0].reshape(nb, (CONV_W - 1) * d), state_h[0],
                                   d=d, steps=steps)
    g_s = g_s.reshape(ts, d)
    qkv = zs[:, 2 * d:2 * d + 3 * d_qkv].reshape(nb, steps, 3, N_GROUPS, HEADS, HEAD_DIM)
    outs, lses, kv_s = [], [], []
    for gi, (win, dil) in enumerate(ATT_GROUPS):
        cache = caches[gi][0]
        o, lse, new_cache = _attn_sample(qkv[:, :, 0, gi], qkv[:, :, 1, gi], qkv[:, :, 2, gi], cache,
                                         dil, rch=min(cache.shape[1], 512))
        outs.append(o.reshape(ts, gw))
        lse = lse[..., 0].reshape(ts, HEADS)
        lses.append(jnp.pad(lse, ((0, 0), (0, LANES - HEADS))))
        kv_s.append(new_cache[None])
    y_s = _token_stages(xs, zs, g_s, outs, lses, w, cols).reshape(nb, steps, d)
    conv_s = conv_s.reshape(1, nb, CONV_W - 1, d)
    h_s = h_s.reshape(1, nb, d)

    return (y_p, y_s, conv_p, h_p, kv_p[0], kv_p[1], kv_p[2],
            conv_s, h_s, kv_s[0], kv_s[1], kv_s[2])
```

```python
import functools
import math

import jax
import jax.numpy as jnp
from jax import lax
from jax.experimental import pallas as pl
from jax.experimental.pallas import tpu as pltpu

F32 = jnp.float32
MXU_DTYPE = jnp.bfloat16

RNN_BLOCKS = 8
CONV_W = 4
LRU_C = 8.0
ATT_GROUPS = ((128, 1), (512, 4), (2048, 16))
N_GROUPS = len(ATT_GROUPS)
HEADS = 8
HEAD_DIM = 128
BAND = 128
PEER_HEADS = 8
N_KEYS = 128
PEER_TOPK = 16
NORM_EPS = 1e-6

LANES = 128
SUBLANES = 8
VMEM_LIMIT_BYTES = 56 * 1024 * 1024

NEG_INF = float("-inf")
POS_INF = float("inf")


def _params(*sem):
    return pltpu.CompilerParams(dimension_semantics=sem, vmem_limit_bytes=VMEM_LIMIT_BYTES)


def _gelu(x):
    return 0.5 * x * (1.0 + lax.erf(x * math.sqrt(0.5)))


def _rms(x, gain):
    ms = jnp.mean(x * x, axis=-1, keepdims=True)
    return x * lax.rsqrt(ms + NORM_EPS) * gain


def _inproj_kernel(x_ref, g_ref, w_ref, z_ref, xn_sc):
    @pl.when(pl.program_id(1) == 0)
    def _():
        xn_sc[...] = _rms(x_ref[...], g_ref[...]).astype(MXU_DTYPE)

    z_ref[...] = jnp.dot(xn_sc[...], w_ref[...], preferred_element_type=F32)


def _inproj(x, gain, w, *, tm, tn):
    m, k = x.shape
    n = w.shape[1]
    return pl.pallas_call(
        _inproj_kernel,
        grid=(m // tm, n // tn),
        in_specs=[
            pl.BlockSpec((tm, k), lambda i, j: (i, 0)),
            pl.BlockSpec((1, k), lambda i, j: (0, 0)),
            pl.BlockSpec((k, tn), lambda i, j: (0, j)),
        ],
        out_specs=pl.BlockSpec((tm, tn), lambda i, j: (i, j)),
        out_shape=jax.ShapeDtypeStruct((m, n), F32),
        scratch_shapes=[pltpu.VMEM((tm, k), MXU_DTYPE)],
        compiler_params=_params("parallel", "arbitrary"),
        name="inproj",
    )(x, gain, w)


def _lru_gates(xc, wgx, bgx, wga, bga, log_sig):
    xb = xc.astype(MXU_DTYPE)
    gx = jax.nn.sigmoid(jnp.dot(xb, wgx, preferred_element_type=F32) + bgx)
    ga = jax.nn.sigmoid(jnp.dot(xb, wga, preferred_element_type=F32) + bga)
    a = jnp.exp(LRU_C * ga * log_sig)
    b = jnp.sqrt(1.0 - a * a) * gx * xc
    return a, b


def _log_sigmoid(x):
    return -(jnp.maximum(-x, 0.0) + jnp.log1p(jnp.exp(-jnp.abs(x))))


def _rnn_prompt_kernel(xr_ref, yr_ref, cw_ref, cb_ref, wgx_ref, bgx_ref, wga_ref, bga_ref, lam_ref,
                       p0_ref, h0_ref, g_ref, pn_ref, hn_ref, p_sc, h_sc, a_sc, b_sc):
    tt, d = xr_ref.shape
    bw = d // RNN_BLOCKS

    @pl.when(pl.program_id(0) == 0)
    def _():
        p_sc[...] = p0_ref[...]
        h_sc[...] = h0_ref[...]

    x = xr_ref[...]
    past = p_sc[...]
    row8 = lax.broadcasted_iota(jnp.int32, (SUBLANES, d), 0)
    xc = cb_ref[...] + cw_ref[0:1, :] * x
    for j in range(1, CONV_W):
        xs = pltpu.roll(x, j, 0)
        head = jnp.where(row8 < j, pltpu.roll(past, j, 0), xs[:SUBLANES])
        xs = jnp.concatenate([head, xs[SUBLANES:]], axis=0)
        xc = xc + cw_ref[j:j + 1, :] * xs
    p_sc[...] = x[tt - SUBLANES:]
    pn_ref[...] = x[tt - SUBLANES:]

    log_sig = _log_sigmoid(lam_ref[...])
    rowm = lax.broadcasted_iota(jnp.int32, (tt, bw), 0) & (SUBLANES - 1)
    for n in range(RNN_BLOCKS):
        sl = slice(n * bw, (n + 1) * bw)
        a, b = _lru_gates(xc[:, sl], wgx_ref[n], bgx_ref[:, sl], wga_ref[n], bga_ref[:, sl],
                          log_sig[:, sl])
        for sh in (1, 2, 4):
            ok = rowm >= sh
            a_sh = pltpu.roll(a, sh, 0)
            b_sh = pltpu.roll(b, sh, 0)
            b = jnp.where(ok, a * b_sh + b, b)
            a = jnp.where(ok, a * a_sh, a)
        a_sc[:, sl] = a
        b_sc[:, sl] = b

    def group(gi, h):
        rows = pl.ds(pl.multiple_of(gi * SUBLANES, SUBLANES), SUBLANES)
        hh = a_sc[rows, :] * h + b_sc[rows, :]
        b_sc[rows, :] = hh
        return hh[SUBLANES - 1:SUBLANES, :]

    h_last = lax.fori_loop(0, tt // SUBLANES, group, h_sc[...])
    h_sc[...] = h_last
    hn_ref[...] = h_last
    g_ref[...] = (_gelu(yr_ref[...]) * b_sc[...]).astype(g_ref.dtype)


def _rnn_prompt(z, conv_w, conv_b, wgx, bgx, wga, bga, lam, past8, h0, *, d, tt):
    t = z.shape[0]
    full2 = lambda i: (0, 0)
    full3 = lambda i: (0, 0, 0)
    bw = d // RNN_BLOCKS
    return pl.pallas_call(
        _rnn_prompt_kernel,
        grid=(t // tt,),
        in_specs=[
            pl.BlockSpec((tt, d), lambda i: (i, 0)),
            pl.BlockSpec((tt, d), lambda i: (i, 1)),
            pl.BlockSpec((CONV_W, d), full2),
            pl.BlockSpec((1, d), full2),
            pl.BlockSpec((RNN_BLOCKS, bw, bw), full3),
            pl.BlockSpec((1, d), full2),
            pl.BlockSpec((RNN_BLOCKS, bw, bw), full3),
            pl.BlockSpec((1, d), full2),
            pl.BlockSpec((1, d), full2),
            pl.BlockSpec((SUBLANES, d), full2),
            pl.BlockSpec((1, d), full2),
        ],
        out_specs=[
            pl.BlockSpec((tt, d), lambda i: (i, 0)),
            pl.BlockSpec((SUBLANES, d), full2),
            pl.BlockSpec((1, d), full2),
        ],
        out_shape=[
            jax.ShapeDtypeStruct((t, d), MXU_DTYPE),
            jax.ShapeDtypeStruct((SUBLANES, d), F32),
            jax.ShapeDtypeStruct((1, d), F32),
        ],
        scratch_shapes=[pltpu.VMEM((SUBLANES, d), F32), pltpu.VMEM((1, d), F32),
                        pltpu.VMEM((tt, d), F32), pltpu.VMEM((tt, d), F32)],
        compiler_params=_params("arbitrary"),
        name="rnn_prompt",
    )(z, z, conv_w, conv_b, wgx, bgx, wga, bga, lam, past8, h0)


def _rnn_sample_kernel(z_ref, cw_ref, cb_ref, wgx_ref, bgx_ref, wga_ref, bga_ref, lam_ref,
                       cs_ref, h0_ref, g_ref, cn_ref, hn_ref, *, d, d_in, steps):
    bw = d // RNN_BLOCKS
    log_sig = _log_sigmoid(lam_ref[...])
    hist = [cs_ref[:, j * d:(j + 1) * d] for j in range(CONV_W - 1)]
    h = [h0_ref[:, n * bw:(n + 1) * bw] for n in range(RNN_BLOCKS)]
    for s in range(steps):
        x = z_ref[:, s * d_in:s * d_in + d]
        hist.append(x)
        xc = cb_ref[...]
        for j in range(CONV_W):
            xc = xc + cw_ref[j:j + 1, :] * hist[-1 - j]
        for n in range(RNN_BLOCKS):
            sl = slice(n * bw, (n + 1) * bw)
            a, b = _lru_gates(xc[:, sl], wgx_ref[n], bgx_ref[:, sl], wga_ref[n], bga_ref[:, sl],
                              log_sig[:, sl])
            h[n] = a * h[n] + b
            y = z_ref[:, s * d_in + d + n * bw:s * d_in + d + (n + 1) * bw]
            g_ref[:, s * d + n * bw:s * d + (n + 1) * bw] = (_gelu(y) * h[n]).astype(g_ref.dtype)
    for j in range(CONV_W - 1):
        cn_ref[:, j * d:(j + 1) * d] = hist[len(hist) - (CONV_W - 1) + j]
    for n in range(RNN_BLOCKS):
        hn_ref[:, n * bw:(n + 1) * bw] = h[n]


def _rnn_sample(z2, conv_w, conv_b, wgx, bgx, wga, bga, lam, conv_state, h0, *, d, steps):
    nb = z2.shape[0]
    d_in = z2.shape[1] // steps
    return pl.pallas_call(
        functools.partial(_rnn_sample_kernel, d=d, d_in=d_in, steps=steps),
        out_shape=[
            jax.ShapeDtypeStruct((nb, steps * d), MXU_DTYPE),
            jax.ShapeDtypeStruct((nb, (CONV_W - 1) * d), F32),
            jax.ShapeDtypeStruct((nb, d), F32),
        ],
        compiler_params=pltpu.CompilerParams(vmem_limit_bytes=VMEM_LIMIT_BYTES),
        name="rnn_sample",
    )(z2, conv_w, conv_b, wgx, bgx, wga, bga, lam, conv_state, h0)


def _attn_prompt_kernel(q_ref, kc_ref, kp_ref, vc_ref, vp_ref, o_ref, lse_ref, *, dil, hpb):
    nblk = pl.program_id(0)
    hchunk = pl.program_id(1)
    qi = lax.broadcasted_iota(jnp.int32, (BAND, 2 * BAND), 0)
    ki = lax.broadcasted_iota(jnp.int32, (BAND, 2 * BAND), 1)
    first_key = jnp.where(nblk > 0, 0, BAND)
    mask = (ki >= qi) & (ki <= qi + BAND) & (ki >= first_key)
    lane = lax.broadcasted_iota(jnp.int32, (BAND, LANES), 1)
    scale = HEAD_DIM ** -0.5

    @pl.when(hchunk == 0)
    def _():
        lse_ref[...] = jnp.zeros_like(lse_ref)

    for r in range(dil):
        rows = pl.ds(r, BAND, stride=dil) if dil > 1 else slice(None)
        lse_tile = lse_ref[rows, :]
        for hh in range(hpb):
            sl = slice(hh * HEAD_DIM, (hh + 1) * HEAD_DIM)
            qh = (q_ref[rows, sl] * scale).astype(MXU_DTYPE)
            kk = jnp.concatenate([kp_ref[rows, sl], kc_ref[rows, sl]], axis=0).astype(MXU_DTYPE)
            vv = jnp.concatenate([vp_ref[rows, sl], vc_ref[rows, sl]], axis=0).astype(MXU_DTYPE)
            s = lax.dot_general(qh, kk, (((1,), (1,)), ((), ())), preferred_element_type=F32)
            s = jnp.where(mask, s, NEG_INF)
            m = jnp.max(s, axis=-1, keepdims=True)
            p = jnp.exp(s - m)
            den = jnp.sum(p, axis=-1, keepdims=True)
            o = jnp.dot(p.astype(MXU_DTYPE), vv, preferred_element_type=F32)
            o_ref[rows, sl] = o / den
            lse_tile = jnp.where(lane == hchunk * hpb + hh, m + jnp.log(den), lse_tile)
        lse_ref[rows, :] = lse_tile


def _attn_prompt(z, gi, dil, *, col_q, col_k, col_v):
    t = z.shape[0]
    gw = HEADS * HEAD_DIM
    hpb = HEADS if dil == 1 else 1
    cw = hpb * HEAD_DIM
    rb = BAND * dil
    per = gw // cw
    cur = lambda c: (lambda n, h: (n, (c + gi) * per + h))
    prev = lambda c: (lambda n, h: (jnp.maximum(n - 1, 0), (c + gi) * per + h))
    return pl.pallas_call(
        functools.partial(_attn_prompt_kernel, dil=dil, hpb=hpb),
        grid=(t // rb, HEADS // hpb),
        in_specs=[
            pl.BlockSpec((rb, cw), cur(col_q)),
            pl.BlockSpec((rb, cw), cur(col_k)),
            pl.BlockSpec((rb, cw), prev(col_k)),
            pl.BlockSpec((rb, cw), cur(col_v)),
            pl.BlockSpec((rb, cw), prev(col_v)),
        ],
        out_specs=[
            pl.BlockSpec((rb, cw), lambda n, h: (n, h)),
            pl.BlockSpec((rb, LANES), lambda n, h: (n, 0)),
        ],
        out_shape=[
            jax.ShapeDtypeStruct((t, gw), F32),
            jax.ShapeDtypeStruct((t, LANES), F32),
        ],
        compiler_params=_params("parallel", "arbitrary"),
        name=f"attn_prompt_g{gi}",
    )(z, z, z, z, z)


def _attn_sample_kernel(q_ref, kn_ref, vn_ref, c_ref, nx_ref, o_ref, lse_ref, new_ref,
                        m_sc, den_sc, acc_sc, *, dil, steps):
    ci = pl.program_id(1)
    nchunks = pl.num_programs(1)
    rch = c_ref.shape[1]

    def attend(s, krows, vrows, valid, first):
        q = q_ref[0, s] * (HEAD_DIM ** -0.5)
        sc = jnp.sum(q[None] * krows, axis=-1, keepdims=True)
        if valid is not None:
            sc = jnp.where(valid, sc, NEG_INF)
        m_blk = jnp.max(sc, axis=0)
        if first:
            m_new = m_blk
            p = jnp.exp(sc - m_new[None])
            den = jnp.sum(p, axis=0)
            acc = jnp.sum(p * vrows, axis=0)
        else:
            m_old = m_sc[s][:, 0:1]
            m_new = jnp.maximum(m_old, m_blk)
            alpha = jnp.exp(m_old - m_new)
            p = jnp.exp(sc - m_new[None])
            den = alpha * den_sc[s][:, 0:1] + jnp.sum(p, axis=0)
            acc = alpha * acc_sc[s] + jnp.sum(p * vrows, axis=0)
        m_sc[s] = jnp.broadcast_to(m_new, (HEADS, HEAD_DIM))
        den_sc[s] = jnp.broadcast_to(den, (HEADS, HEAD_DIM))
        acc_sc[s] = acc

    @pl.when(ci == 0)
    def _():
        for s in range(steps):
            lo = 0 if dil == 1 else s
            attend(s, kn_ref[0, lo:s + 1], vn_ref[0, lo:s + 1], None, True)

    for s in range(steps):
        if dil == 1:
            ridx = lax.broadcasted_iota(jnp.int32, (rch, 1, 1), 0) + ci * rch
            attend(s, c_ref[0, :, 0], c_ref[0, :, 1], ridx >= s, False)
        else:
            n = rch // dil
            attend(s, c_ref[0, pl.ds(s, n, stride=dil), 0], c_ref[0, pl.ds(s, n, stride=dil), 1],
                   None, False)

    @pl.when(ci == nchunks - 1)
    def _():
        for s in range(steps):
            o_ref[0, s] = acc_sc[s] / den_sc[s]
            lse_ref[0, s] = m_sc[s] + jnp.log(den_sc[s])

    new_ref[0, 0:rch - steps] = c_ref[0, steps:rch]

    @pl.when(ci < nchunks - 1)
    def _():
        new_ref[0, rch - steps:rch] = nx_ref[0]

    @pl.when(ci == nchunks - 1)
    def _():
        new_ref[0, rch - steps:rch, 0] = kn_ref[0]
        new_ref[0, rch - steps:rch, 1] = vn_ref[0]


def _attn_sample(q, kn, vn, cache, dil, *, rch):
    nb, steps = q.shape[0], q.shape[1]
    lg = cache.shape[1]
    nchunks = lg // rch
    per = rch // steps
    last = lg // steps - 1
    small = pl.BlockSpec((1, steps, HEADS, HEAD_DIM), lambda b, c: (b, 0, 0, 0))
    big = pl.BlockSpec((1, rch, 2, HEADS, HEAD_DIM), lambda b, c: (b, c, 0, 0, 0))
    nxt = pl.BlockSpec((1, steps, 2, HEADS, HEAD_DIM),
                       lambda b, c: (b, jnp.minimum((c + 1) * per, last), 0, 0, 0))
    return pl.pallas_call(
        functools.partial(_attn_sample_kernel, dil=dil, steps=steps),
        grid=(nb, nchunks),
        in_specs=[small, small, small, big, nxt],
        out_specs=[small, small, big],
        out_shape=[
            jax.ShapeDtypeStruct(q.shape, F32),
            jax.ShapeDtypeStruct(q.shape, F32),
            jax.ShapeDtypeStruct(cache.shape, cache.dtype),
        ],
        scratch_shapes=[pltpu.VMEM((steps, HEADS, HEAD_DIM), F32)] * 3,
        compiler_params=_params("parallel", "arbitrary"),
        name=f"attn_sample_d{dil}",
    )(q, kn, vn, cache, cache)


def _merge_kernel(g_ref, o0_ref, o1_ref, o2_ref, l0_ref, l1_ref, l2_ref, ga_ref, gb_ref,
                  wr_ref, wa_ref, out_ref, att_sc):
    @pl.when(pl.program_id(1) == 0)
    def _():
        l0, l1, l2 = l0_ref[...], l1_ref[...], l2_ref[...]
        mx = jnp.maximum(jnp.maximum(l0, l1), l2)
        e0, e1, e2 = jnp.exp(l0 - mx), jnp.exp(l1 - mx), jnp.exp(l2 - mx)
        inv = 1.0 / (e0 + e1 + e2)
        w0, w1, w2 = e0 * inv, e1 * inv, e2 * inv
        for h in range(HEADS):
            sl = slice(h * HEAD_DIM, (h + 1) * HEAD_DIM)
            att = (w0[:, h:h + 1] * o0_ref[:, sl] + w1[:, h:h + 1] * o1_ref[:, sl]
                   + w2[:, h:h + 1] * o2_ref[:, sl])
            att_sc[:, sl] = att.astype(att_sc.dtype)

    rnn_out = jnp.dot(g_ref[...], wr_ref[...], preferred_element_type=F32)
    att_out = jnp.dot(att_sc[...], wa_ref[...], preferred_element_type=F32)
    mixed = jax.nn.sigmoid(ga_ref[...]) * rnn_out + jax.nn.sigmoid(gb_ref[...]) * att_out
    out_ref[...] = mixed.astype(out_ref.dtype)


def _merge(g, outs, lses, z, w_rnn, w_att, *, col_ga, col_gb, tm, tn):
    t, d = g.shape
    gw = HEADS * HEAD_DIM
    nj = d // tn
    row = lambda i, j: (i, 0)
    return pl.pallas_call(
        _merge_kernel,
        grid=(t // tm, nj),
        in_specs=[
            pl.BlockSpec((tm, d), row),
            pl.BlockSpec((tm, gw), row), pl.BlockSpec((tm, gw), row), pl.BlockSpec((tm, gw), row),
            pl.BlockSpec((tm, LANES), row), pl.BlockSpec((tm, LANES), row),
            pl.BlockSpec((tm, LANES), row),
            pl.BlockSpec((tm, tn), lambda i, j: (i, col_ga + j)),
            pl.BlockSpec((tm, tn), lambda i, j: (i, col_gb + j)),
            pl.BlockSpec((d, tn), lambda i, j: (0, j)),
            pl.BlockSpec((gw, tn), lambda i, j: (0, j)),
        ],
        out_specs=pl.BlockSpec((tm, tn), lambda i, j: (i, j)),
        out_shape=jax.ShapeDtypeStruct((t, d), MXU_DTYPE),
        scratch_shapes=[pltpu.VMEM((tm, gw), MXU_DTYPE)],
        compiler_params=_params("parallel", "arbitrary"),
        name="merge",
    )(g, *outs, *lses, z, z, w_rnn, w_att)


def _oproj_kernel(x_ref, mx_ref, wo_ref, gain_ref, h_ref, hn_ref, hnt_ref):
    h = x_ref[...] + jnp.dot(mx_ref[...], wo_ref[...], preferred_element_type=F32)
    h_ref[...] = h
    hn = _rms(h, gain_ref[...])
    hn_ref[...] = hn.astype(hn_ref.dtype)
    hnt_ref[...] = hn.T.astype(hnt_ref.dtype)


def _oproj(x, mixed, w_o, gain, *, tm):
    t, d = x.shape
    row = lambda i: (i, 0)
    return pl.pallas_call(
        _oproj_kernel,
        grid=(t // tm,),
        in_specs=[
            pl.BlockSpec((tm, d), row),
            pl.BlockSpec((tm, d), row),
            pl.BlockSpec((d, d), lambda i: (0, 0)),
            pl.BlockSpec((1, d), lambda i: (0, 0)),
        ],
        out_specs=[pl.BlockSpec((tm, d), row), pl.BlockSpec((tm, d), row),
                   pl.BlockSpec((d, tm), lambda i: (0, i))],
        out_shape=[jax.ShapeDtypeStruct((t, d), F32), jax.ShapeDtypeStruct((t, d), MXU_DTYPE),
                   jax.ShapeDtypeStruct((d, t), MXU_DTYPE)],
        compiler_params=_params("parallel"),
        name="oproj",
    )(x, mixed, w_o, gain)


def _all_sublanes(x, op):
    for sh in (4, 2, 1):
        x = op(x, pltpu.roll(x, sh, 0))
    return x


def _take_ranked(vals, order, count):
    rest = vals
    rank = jnp.full(vals.shape, float(count), F32)
    tops, firsts = [], []
    for k in range(count):
        m = _all_sublanes(jnp.max(rest, axis=0), jnp.maximum)
        first = _all_sublanes(jnp.min(jnp.where(rest == m[None], order, POS_INF), axis=0),
                              jnp.minimum)
        hit = order == first[None]
        rest = jnp.where(hit, NEG_INF, rest)
        rank = jnp.where(hit, float(k), rank)
        tops.append(m)
        firsts.append(first)
    return tops, firsts, rank


def _route_lanes(s0, s1):
    nt = PEER_TOPK
    nk, ln = s0.shape
    nv = nk // SUBLANES
    s0 = s0.reshape(nv, SUBLANES, ln)
    s1 = s1.reshape(nv, SUBLANES, ln)
    key = (lax.broadcasted_iota(jnp.int32, (nv, SUBLANES, ln), 0) * SUBLANES
           + lax.broadcasted_iota(jnp.int32, (nv, SUBLANES, ln), 1)).astype(F32)
    top0, _, rank0 = _take_ranked(s0, key, nt)
    top1, _, rank1 = _take_ranked(s1, key, nt)

    sub = lax.broadcasted_iota(jnp.int32, (SUBLANES, ln), 0)
    subf = sub.astype(F32)

    def on_sublanes(rows):
        out = rows[0]
        for b in range(1, SUBLANES):
            out = jnp.where(sub == b, rows[b], out)
        return out

    assert nt == 2 * SUBLANES
    t1lo, t1hi = on_sublanes(top1[:SUBLANES]), on_sublanes(top1[SUBLANES:])
    t0hi = on_sublanes(top0[SUBLANES:])
    vals = [top0[0] + t1lo, top0[0] + t1hi]
    poss = [subf, subf + SUBLANES]
    for a in range(1, SUBLANES):
        vals.append(jnp.where(sub < nt // (a + 1), top0[a] + t1lo, NEG_INF))
        poss.append(subf + a * nt)
    vals.append(t0hi + top1[0])
    poss.append((subf + SUBLANES) * nt)
    best, bpos, _ = _take_ranked(jnp.stack(vals), jnp.stack(poss), nt)

    zsum = jnp.zeros((SUBLANES, ln), F32)
    cnt_lo = jnp.zeros((SUBLANES, ln), F32)
    cnt_hi = jnp.zeros((SUBLANES, ln), F32)
    for k in range(nt):
        zsum = zsum + jnp.exp(best[k] - best[0])
        a_k = jnp.floor(bpos[k] * (1.0 / nt))
        cnt_lo = cnt_lo + jnp.where(subf == a_k, 1.0, 0.0)
        cnt_hi = cnt_hi + jnp.where(subf + SUBLANES == a_k, 1.0, 0.0)

    th = jnp.full((nv, SUBLANES, ln), POS_INF, F32)
    for a in range(nt):
        cnt = cnt_lo if a < SUBLANES else cnt_hi
        n_a = jnp.broadcast_to(cnt[a % SUBLANES:a % SUBLANES + 1], (SUBLANES, ln))
        th = jnp.where(rank0 == float(a), (1.0 - n_a)[None], th)
    c = jnp.exp(s0 - top0[0][None]) / zsum[None]
    r1 = jnp.where(rank1 < float(nt), -rank1, NEG_INF)
    e1 = jnp.exp(s1 - top1[0][None])
    return tuple(x.reshape(nk, ln) for x in (th, c, r1, e1))


def _route_kernel(hn_ref, wq_ref, sk_ref, th_ref, c_ref, r1_ref, e1_ref, q_sc):
    q = jnp.dot(hn_ref[...], wq_ref[...], preferred_element_type=F32)
    tm = hn_ref.shape[0]
    d_half = sk_ref.shape[2]
    for hp in range(2 * PEER_HEADS):
        q_sc[hp] = q[:, hp * d_half:(hp + 1) * d_half]

    def head(h, _):
        def scores(half):
            return lax.dot_general(sk_ref[2 * h + half], q_sc[2 * h + half],
                                   (((1,), (1,)), ((), ())),
                                   preferred_element_type=F32)

        s0, s1 = scores(0), scores(1)
        for lc in range(tm // LANES):
            ls = slice(lc * LANES, (lc + 1) * LANES)
            th, c, r1, e1 = _route_lanes(s0[:, ls], s1[:, ls])
            th_ref[h, :, ls] = th
            c_ref[h, :, ls] = c
            r1_ref[h, :, ls] = r1
            e1_ref[h, :, ls] = e1
        return 0

    lax.fori_loop(0, PEER_HEADS, head, 0)


def _route(hn, w_query, sub_keys, *, tm):
    t, d = hn.shape
    dq = w_query.shape[1]
    slab = jax.ShapeDtypeStruct((PEER_HEADS, N_KEYS, t), F32)
    slab_spec = pl.BlockSpec((PEER_HEADS, N_KEYS, tm), lambda i: (0, 0, i))
    return pl.pallas_call(
        _route_kernel,
        grid=(t // tm,),
        in_specs=[
            pl.BlockSpec((tm, d), lambda i: (i, 0)),
            pl.BlockSpec((d, dq), lambda i: (0, 0)),
            pl.BlockSpec(sub_keys.shape, lambda i: (0, 0, 0)),
        ],
        out_specs=[slab_spec] * 4,
        out_shape=[slab] * 4,
        scratch_shapes=[pltpu.VMEM((2 * PEER_HEADS, tm, sub_keys.shape[2]), F32)],
        compiler_params=_params("parallel"),
        name="peer_route",
    )(hn, w_query, sub_keys)


PEER_ROW_CHUNK = 16


def _peer_kernel(x_ref, u_ref, vt_ref, th_ref, c_ref, r1_ref, e1_ref, y_ref, coef_sc):
    e = pl.program_id(1)
    tm = x_ref.shape[1]
    ib = th_ref.shape[1]
    assert tm % LANES == 0 and N_KEYS % PEER_ROW_CHUNK == 0

    @pl.when(e == 0)
    def _():
        y_ref[...] = jnp.zeros_like(y_ref)

    act = jnp.dot(u_ref[...], x_ref[...], preferred_element_type=F32)
    for ii in range(ib):
        for lc in range(tm // LANES):
            ls = slice(lc * LANES, (lc + 1) * LANES)
            gate = jnp.zeros((N_KEYS, LANES), F32)
            for h in range(PEER_HEADS):
                th = th_ref[h, ii:ii + 1, ls]
                ch = c_ref[h, ii:ii + 1, ls]
                gate = gate + jnp.where(r1_ref[h, :, ls] >= th, e1_ref[h, :, ls] * ch, 0.0)
            a = act[ii * N_KEYS:(ii + 1) * N_KEYS, ls]
            coef_sc[ii * N_KEYS:(ii + 1) * N_KEYS, ls] = (_gelu(a) * gate).astype(coef_sc.dtype)
    y_ref[...] += jnp.dot(vt_ref[...], coef_sc[...], preferred_element_type=F32)


def _peer(hnt, u, vt, th, c, r1, e1, *, tm, ib):
    d, t = hnt.shape
    ne = u.shape[0]
    te = ib * N_KEYS
    slab_spec = pl.BlockSpec((PEER_HEADS, N_KEYS, tm), lambda i, e: (0, 0, i))
    first_spec = pl.BlockSpec((PEER_HEADS, ib, tm), lambda i, e: (0, e, i))
    return pl.pallas_call(
        _peer_kernel,
        grid=(t // tm, ne // te),
        in_specs=[
            pl.BlockSpec((d, tm), lambda i, e: (0, i)),
            pl.BlockSpec((te, d), lambda i, e: (e, 0)),
            pl.BlockSpec((d, te), lambda i, e: (0, e)),
            first_spec, first_spec, slab_spec, slab_spec,
        ],
        out_specs=pl.BlockSpec((d, tm), lambda i, e: (0, i)),
        out_shape=jax.ShapeDtypeStruct((d, t), F32),
        scratch_shapes=[pltpu.VMEM((te, tm), MXU_DTYPE)],
        compiler_params=_params("parallel", "arbitrary"),
        name="peer_experts",
    )(hnt, u, vt, th, c, r1, e1)


def _final_kernel(h_ref, yt_ref, gain_ref, out_ref):
    out_ref[...] = _rms(h_ref[...] + yt_ref[...].T, gain_ref[...])


def _final(h, yt, gain, *, tm):
    t, d = h.shape
    return pl.pallas_call(
        _final_kernel,
        grid=(t // tm,),
        in_specs=[pl.BlockSpec((tm, d), lambda i: (i, 0)), pl.BlockSpec((d, tm), lambda i: (0, i)),
                  pl.BlockSpec((1, d), lambda i: (0, 0))],
        out_specs=pl.BlockSpec((tm, d), lambda i: (i, 0)),
        out_shape=jax.ShapeDtypeStruct((t, d), F32),
        compiler_params=_params("parallel"),
        name="final_norm",
    )(h, yt, gain)


def _tile(n, want):
    return min(n, want)


def _token_stages(x2, z, g, outs, lses, w, cols):
    t = x2.shape[0]
    mixed = _merge(g, outs, lses, z, w["w_rnn_out"], w["w_att_out"], col_ga=cols["ga"],
                   col_gb=cols["gb"], tm=_tile(t, 256), tn=HEADS * HEAD_DIM)
    h, hn, hnt = _oproj(x2, mixed, w["w_o"], w["norm_ffn"], tm=_tile(t, 256))
    th, c, s1, e1 = _route(hn, w["w_query"], w["sub_keys"], tm=_tile(t, 256))
    yt = _peer(hnt, w["expert_u"], w["expert_vt"], th, c, s1, e1, tm=_tile(t, 512), ib=SUBLANES)
    return _final(h, yt, w["norm_final"], tm=_tile(t, 256))


def kernel(x_prompt, x_sample, state_conv, state_h, cache_kv_w128, cache_kv_w512, cache_kv_w2048,
           norm_mix, w_in, conv_w, conv_b, w_gate_x, b_gate_x, w_gate_a, b_gate_a, lru_lambda,
           w_rnn_out, w_att_out, w_o, norm_ffn, w_query, sub_keys, expert_u, expert_v, norm_final):
    depth = w_in.shape[0]
    assert depth == 1 and x_prompt.shape[0] == 1
    batch, seq, d = x_prompt.shape
    nb, steps, _ = x_sample.shape
    d_in = w_in.shape[2]
    gw = HEADS * HEAD_DIM
    d_qkv = N_GROUPS * gw
    assert d_in == 4 * d + 3 * d_qkv and d % gw == 0
    caches = (cache_kv_w128, cache_kv_w512, cache_kv_w2048)
    col_q, col_k, col_v = 2 * d // gw, (2 * d + d_qkv) // gw, (2 * d + 2 * d_qkv) // gw
    cols = {"ga": (2 * d + 3 * d_qkv) // gw, "gb": (3 * d + 3 * d_qkv) // gw}

    mx = lambda a: a.astype(MXU_DTYPE)
    row = lambda a: a.reshape(1, -1).astype(F32)
    w = {
        "w_in": mx(w_in[0]), "w_rnn_out": mx(w_rnn_out[0]), "w_att_out": mx(w_att_out[0]),
        "w_o": mx(w_o[0]), "w_query": mx(w_query[0]),
        "sub_keys": sub_keys[0].reshape(2 * PEER_HEADS, N_KEYS, -1),
        "expert_u": mx(expert_u[0]), "expert_vt": mx(expert_v[0]).T,
        "norm_ffn": row(norm_ffn[0]), "norm_final": row(norm_final),
    }
    rnn_w = (conv_w[0], row(conv_b[0]), mx(w_gate_x[0]), row(b_gate_x[0]), mx(w_gate_a[0]),
             row(b_gate_a[0]), row(lru_lambda[0]))
    gain_mix = row(norm_mix[0])

    xp = x_prompt.reshape(seq, d)
    zp = _inproj(xp, gain_mix, w["w_in"], tm=_tile(seq, 1024), tn=gw)
    g_p, past_p, h_p = _rnn_prompt(zp, *rnn_w, jnp.zeros((SUBLANES, d), F32),
                                   jnp.zeros((1, d), F32), d=d, tt=_tile(seq, 256))
    outs, lses, kv_p = [], [], []
    for gi, (win, dil) in enumerate(ATT_GROUPS):
        o, lse = _attn_prompt(zp, gi, dil, col_q=col_q, col_k=col_k, col_v=col_v)
        outs.append(o)
        lses.append(lse)
        keep = min(win, seq)
        kcol = (col_k + gi) * gw
        vcol = (col_v + gi) * gw
        kv = jnp.stack([zp[seq - keep:, kcol:kcol + gw], zp[seq - keep:, vcol:vcol + gw]], axis=1)
        kv_p.append(kv.reshape(1, 1, keep, 2, HEADS, HEAD_DIM))
    y_p = _token_stages(xp, zp, g_p, outs, lses, w, cols).reshape(batch, seq, d)
    conv_p = past_p[SUBLANES - (CONV_W - 1):].reshape(1, 1, CONV_W - 1, d)
    h_p = h_p.reshape(1, 1, d)

    ts = nb * steps
    xs = x_sample.reshape(ts, d)
    zs = _inproj(xs, gain_mix, w["w_in"], tm=ts, tn=gw)
    g_s, conv_s, h_s = _rnn_sample(zs.reshape(nb, steps * d_in), *rnn_w,
                                   state_conv[0].reshape(nb, (CONV_W - 1) * d), state_h[0],
                                   d=d, steps=steps)
    g_s = g_s.reshape(ts, d)
    qkv = zs[:, 2 * d:2 * d + 3 * d_qkv].reshape(nb, steps, 3, N_GROUPS, HEADS, HEAD_DIM)
    outs, lses, kv_s = [], [], []
    for gi, (win, dil) in enumerate(ATT_GROUPS):
        cache = caches[gi][0]
        o, lse, new_cache = _attn_sample(qkv[:, :, 0, gi], qkv[:, :, 1, gi], qkv[:, :, 2, gi], cache,
                                         dil, rch=min(cache.shape[1], 512))
        outs.append(o.reshape(ts, gw))
        lse = lse[..., 0].reshape(ts, HEADS)
        lses.append(jnp.pad(lse, ((0, 0), (0, LANES - HEADS))))
        kv_s.append(new_cache[None])
    y_s = _token_stages(xs, zs, g_s, outs, lses, w, cols).reshape(nb, steps, d)
    conv_s = conv_s.reshape(1, nb, CONV_W - 1, d)
    h_s = h_s.reshape(1, nb, d)

    return (y_p, y_s, conv_p, h_p, kv_p[0], kv_p[1], kv_p[2],
            conv_s, h_s, kv_s[0], kv_s[1], kv_s[2])
```

```python
import functools
import math

import jax
import jax.numpy as jnp
from jax import lax
from jax.experimental import pallas as pl
from jax.experimental.pallas import tpu as pltpu

F32 = jnp.float32
MXU_DTYPE = jnp.bfloat16

RNN_BLOCKS = 8
CONV_W = 4
LRU_C = 8.0
ATT_GROUPS = ((128, 1), (512, 4), (2048, 16))
N_GROUPS = len(ATT_GROUPS)
HEADS = 8
HEAD_DIM = 128
BAND = 128
PEER_HEADS = 8
N_KEYS = 128
PEER_TOPK = 16
NORM_EPS = 1e-6

LANES = 128
SUBLANES = 8
VMEM_LIMIT_BYTES = 56 * 1024 * 1024

NEG_INF = float("-inf")
POS_INF = float("inf")


def _params(*sem):
    return pltpu.CompilerParams(dimension_semantics=sem, vmem_limit_bytes=VMEM_LIMIT_BYTES)


def _gelu(x):
    return 0.5 * x * (1.0 + lax.erf(x * math.sqrt(0.5)))


def _rms(x, gain):
    ms = jnp.mean(x * x, axis=-1, keepdims=True)
    return x * lax.rsqrt(ms + NORM_EPS) * gain


def _inproj_kernel(x_ref, g_ref, w_ref, z_ref, xn_sc):
    @pl.when(pl.program_id(1) == 0)
    def _():
        xn_sc[...] = _rms(x_ref[...], g_ref[...]).astype(MXU_DTYPE)

    z_ref[...] = jnp.dot(xn_sc[...], w_ref[...], preferred_element_type=F32)


def _inproj(x, gain, w, *, tm, tn):
    m, k = x.shape
    n = w.shape[1]
    return pl.pallas_call(
        _inproj_kernel,
        grid=(m // tm, n // tn),
        in_specs=[
            pl.BlockSpec((tm, k), lambda i, j: (i, 0)),
            pl.BlockSpec((1, k), lambda i, j: (0, 0)),
            pl.BlockSpec((k, tn), lambda i, j: (0, j)),
        ],
        out_specs=pl.BlockSpec((tm, tn), lambda i, j: (i, j)),
        out_shape=jax.ShapeDtypeStruct((m, n), F32),
        scratch_shapes=[pltpu.VMEM((tm, k), MXU_DTYPE)],
        compiler_params=_params("parallel", "arbitrary"),
        name="inproj",
    )(x, gain, w)


def _lru_gates(xc, wgx, bgx, wga, bga, log_sig):
    xb = xc.astype(MXU_DTYPE)
    gx = jax.nn.sigmoid(jnp.dot(xb, wgx, preferred_element_type=F32) + bgx)
    ga = jax.nn.sigmoid(jnp.dot(xb, wga, preferred_element_type=F32) + bga)
    a = jnp.exp(LRU_C * ga * log_sig)
    b = jnp.sqrt(1.0 - a * a) * gx * xc
    return a, b


def _log_sigmoid(x):
    return -(jnp.maximum(-x, 0.0) + jnp.log1p(jnp.exp(-jnp.abs(x))))


def _rnn_prompt_kernel(xr_ref, yr_ref, cw_ref, cb_ref, wgx_ref, bgx_ref, wga_ref, bga_ref, lam_ref,
                       p0_ref, h0_ref, g_ref, pn_ref, hn_ref, p_sc, h_sc, a_sc, b_sc):
    tt, d = xr_ref.shape
    bw = d // RNN_BLOCKS

    @pl.when(pl.program_id(0) == 0)
    def _():
        p_sc[...] = p0_ref[...]
        h_sc[...] = h0_ref[...]

    x = xr_ref[...]
    past = p_sc[...]
    row8 = lax.broadcasted_iota(jnp.int32, (SUBLANES, d), 0)
    xc = cb_ref[...] + cw_ref[0:1, :] * x
    for j in range(1, CONV_W):
        xs = pltpu.roll(x, j, 0)
        head = jnp.where(row8 < j, pltpu.roll(past, j, 0), xs[:SUBLANES])
        xs = jnp.concatenate([head, xs[SUBLANES:]], axis=0)
        xc = xc + cw_ref[j:j + 1, :] * xs
    p_sc[...] = x[tt - SUBLANES:]
    pn_ref[...] = x[tt - SUBLANES:]

    log_sig = _log_sigmoid(lam_ref[...])
    rowm = lax.broadcasted_iota(jnp.int32, (tt, bw), 0) & (SUBLANES - 1)
    for n in range(RNN_BLOCKS):
        sl = slice(n * bw, (n + 1) * bw)
        a, b = _lru_gates(xc[:, sl], wgx_ref[n], bgx_ref[:, sl], wga_ref[n], bga_ref[:, sl],
                          log_sig[:, sl])
        for sh in (1, 2, 4):
            ok = rowm >= sh
            a_sh = pltpu.roll(a, sh, 0)
            b_sh = pltpu.roll(b, sh, 0)
            b = jnp.where(ok, a * b_sh + b, b)
            a = jnp.where(ok, a * a_sh, a)
        a_sc[:, sl] = a
        b_sc[:, sl] = b

    def group(gi, h):
        rows = pl.ds(pl.multiple_of(gi * SUBLANES, SUBLANES), SUBLANES)
        hh = a_sc[rows, :] * h + b_sc[rows, :]
        b_sc[rows, :] = hh
        return hh[SUBLANES - 1:SUBLANES, :]

    h_last = lax.fori_loop(0, tt // SUBLANES, group, h_sc[...])
    h_sc[...] = h_last
    hn_ref[...] = h_last
    g_ref[...] = (_gelu(yr_ref[...]) * b_sc[...]).astype(g_ref.dtype)


def _rnn_prompt(z, conv_w, conv_b, wgx, bgx, wga, bga, lam, past8, h0, *, d, tt):
    t = z.shape[0]
    full2 = lambda i: (0, 0)
    full3 = lambda i: (0, 0, 0)
    bw = d // RNN_BLOCKS
    return pl.pallas_call(
        _rnn_prompt_kernel,
        grid=(t // tt,),
        in_specs=[
            pl.BlockSpec((tt, d), lambda i: (i, 0)),
            pl.BlockSpec((tt, d), lambda i: (i, 1)),
            pl.BlockSpec((CONV_W, d), full2),
            pl.BlockSpec((1, d), full2),
            pl.BlockSpec((RNN_BLOCKS, bw, bw), full3),
            pl.BlockSpec((1, d), full2),
            pl.BlockSpec((RNN_BLOCKS, bw, bw), full3),
            pl.BlockSpec((1, d), full2),
            pl.BlockSpec((1, d), full2),
            pl.BlockSpec((SUBLANES, d), full2),
            pl.BlockSpec((1, d), full2),
        ],
        out_specs=[
            pl.BlockSpec((tt, d), lambda i: (i, 0)),
            pl.BlockSpec((SUBLANES, d), full2),
            pl.BlockSpec((1, d), full2),
        ],
        out_shape=[
            jax.ShapeDtypeStruct((t, d), MXU_DTYPE),
            jax.ShapeDtypeStruct((SUBLANES, d), F32),
            jax.ShapeDtypeStruct((1, d), F32),
        ],
        scratch_shapes=[pltpu.VMEM((SUBLANES, d), F32), pltpu.VMEM((1, d), F32),
                        pltpu.VMEM((tt, d), F32), pltpu.VMEM((tt, d), F32)],
        compiler_params=_params("arbitrary"),
        name="rnn_prompt",
    )(z, z, conv_w, conv_b, wgx, bgx, wga, bga, lam, past8, h0)


def _rnn_sample_kernel(z_ref, cw_ref, cb_ref, wgx_ref, bgx_ref, wga_ref, bga_ref, lam_ref,
                       cs_ref, h0_ref, g_ref, cn_ref, hn_ref, *, d, d_in, steps):
    bw = d // RNN_BLOCKS
    log_sig = _log_sigmoid(lam_ref[...])
    hist = [cs_ref[:, j * d:(j + 1) * d] for j in range(CONV_W - 1)]
    h = [h0_ref[:, n * bw:(n + 1) * bw] for n in range(RNN_BLOCKS)]
    for s in range(steps):
        x = z_ref[:, s * d_in:s * d_in + d]
        hist.append(x)
        xc = cb_ref[...]
        for j in range(CONV_W):
            xc = xc + cw_ref[j:j + 1, :] * hist[-1 - j]
        for n in range(RNN_BLOCKS):
            sl = slice(n * bw, (n + 1) * bw)
            a, b = _lru_gates(xc[:, sl], wgx_ref[n], bgx_ref[:, sl], wga_ref[n], bga_ref[:, sl],
                              log_sig[:, sl])
            h[n] = a * h[n] + b
            y = z_ref[:, s * d_in + d + n * bw:s * d_in + d + (n + 1) * bw]
            g_ref[:, s * d + n * bw:s * d + (n + 1) * bw] = (_gelu(y) * h[n]).astype(g_ref.dtype)
    for j in range(CONV_W - 1):
        cn_ref[:, j * d:(j + 1) * d] = hist[len(hist) - (CONV_W - 1) + j]
    for n in range(RNN_BLOCKS):
        hn_ref[:, n * bw:(n + 1) * bw] = h[n]


def _rnn_sample(z2, conv_w, conv_b, wgx, bgx, wga, bga, lam, conv_state, h0, *, d, steps):
    nb = z2.shape[0]
    d_in = z2.shape[1] // steps
    return pl.pallas_call(
        functools.partial(_rnn_sample_kernel, d=d, d_in=d_in, steps=steps),
        out_shape=[
            jax.ShapeDtypeStruct((nb, steps * d), MXU_DTYPE),
            jax.ShapeDtypeStruct((nb, (CONV_W - 1) * d), F32),
            jax.ShapeDtypeStruct((nb, d), F32),
        ],
        compiler_params=pltpu.CompilerParams(vmem_limit_bytes=VMEM_LIMIT_BYTES),
        name="rnn_sample",
    )(z2, conv_w, conv_b, wgx, bgx, wga, bga, lam, conv_state, h0)


def _attn_prompt_kernel(q_ref, kc_ref, kp_ref, vc_ref, vp_ref, o_ref, lse_ref, *, dil, hpb):
    nblk = pl.program_id(0)
    hchunk = pl.program_id(1)
    qi = lax.broadcasted_iota(jnp.int32, (BAND, 2 * BAND), 0)
    ki = lax.broadcasted_iota(jnp.int32, (BAND, 2 * BAND), 1)
    first_key = jnp.where(nblk > 0, 0, BAND)
    mask = (ki >= qi) & (ki <= qi + BAND) & (ki >= first_key)
    lane = lax.broadcasted_iota(jnp.int32, (BAND, LANES), 1)
    scale = HEAD_DIM ** -0.5

    @pl.when(hchunk == 0)
    def _():
        lse_ref[...] = jnp.zeros_like(lse_ref)

    for r in range(dil):
        rows = pl.ds(r, BAND, stride=dil) if dil > 1 else slice(None)
        lse_tile = lse_ref[rows, :]
        for hh in range(hpb):
            sl = slice(hh * HEAD_DIM, (hh + 1) * HEAD_DIM)
            qh = (q_ref[rows, sl] * scale).astype(MXU_DTYPE)
            kk = jnp.concatenate([kp_ref[rows, sl], kc_ref[rows, sl]], axis=0).astype(MXU_DTYPE)
            vv = jnp.concatenate([vp_ref[rows, sl], vc_ref[rows, sl]], axis=0).astype(MXU_DTYPE)
            s = lax.dot_general(qh, kk, (((1,), (1,)), ((), ())), preferred_element_type=F32)
            s = jnp.where(mask, s, NEG_INF)
            m = jnp.max(s, axis=-1, keepdims=True)
            p = jnp.exp(s - m)
            den = jnp.sum(p, axis=-1, keepdims=True)
            o = jnp.dot(p.astype(MXU_DTYPE), vv, preferred_element_type=F32)
            o_ref[rows, sl] = o / den
            lse_tile = jnp.where(lane == hchunk * hpb + hh, m + jnp.log(den), lse_tile)
        lse_ref[rows, :] = lse_tile


def _attn_prompt(z, gi, dil, *, col_q, col_k, col_v):
    t = z.shape[0]
    gw = HEADS * HEAD_DIM
    hpb = HEADS if dil == 1 else 1
    cw = hpb * HEAD_DIM
    rb = BAND * dil
    per = gw // cw
    cur = lambda c: (lambda n, h: (n, (c + gi) * per + h))
    prev = lambda c: (lambda n, h: (jnp.maximum(n - 1, 0), (c + gi) * per + h))
    return pl.pallas_call(
        functools.partial(_attn_prompt_kernel, dil=dil, hpb=hpb),
        grid=(t // rb, HEADS // hpb),
        in_specs=[
            pl.BlockSpec((rb, cw), cur(col_q)),
            pl.BlockSpec((rb, cw), cur(col_k)),
            pl.BlockSpec((rb, cw), prev(col_k)),
            pl.BlockSpec((rb, cw), cur(col_v)),
            pl.BlockSpec((rb, cw), prev(col_v)),
        ],
        out_specs=[
            pl.BlockSpec((rb, cw), lambda n, h: (n, h)),
            pl.BlockSpec((rb, LANES), lambda n, h: (n, 0)),
        ],
        out_shape=[
            jax.ShapeDtypeStruct((t, gw), F32),
            jax.ShapeDtypeStruct((t, LANES), F32),
        ],
        compiler_params=_params("parallel", "arbitrary"),
        name=f"attn_prompt_g{gi}",
    )(z, z, z, z, z)


def _attn_sample_kernel(q_ref, kn_ref, vn_ref, c_ref, nx_ref, o_ref, lse_ref, new_ref,
                        m_sc, den_sc, acc_sc, *, dil, steps):
    ci = pl.program_id(1)
    nchunks = pl.num_programs(1)
    rch = c_ref.shape[1]

    def attend(s, krows, vrows, valid, first):
        q = q_ref[0, s] * (HEAD_DIM ** -0.5)
        sc = jnp.sum(q[None] * krows, axis=-1, keepdims=True)
        if valid is not None:
            sc = jnp.where(valid, sc, NEG_INF)
        m_blk = jnp.max(sc, axis=0)
        if first:
            m_new = m_blk
            p = jnp.exp(sc - m_new[None])
            den = jnp.sum(p, axis=0)
            acc = jnp.sum(p * vrows, axis=0)
        else:
            m_old = m_sc[s][:, 0:1]
            m_new = jnp.maximum(m_old, m_blk)
            alpha = jnp.exp(m_old - m_new)
            p = jnp.exp(sc - m_new[None])
            den = alpha * den_sc[s][:, 0:1] + jnp.sum(p, axis=0)
            acc = alpha * acc_sc[s] + jnp.sum(p * vrows, axis=0)
        m_sc[s] = jnp.broadcast_to(m_new, (HEADS, HEAD_DIM))
        den_sc[s] = jnp.broadcast_to(den, (HEADS, HEAD_DIM))
        acc_sc[s] = acc

    @pl.when(ci == 0)
    def _():
        for s in range(steps):
            lo = 0 if dil == 1 else s
            attend(s, kn_ref[0, lo:s + 1], vn_ref[0, lo:s + 1], None, True)

    for s in range(steps):
        if dil == 1:
            ridx = lax.broadcasted_iota(jnp.int32, (rch, 1, 1), 0) + ci * rch
            attend(s, c_ref[0, :, 0], c_ref[0, :, 1], ridx >= s, False)
        else:
            n = rch // dil
            attend(s, c_ref[0, pl.ds(s, n, stride=dil), 0], c_ref[0, pl.ds(s, n, stride=dil), 1],
                   None, False)

    @pl.when(ci == nchunks - 1)
    def _():
        for s in range(steps):
            o_ref[0, s] = acc_sc[s] / den_sc[s]
            lse_ref[0, s] = m_sc[s] + jnp.log(den_sc[s])

    new_ref[0, 0:rch - steps] = c_ref[0, steps:rch]

    @pl.when(ci < nchunks - 1)
    def _():
        new_ref[0, rch - steps:rch] = nx_ref[0]

    @pl.when(ci == nchunks - 1)
    def _():
        new_ref[0, rch - steps:rch, 0] = kn_ref[0]
        new_ref[0, rch - steps:rch, 1] = vn_ref[0]


def _attn_sample(q, kn, vn, cache, dil, *, rch):
    nb, steps = q.shape[0], q.shape[1]
    lg = cache.shape[1]
    nchunks = lg // rch
    per = rch // steps
    last = lg // steps - 1
    small = pl.BlockSpec((1, steps, HEADS, HEAD_DIM), lambda b, c: (b, 0, 0, 0))
    big = pl.BlockSpec((1, rch, 2, HEADS, HEAD_DIM), lambda b, c: (b, c, 0, 0, 0))
    nxt = pl.BlockSpec((1, steps, 2, HEADS, HEAD_DIM),
                       lambda b, c: (b, jnp.minimum((c + 1) * per, last), 0, 0, 0))
    return pl.pallas_call(
        functools.partial(_attn_sample_kernel, dil=dil, steps=steps),
        grid=(nb, nchunks),
        in_specs=[small, small, small, big, nxt],
        out_specs=[small, small, big],
        out_shape=[
            jax.ShapeDtypeStruct(q.shape, F32),
            jax.ShapeDtypeStruct(q.shape, F32),
            jax.ShapeDtypeStruct(cache.shape, cache.dtype),
        ],
        scratch_shapes=[pltpu.VMEM((steps, HEADS, HEAD_DIM), F32)] * 3,
        compiler_params=_params("parallel", "arbitrary"),
        name=f"attn_sample_d{dil}",
    )(q, kn, vn, cache, cache)


def _merge_kernel(g_ref, o0_ref, o1_ref, o2_ref, l0_ref, l1_ref, l2_ref, ga_ref, gb_ref,
                  wr_ref, wa_ref, out_ref, att_sc):
    @pl.when(pl.program_id(1) == 0)
    def _():
        l0, l1, l2 = l0_ref[...], l1_ref[...], l2_ref[...]
        mx = jnp.maximum(jnp.maximum(l0, l1), l2)
        e0, e1, e2 = jnp.exp(l0 - mx), jnp.exp(l1 - mx), jnp.exp(l2 - mx)
        inv = 1.0 / (e0 + e1 + e2)
        w0, w1, w2 = e0 * inv, e1 * inv, e2 * inv
        for h in range(HEADS):
            sl = slice(h * HEAD_DIM, (h + 1) * HEAD_DIM)
            att = (w0[:, h:h + 1] * o0_ref[:, sl] + w1[:, h:h + 1] * o1_ref[:, sl]
                   + w2[:, h:h + 1] * o2_ref[:, sl])
            att_sc[:, sl] = att.astype(att_sc.dtype)

    rnn_out = jnp.dot(g_ref[...], wr_ref[...], preferred_element_type=F32)
    att_out = jnp.dot(att_sc[...], wa_ref[...], preferred_element_type=F32)
    mixed = jax.nn.sigmoid(ga_ref[...]) * rnn_out + jax.nn.sigmoid(gb_ref[...]) * att_out
    out_ref[...] = mixed.astype(out_ref.dtype)


def _merge(g, outs, lses, z, w_rnn, w_att, *, col_ga, col_gb, tm, tn):
    t, d = g.shape
    gw = HEADS * HEAD_DIM
    nj = d // tn
    row = lambda i, j: (i, 0)
    return pl.pallas_call(
        _merge_kernel,
        grid=(t // tm, nj),
        in_specs=[
            pl.BlockSpec((tm, d), row),
            pl.BlockSpec((tm, gw), row), pl.BlockSpec((tm, gw), row), pl.BlockSpec((tm, gw), row),
            pl.BlockSpec((tm, LANES), row), pl.BlockSpec((tm, LANES), row),
            pl.BlockSpec((tm, LANES), row),
            pl.BlockSpec((tm, tn), lambda i, j: (i, col_ga + j)),
            pl.BlockSpec((tm, tn), lambda i, j: (i, col_gb + j)),
            pl.BlockSpec((d, tn), lambda i, j: (0, j)),
            pl.BlockSpec((gw, tn), lambda i, j: (0, j)),
        ],
        out_specs=pl.BlockSpec((tm, tn), lambda i, j: (i, j)),
        out_shape=jax.ShapeDtypeStruct((t, d), MXU_DTYPE),
        scratch_shapes=[pltpu.VMEM((tm, gw), MXU_DTYPE)],
        compiler_params=_params("parallel", "arbitrary"),
        name="merge",
    )(g, *outs, *lses, z, z, w_rnn, w_att)


def _oproj_kernel(x_ref, mx_ref, wo_ref, gain_ref, h_ref, hn_ref, hnt_ref):
    h = x_ref[...] + jnp.dot(mx_ref[...], wo_ref[...], preferred_element_type=F32)
    h_ref[...] = h
    hn = _rms(h, gain_ref[...])
    hn_ref[...] = hn.astype(hn_ref.dtype)
    hnt_ref[...] = hn.T.astype(hnt_ref.dtype)


def _oproj(x, mixed, w_o, gain, *, tm):
    t, d = x.shape
    row = lambda i: (i, 0)
    return pl.pallas_call(
        _oproj_kernel,
        grid=(t // tm,),
        in_specs=[
            pl.BlockSpec((tm, d), row),
            pl.BlockSpec((tm, d), row),
            pl.BlockSpec((d, d), lambda i: (0, 0)),
            pl.BlockSpec((1, d), lambda i: (0, 0)),
        ],
        out_specs=[pl.BlockSpec((tm, d), row), pl.BlockSpec((tm, d), row),
                   pl.BlockSpec((d, tm), lambda i: (0, i))],
        out_shape=[jax.ShapeDtypeStruct((t, d), F32), jax.ShapeDtypeStruct((t, d), MXU_DTYPE),
                   jax.ShapeDtypeStruct((d, t), MXU_DTYPE)],
        compiler_params=_params("parallel"),
        name="oproj",
    )(x, mixed, w_o, gain)


def _all_sublanes(x, op):
    for sh in (4, 2, 1):
        x = op(x, pltpu.roll(x, sh, 0))
    return x


def _take_ranked(vals, order, count):
    rest = vals
    rank = jnp.full(vals.shape, float(count), F32)
    tops, firsts = [], []
    for k in range(count):
        m = _all_sublanes(jnp.max(rest, axis=0), jnp.maximum)
        first = _all_sublanes(jnp.min(jnp.where(rest == m[None], order, POS_INF), axis=0),
                              jnp.minimum)
        hit = order == first[None]
        rest = jnp.where(hit, NEG_INF, rest)
        rank = jnp.where(hit, float(k), rank)
        tops.append(m)
        firsts.append(first)
    return tops, firsts, rank


def _route_lanes(s0, s1):
    nt = PEER_TOPK
    nk, ln = s0.shape
    nv = nk // SUBLANES
    s0 = s0.reshape(nv, SUBLANES, ln)
    s1 = s1.reshape(nv, SUBLANES, ln)
    key = (lax.broadcasted_iota(jnp.int32, (nv, SUBLANES, ln), 0) * SUBLANES
           + lax.broadcasted_iota(jnp.int32, (nv, SUBLANES, ln), 1)).astype(F32)
    top0, _, rank0 = _take_ranked(s0, key, nt)
    top1, _, rank1 = _take_ranked(s1, key, nt)

    sub = lax.broadcasted_iota(jnp.int32, (SUBLANES, ln), 0)
    subf = sub.astype(F32)

    def on_sublanes(rows):
        out = rows[0]
        for b in range(1, SUBLANES):
            out = jnp.where(sub == b, rows[b], out)
        return out

    assert nt == 2 * SUBLANES
    t1lo, t1hi = on_sublanes(top1[:SUBLANES]), on_sublanes(top1[SUBLANES:])
    t0hi = on_sublanes(top0[SUBLANES:])
    vals = [top0[0] + t1lo, top0[0] + t1hi]
    poss = [subf, subf + SUBLANES]
    for a in range(1, SUBLANES):
        vals.append(jnp.where(sub < nt // (a + 1), top0[a] + t1lo, NEG_INF))
        poss.append(subf + a * nt)
    vals.append(t0hi + top1[0])
    poss.append((subf + SUBLANES) * nt)
    best, bpos, _ = _take_ranked(jnp.stack(vals), jnp.stack(poss), nt)

    zsum = jnp.zeros((SUBLANES, ln), F32)
    cnt_lo = jnp.zeros((SUBLANES, ln), F32)
    cnt_hi = jnp.zeros((SUBLANES, ln), F32)
    for k in range(nt):
        zsum = zsum + jnp.exp(best[k] - best[0])
        a_k = jnp.floor(bpos[k] * (1.0 / nt))
        cnt_lo = cnt_lo + jnp.where(subf == a_k, 1.0, 0.0)
        cnt_hi = cnt_hi + jnp.where(subf + SUBLANES == a_k, 1.0, 0.0)

    th = jnp.full((nv, SUBLANES, ln), POS_INF, F32)
    for a in range(nt):
        cnt = cnt_lo if a < SUBLANES else cnt_hi
        n_a = jnp.broadcast_to(cnt[a % SUBLANES:a % SUBLANES + 1], (SUBLANES, ln))
        th = jnp.where(rank0 == float(a), (1.0 - n_a)[None], th)
    c = jnp.exp(s0 - top0[0][None]) / zsum[None]
    r1 = jnp.where(rank1 < float(nt), -rank1, NEG_INF)
    e1 = jnp.exp(s1 - top1[0][None])
    return tuple(x.reshape(nk, ln) for x in (th, c, r1, e1))


def _route_kernel(hn_ref, wq_ref, sk_ref, th_ref, c_ref, r1_ref, e1_ref, q_sc):
    q = jnp.dot(hn_ref[...], wq_ref[...], preferred_element_type=F32)
    tm = hn_ref.shape[0]
    d_half = sk_ref.shape[2]
    for hp in range(2 * PEER_HEADS):
        q_sc[hp] = q[:, hp * d_half:(hp + 1) * d_half]

    def head(h, _):
        def scores(half):
            return lax.dot_general(sk_ref[2 * h + half], q_sc[2 * h + half],
                                   (((1,), (1,)), ((), ())),
                                   preferred_element_type=F32)

        s0, s1 = scores(0), scores(1)
        for lc in range(tm // LANES):
            ls = slice(lc * LANES, (lc + 1) * LANES)
            th, c, r1, e1 = _route_lanes(s0[:, ls], s1[:, ls])
            th_ref[h, :, ls] = th
            c_ref[h, :, ls] = c
            r1_ref[h, :, ls] = r1
            e1_ref[h, :, ls] = e1
        return 0

    lax.fori_loop(0, PEER_HEADS, head, 0)


def _route(hn, w_query, sub_keys, *, tm):
    t, d = hn.shape
    dq = w_query.shape[1]
    slab = jax.ShapeDtypeStruct((PEER_HEADS, N_KEYS, t), F32)
    slab_spec = pl.BlockSpec((PEER_HEADS, N_KEYS, tm), lambda i: (0, 0, i))
    return pl.pallas_call(
        _route_kernel,
        grid=(t // tm,),
        in_specs=[
            pl.BlockSpec((tm, d), lambda i: (i, 0)),
            pl.BlockSpec((d, dq), lambda i: (0, 0)),
            pl.BlockSpec(sub_keys.shape, lambda i: (0, 0, 0)),
        ],
        out_specs=[slab_spec] * 4,
        out_shape=[slab] * 4,
        scratch_shapes=[pltpu.VMEM((2 * PEER_HEADS, tm, sub_keys.shape[2]), F32)],
        compiler_params=_params("parallel"),
        name="peer_route",
    )(hn, w_query, sub_keys)


PEER_ROW_CHUNK = 32


def _peer_kernel(x_ref, u_ref, vt_ref, th_ref, c_ref, r1_ref, e1_ref, y_ref, act_sc, gate_sc,
                 coef_sc):
    e = pl.program_id(1)
    tm = x_ref.shape[1]
    ib = th_ref.shape[1]
    hk = ib // 2
    hr = hk * N_KEYS
    assert tm % LANES == 0 and N_KEYS % PEER_ROW_CHUNK == 0 and ib % 2 == 0

    @pl.when(e == 0)
    def _():
        y_ref[...] = jnp.zeros_like(y_ref)

    def first_matmul(half):
        rows = slice(half * hr, (half + 1) * hr)
        act_sc[rows, :] = jnp.dot(u_ref[rows, :], x_ref[...], preferred_element_type=F32)

    def routing_weights(half):
        for lc in range(tm // LANES):
            ls = slice(lc * LANES, (lc + 1) * LANES)
            for jc in range(N_KEYS // PEER_ROW_CHUNK):
                js = slice(jc * PEER_ROW_CHUNK, (jc + 1) * PEER_ROW_CHUNK)
                gates = [jnp.zeros((PEER_ROW_CHUNK, LANES), F32) for _ in range(hk)]
                for h in range(PEER_HEADS):
                    r1 = r1_ref[h, js, ls]
                    e1 = e1_ref[h, js, ls]
                    for k in range(hk):
                        ii = half * hk + k
                        th = th_ref[h, ii:ii + 1, ls]
                        ch = c_ref[h, ii:ii + 1, ls]
                        gates[k] = gates[k] + jnp.where(r1 >= th, e1 * ch, 0.0)
                for k in range(hk):
                    r0 = (half * hk + k) * N_KEYS + jc * PEER_ROW_CHUNK
                    gate_sc[r0:r0 + PEER_ROW_CHUNK, ls] = gates[k]

    def second_matmul(half):
        rows = slice(half * hr, (half + 1) * hr)
        coef_sc[rows, :] = (_gelu(act_sc[rows, :]) * gate_sc[rows, :]).astype(coef_sc.dtype)
        y_ref[...] += jnp.dot(vt_ref[:, rows], coef_sc[rows, :], preferred_element_type=F32)

    @pl.when(e >= 0)
    def _():
        first_matmul(0)
        routing_weights(0)

    @pl.when(e >= -1)
    def _():
        first_matmul(1)
        routing_weights(1)
        second_matmul(0)

    @pl.when(e >= -2)
    def _():
        second_matmul(1)


def _peer(hnt, u, vt, th, c, r1, e1, *, tm, ib):
    d, t = hnt.shape
    ne = u.shape[0]
    te = ib * N_KEYS
    slab_spec = pl.BlockSpec((PEER_HEADS, N_KEYS, tm), lambda i, e: (0, 0, i))
    first_spec = pl.BlockSpec((PEER_HEADS, ib, tm), lambda i, e: (0, e, i))
    return pl.pallas_call(
        _peer_kernel,
        grid=(t // tm, ne // te),
        in_specs=[
            pl.BlockSpec((d, tm), lambda i, e: (0, i)),
            pl.BlockSpec((te, d), lambda i, e: (e, 0)),
            pl.BlockSpec((d, te), lambda i, e: (0, e)),
            first_spec, first_spec, slab_spec, slab_spec,
        ],
        out_specs=pl.BlockSpec((d, tm), lambda i, e: (0, i)),
        out_shape=jax.ShapeDtypeStruct((d, t), F32),
        scratch_shapes=[pltpu.VMEM((te, tm), F32), pltpu.VMEM((te, tm), F32),
                        pltpu.VMEM((te, tm), MXU_DTYPE)],
        compiler_params=_params("parallel", "arbitrary"),
        name="peer_experts",
    )(hnt, u, vt, th, c, r1, e1)


def _final_kernel(h_ref, yt_ref, gain_ref, out_ref):
    out_ref[...] = _rms(h_ref[...] + yt_ref[...].T, gain_ref[...])


def _final(h, yt, gain, *, tm):
    t, d = h.shape
    return pl.pallas_call(
        _final_kernel,
        grid=(t // tm,),
        in_specs=[pl.BlockSpec((tm, d), lambda i: (i, 0)), pl.BlockSpec((d, tm), lambda i: (0, i)),
                  pl.BlockSpec((1, d), lambda i: (0, 0))],
        out_specs=pl.BlockSpec((tm, d), lambda i: (i, 0)),
        out_shape=jax.ShapeDtypeStruct((t, d), F32),
        compiler_params=_params("parallel"),
        name="final_norm",
    )(h, yt, gain)


def _tile(n, want):
    return min(n, want)


def _token_stages(x2, z, g, outs, lses, w, cols):
    t = x2.shape[0]
    mixed = _merge(g, outs, lses, z, w["w_rnn_out"], w["w_att_out"], col_ga=cols["ga"],
                   col_gb=cols["gb"], tm=_tile(t, 256), tn=HEADS * HEAD_DIM)
    h, hn, hnt = _oproj(x2, mixed, w["w_o"], w["norm_ffn"], tm=_tile(t, 256))
    th, c, s1, e1 = _route(hn, w["w_query"], w["sub_keys"], tm=_tile(t, 256))
    yt = _peer(hnt, w["expert_u"], w["expert_vt"], th, c, s1, e1, tm=_tile(t, 512), ib=SUBLANES)
    return _final(h, yt, w["norm_final"], tm=_tile(t, 256))


def kernel(x_prompt, x_sample, state_conv, state_h, cache_kv_w128, cache_kv_w512, cache_kv_w2048,
           norm_mix, w_in, conv_w, conv_b, w_gate_x, b_gate_x, w_gate_a, b_gate_a, lru_lambda,
           w_rnn_out, w_att_out, w_o, norm_ffn, w_query, sub_keys, expert_u, expert_v, norm_final):
    depth = w_in.shape[0]
    assert depth == 1 and x_prompt.shape[0] == 1
    batch, seq, d = x_prompt.shape
    nb, steps, _ = x_sample.shape
    d_in = w_in.shape[2]
    gw = HEADS * HEAD_DIM
    d_qkv = N_GROUPS * gw
    assert d_in == 4 * d + 3 * d_qkv and d % gw == 0
    caches = (cache_kv_w128, cache_kv_w512, cache_kv_w2048)
    col_q, col_k, col_v = 2 * d // gw, (2 * d + d_qkv) // gw, (2 * d + 2 * d_qkv) // gw
    cols = {"ga": (2 * d + 3 * d_qkv) // gw, "gb": (3 * d + 3 * d_qkv) // gw}

    mx = lambda a: a.astype(MXU_DTYPE)
    row = lambda a: a.reshape(1, -1).astype(F32)
    w = {
        "w_in": mx(w_in[0]), "w_rnn_out": mx(w_rnn_out[0]), "w_att_out": mx(w_att_out[0]),
        "w_o": mx(w_o[0]), "w_query": mx(w_query[0]),
        "sub_keys": sub_keys[0].reshape(2 * PEER_HEADS, N_KEYS, -1),
        "expert_u": mx(expert_u[0]), "expert_vt": mx(expert_v[0]).T,
        "norm_ffn": row(norm_ffn[0]), "norm_final": row(norm_final),
    }
    rnn_w = (conv_w[0], row(conv_b[0]), mx(w_gate_x[0]), row(b_gate_x[0]), mx(w_gate_a[0]),
             row(b_gate_a[0]), row(lru_lambda[0]))
    gain_mix = row(norm_mix[0])

    xp = x_prompt.reshape(seq, d)
    zp = _inproj(xp, gain_mix, w["w_in"], tm=_tile(seq, 1024), tn=gw)
    g_p, past_p, h_p = _rnn_prompt(zp, *rnn_w, jnp.zeros((SUBLANES, d), F32),
                                   jnp.zeros((1, d), F32), d=d, tt=_tile(seq, 256))
    outs, lses, kv_p = [], [], []
    for gi, (win, dil) in enumerate(ATT_GROUPS):
        o, lse = _attn_prompt(zp, gi, dil, col_q=col_q, col_k=col_k, col_v=col_v)
        outs.append(o)
        lses.append(lse)
        keep = min(win, seq)
        kcol = (col_k + gi) * gw
        vcol = (col_v + gi) * gw
        kv = jnp.stack([zp[seq - keep:, kcol:kcol + gw], zp[seq - keep:, vcol:vcol + gw]], axis=1)
        kv_p.append(kv.reshape(1, 1, keep, 2, HEADS, HEAD_DIM))
    y_p = _token_stages(xp, zp, g_p, outs, lses, w, cols).reshape(batch, seq, d)
    conv_p = past_p[SUBLANES - (CONV_W - 1):].reshape(1, 1, CONV_W - 1, d)
    h_p = h_p.reshape(1, 1, d)

    ts = nb * steps
    xs = x_sample.reshape(ts, d)
    zs = _inproj(xs, gain_mix, w["w_in"], tm=ts, tn=gw)
    g_s, conv_s, h_s = _rnn_sample(zs.reshape(nb, steps * d_in), *rnn_w,
                                   state_conv[0].reshape(nb, (CONV_W - 1) * d), state_h[0],
                                   d=d, steps=steps)
    g_s = g_s.reshape(ts, d)
    qkv = zs[:, 2 * d:2 * d + 3 * d_qkv].reshape(nb, steps, 3, N_GROUPS, HEADS, HEAD_DIM)
    outs, lses, kv_s = [], [], []
    for gi, (win, dil) in enumerate(ATT_GROUPS):
        cache = caches[gi][0]
        o, lse, new_cache = _attn_sample(qkv[:, :, 0, gi], qkv[:, :, 1, gi], qkv[:, :, 2, gi], cache,
                                         dil, rch=min(cache.shape[1], 512))
        outs.append(o.reshape(ts, gw))
        lse = lse[..., 0].reshape(ts, HEADS)
        lses.append(jnp.pad(lse, ((0, 0), (0, LANES - HEADS))))
        kv_s.append(new_cache[None])
    y_s = _token_stages(xs, zs, g_s, outs, lses, w, cols).reshape(nb, steps, d)
    conv_s = conv_s.reshape(1, nb, CONV_W - 1, d)
    h_s = h_s.reshape(1, nb, d)

    return (y_p, y_s, conv_p, h_p, kv_p[0], kv_p[1], kv_p[2],
            conv_s, h_s, kv_s[0], kv_s[1], kv_s[2])
```

```python
import functools
import math

import jax
import jax.numpy as jnp
from jax import lax
from jax.experimental import pallas as pl
from jax.experimental.pallas import tpu as pltpu

F32 = jnp.float32
MXU_DTYPE = jnp.bfloat16

RNN_BLOCKS = 8
CONV_W = 4
LRU_C = 8.0
ATT_GROUPS = ((128, 1), (512, 4), (2048, 16))
N_GROUPS = len(ATT_GROUPS)
HEADS = 8
HEAD_DIM = 128
BAND = 128
PEER_HEADS = 8
N_KEYS = 128
PEER_TOPK = 16
NORM_EPS = 1e-6

LANES = 128
SUBLANES = 8
VMEM_LIMIT_BYTES = 56 * 1024 * 1024

NEG_INF = float("-inf")
POS_INF = float("inf")


def _params(*sem):
    return pltpu.CompilerParams(dimension_semantics=sem, vmem_limit_bytes=VMEM_LIMIT_BYTES)


def _gelu(x):
    return 0.5 * x * (1.0 + lax.erf(x * math.sqrt(0.5)))


def _rms(x, gain):
    ms = jnp.mean(x * x, axis=-1, keepdims=True)
    return x * lax.rsqrt(ms + NORM_EPS) * gain


def _inproj_kernel(x_ref, g_ref, w_ref, z_ref, xn_sc):
    @pl.when(pl.program_id(1) == 0)
    def _():
        xn_sc[...] = _rms(x_ref[...], g_ref[...]).astype(MXU_DTYPE)

    z_ref[...] = jnp.dot(xn_sc[...], w_ref[...], preferred_element_type=F32)


def _inproj(x, gain, w, *, tm, tn):
    m, k = x.shape
    n = w.shape[1]
    return pl.pallas_call(
        _inproj_kernel,
        grid=(m // tm, n // tn),
        in_specs=[
            pl.BlockSpec((tm, k), lambda i, j: (i, 0)),
            pl.BlockSpec((1, k), lambda i, j: (0, 0)),
            pl.BlockSpec((k, tn), lambda i, j: (0, j)),
        ],
        out_specs=pl.BlockSpec((tm, tn), lambda i, j: (i, j)),
        out_shape=jax.ShapeDtypeStruct((m, n), F32),
        scratch_shapes=[pltpu.VMEM((tm, k), MXU_DTYPE)],
        compiler_params=_params("parallel", "arbitrary"),
        name="inproj",
    )(x, gain, w)


def _lru_gates(xc, wgx, bgx, wga, bga, log_sig):
    xb = xc.astype(MXU_DTYPE)
    gx = jax.nn.sigmoid(jnp.dot(xb, wgx, preferred_element_type=F32) + bgx)
    ga = jax.nn.sigmoid(jnp.dot(xb, wga, preferred_element_type=F32) + bga)
    a = jnp.exp(LRU_C * ga * log_sig)
    b = jnp.sqrt(1.0 - a * a) * gx * xc
    return a, b


def _log_sigmoid(x):
    return -(jnp.maximum(-x, 0.0) + jnp.log1p(jnp.exp(-jnp.abs(x))))


def _rnn_prompt_kernel(xr_ref, yr_ref, cw_ref, cb_ref, wgx_ref, bgx_ref, wga_ref, bga_ref, lam_ref,
                       p0_ref, h0_ref, g_ref, pn_ref, hn_ref, p_sc, h_sc, a_sc, b_sc):
    tt, d = xr_ref.shape
    bw = d // RNN_BLOCKS

    @pl.when(pl.program_id(0) == 0)
    def _():
        p_sc[...] = p0_ref[...]
        h_sc[...] = h0_ref[...]

    x = xr_ref[...]
    past = p_sc[...]
    row8 = lax.broadcasted_iota(jnp.int32, (SUBLANES, d), 0)
    xc = cb_ref[...] + cw_ref[0:1, :] * x
    for j in range(1, CONV_W):
        xs = pltpu.roll(x, j, 0)
        head = jnp.where(row8 < j, pltpu.roll(past, j, 0), xs[:SUBLANES])
        xs = jnp.concatenate([head, xs[SUBLANES:]], axis=0)
        xc = xc + cw_ref[j:j + 1, :] * xs
    p_sc[...] = x[tt - SUBLANES:]
    pn_ref[...] = x[tt - SUBLANES:]

    log_sig = _log_sigmoid(lam_ref[...])
    rowm = lax.broadcasted_iota(jnp.int32, (tt, bw), 0) & (SUBLANES - 1)
    for n in range(RNN_BLOCKS):
        sl = slice(n * bw, (n + 1) * bw)
        a, b = _lru_gates(xc[:, sl], wgx_ref[n], bgx_ref[:, sl], wga_ref[n], bga_ref[:, sl],
                          log_sig[:, sl])
        for sh in (1, 2, 4):
            ok = rowm >= sh
            a_sh = pltpu.roll(a, sh, 0)
            b_sh = pltpu.roll(b, sh, 0)
            b = jnp.where(ok, a * b_sh + b, b)
            a = jnp.where(ok, a * a_sh, a)
        a_sc[:, sl] = a
        b_sc[:, sl] = b

    def group(gi, h):
        rows = pl.ds(pl.multiple_of(gi * SUBLANES, SUBLANES), SUBLANES)
        hh = a_sc[rows, :] * h + b_sc[rows, :]
        b_sc[rows, :] = hh
        return hh[SUBLANES - 1:SUBLANES, :]

    h_last = lax.fori_loop(0, tt // SUBLANES, group, h_sc[...])
    h_sc[...] = h_last
    hn_ref[...] = h_last
    g_ref[...] = (_gelu(yr_ref[...]) * b_sc[...]).astype(g_ref.dtype)


def _rnn_prompt(z, conv_w, conv_b, wgx, bgx, wga, bga, lam, past8, h0, *, d, tt):
    t = z.shape[0]
    full2 = lambda i: (0, 0)
    full3 = lambda i: (0, 0, 0)
    bw = d // RNN_BLOCKS
    return pl.pallas_call(
        _rnn_prompt_kernel,
        grid=(t // tt,),
        in_specs=[
            pl.BlockSpec((tt, d), lambda i: (i, 0)),
            pl.BlockSpec((tt, d), lambda i: (i, 1)),
            pl.BlockSpec((CONV_W, d), full2),
            pl.BlockSpec((1, d), full2),
            pl.BlockSpec((RNN_BLOCKS, bw, bw), full3),
            pl.BlockSpec((1, d), full2),
            pl.BlockSpec((RNN_BLOCKS, bw, bw), full3),
            pl.BlockSpec((1, d), full2),
            pl.BlockSpec((1, d), full2),
            pl.BlockSpec((SUBLANES, d), full2),
            pl.BlockSpec((1, d), full2),
        ],
        out_specs=[
            pl.BlockSpec((tt, d), lambda i: (i, 0)),
            pl.BlockSpec((SUBLANES, d), full2),
            pl.BlockSpec((1, d), full2),
        ],
        out_shape=[
            jax.ShapeDtypeStruct((t, d), MXU_DTYPE),
            jax.ShapeDtypeStruct((SUBLANES, d), F32),
            jax.ShapeDtypeStruct((1, d), F32),
        ],
        scratch_shapes=[pltpu.VMEM((SUBLANES, d), F32), pltpu.VMEM((1, d), F32),
                        pltpu.VMEM((tt, d), F32), pltpu.VMEM((tt, d), F32)],
        compiler_params=_params("arbitrary"),
        name="rnn_prompt",
    )(z, z, conv_w, conv_b, wgx, bgx, wga, bga, lam, past8, h0)


def _rnn_sample_kernel(z_ref, cw_ref, cb_ref, wgx_ref, bgx_ref, wga_ref, bga_ref, lam_ref,
                       cs_ref, h0_ref, g_ref, cn_ref, hn_ref, *, d, d_in, steps):
    bw = d // RNN_BLOCKS
    log_sig = _log_sigmoid(lam_ref[...])
    hist = [cs_ref[:, j * d:(j + 1) * d] for j in range(CONV_W - 1)]
    h = [h0_ref[:, n * bw:(n + 1) * bw] for n in range(RNN_BLOCKS)]
    for s in range(steps):
        x = z_ref[:, s * d_in:s * d_in + d]
        hist.append(x)
        xc = cb_ref[...]
        for j in range(CONV_W):
            xc = xc + cw_ref[j:j + 1, :] * hist[-1 - j]
        for n in range(RNN_BLOCKS):
            sl = slice(n * bw, (n + 1) * bw)
            a, b = _lru_gates(xc[:, sl], wgx_ref[n], bgx_ref[:, sl], wga_ref[n], bga_ref[:, sl],
                              log_sig[:, sl])
            h[n] = a * h[n] + b
            y = z_ref[:, s * d_in + d + n * bw:s * d_in + d + (n + 1) * bw]
            g_ref[:, s * d + n * bw:s * d + (n + 1) * bw] = (_gelu(y) * h[n]).astype(g_ref.dtype)
    for j in range(CONV_W - 1):
        cn_ref[:, j * d:(j + 1) * d] = hist[len(hist) - (CONV_W - 1) + j]
    for n in range(RNN_BLOCKS):
        hn_ref[:, n * bw:(n + 1) * bw] = h[n]


def _rnn_sample(z2, conv_w, conv_b, wgx, bgx, wga, bga, lam, conv_state, h0, *, d, steps):
    nb = z2.shape[0]
    d_in = z2.shape[1] // steps
    return pl.pallas_call(
        functools.partial(_rnn_sample_kernel, d=d, d_in=d_in, steps=steps),
        out_shape=[
            jax.ShapeDtypeStruct((nb, steps * d), MXU_DTYPE),
            jax.ShapeDtypeStruct((nb, (CONV_W - 1) * d), F32),
            jax.ShapeDtypeStruct((nb, d), F32),
        ],
        compiler_params=pltpu.CompilerParams(vmem_limit_bytes=VMEM_LIMIT_BYTES),
        name="rnn_sample",
    )(z2, conv_w, conv_b, wgx, bgx, wga, bga, lam, conv_state, h0)


def _attn_prompt_kernel(q_ref, kc_ref, kp_ref, vc_ref, vp_ref, o_ref, lse_ref, *, dil, hpb):
    nblk = pl.program_id(0)
    hchunk = pl.program_id(1)
    qi = lax.broadcasted_iota(jnp.int32, (BAND, 2 * BAND), 0)
    ki = lax.broadcasted_iota(jnp.int32, (BAND, 2 * BAND), 1)
    first_key = jnp.where(nblk > 0, 0, BAND)
    mask = (ki >= qi) & (ki <= qi + BAND) & (ki >= first_key)
    lane = lax.broadcasted_iota(jnp.int32, (BAND, LANES), 1)
    scale = HEAD_DIM ** -0.5

    @pl.when(hchunk == 0)
    def _():
        lse_ref[...] = jnp.zeros_like(lse_ref)

    for r in range(dil):
        rows = pl.ds(r, BAND, stride=dil) if dil > 1 else slice(None)
        lse_tile = lse_ref[rows, :]
        for hh in range(hpb):
            sl = slice(hh * HEAD_DIM, (hh + 1) * HEAD_DIM)
            qh = (q_ref[rows, sl] * scale).astype(MXU_DTYPE)
            kk = jnp.concatenate([kp_ref[rows, sl], kc_ref[rows, sl]], axis=0).astype(MXU_DTYPE)
            vv = jnp.concatenate([vp_ref[rows, sl], vc_ref[rows, sl]], axis=0).astype(MXU_DTYPE)
            s = lax.dot_general(qh, kk, (((1,), (1,)), ((), ())), preferred_element_type=F32)
            s = jnp.where(mask, s, NEG_INF)
            m = jnp.max(s, axis=-1, keepdims=True)
            p = jnp.exp(s - m)
            den = jnp.sum(p, axis=-1, keepdims=True)
            o = jnp.dot(p.astype(MXU_DTYPE), vv, preferred_element_type=F32)
            o_ref[rows, sl] = o / den
            lse_tile = jnp.where(lane == hchunk * hpb + hh, m + jnp.log(den), lse_tile)
        lse_ref[rows, :] = lse_tile


def _attn_prompt(z, gi, dil, *, col_q, col_k, col_v):
    t = z.shape[0]
    gw = HEADS * HEAD_DIM
    hpb = HEADS if dil == 1 else 1
    cw = hpb * HEAD_DIM
    rb = BAND * dil
    per = gw // cw
    cur = lambda c: (lambda n, h: (n, (c + gi) * per + h))
    prev = lambda c: (lambda n, h: (jnp.maximum(n - 1, 0), (c + gi) * per + h))
    return pl.pallas_call(
        functools.partial(_attn_prompt_kernel, dil=dil, hpb=hpb),
        grid=(t // rb, HEADS // hpb),
        in_specs=[
            pl.BlockSpec((rb, cw), cur(col_q)),
            pl.BlockSpec((rb, cw), cur(col_k)),
            pl.BlockSpec((rb, cw), prev(col_k)),
            pl.BlockSpec((rb, cw), cur(col_v)),
            pl.BlockSpec((rb, cw), prev(col_v)),
        ],
        out_specs=[
            pl.BlockSpec((rb, cw), lambda n, h: (n, h)),
            pl.BlockSpec((rb, LANES), lambda n, h: (n, 0)),
        ],
        out_shape=[
            jax.ShapeDtypeStruct((t, gw), F32),
            jax.ShapeDtypeStruct((t, LANES), F32),
        ],
        compiler_params=_params("parallel", "arbitrary"),
        name=f"attn_prompt_g{gi}",
    )(z, z, z, z, z)


def _attn_sample_kernel(q_ref, kn_ref, vn_ref, c_ref, nx_ref, o_ref, lse_ref, new_ref,
                        m_sc, den_sc, acc_sc, *, dil, steps):
    ci = pl.program_id(1)
    nchunks = pl.num_programs(1)
    rch = c_ref.shape[1]

    def attend(s, krows, vrows, valid, first):
        q = q_ref[0, s] * (HEAD_DIM ** -0.5)
        sc = jnp.sum(q[None] * krows, axis=-1, keepdims=True)
        if valid is not None:
            sc = jnp.where(valid, sc, NEG_INF)
        m_blk = jnp.max(sc, axis=0)
        if first:
            m_new = m_blk
            p = jnp.exp(sc - m_new[None])
            den = jnp.sum(p, axis=0)
            acc = jnp.sum(p * vrows, axis=0)
        else:
            m_old = m_sc[s][:, 0:1]
            m_new = jnp.maximum(m_old, m_blk)
            alpha = jnp.exp(m_old - m_new)
            p = jnp.exp(sc - m_new[None])
            den = alpha * den_sc[s][:, 0:1] + jnp.sum(p, axis=0)
            acc = alpha * acc_sc[s] + jnp.sum(p * vrows, axis=0)
        m_sc[s] = jnp.broadcast_to(m_new, (HEADS, HEAD_DIM))
        den_sc[s] = jnp.broadcast_to(den, (HEADS, HEAD_DIM))
        acc_sc[s] = acc

    @pl.when(ci == 0)
    def _():
        for s in range(steps):
            lo = 0 if dil == 1 else s
            attend(s, kn_ref[0, lo:s + 1], vn_ref[0, lo:s + 1], None, True)

    for s in range(steps):
        if dil == 1:
            ridx = lax.broadcasted_iota(jnp.int32, (rch, 1, 1), 0) + ci * rch
            attend(s, c_ref[0, :, 0], c_ref[0, :, 1], ridx >= s, False)
        else:
            n = rch // dil
            attend(s, c_ref[0, pl.ds(s, n, stride=dil), 0], c_ref[0, pl.ds(s, n, stride=dil), 1],
                   None, False)

    @pl.when(ci == nchunks - 1)
    def _():
        for s in range(steps):
            o_ref[0, s] = acc_sc[s] / den_sc[s]
            lse_ref[0, s] = m_sc[s] + jnp.log(den_sc[s])

    new_ref[0, 0:rch - steps] = c_ref[0, steps:rch]

    @pl.when(ci < nchunks - 1)
    def _():
        new_ref[0, rch - steps:rch] = nx_ref[0]

    @pl.when(ci == nchunks - 1)
    def _():
        new_ref[0, rch - steps:rch, 0] = kn_ref[0]
        new_ref[0, rch - steps:rch, 1] = vn_ref[0]


def _attn_sample(q, kn, vn, cache, dil, *, rch):
    nb, steps = q.shape[0], q.shape[1]
    lg = cache.shape[1]
    nchunks = lg // rch
    per = rch // steps
    last = lg // steps - 1
    small = pl.BlockSpec((1, steps, HEADS, HEAD_DIM), lambda b, c: (b, 0, 0, 0))
    big = pl.BlockSpec((1, rch, 2, HEADS, HEAD_DIM), lambda b, c: (b, c, 0, 0, 0))
    nxt = pl.BlockSpec((1, steps, 2, HEADS, HEAD_DIM),
                       lambda b, c: (b, jnp.minimum((c + 1) * per, last), 0, 0, 0))
    return pl.pallas_call(
        functools.partial(_attn_sample_kernel, dil=dil, steps=steps),
        grid=(nb, nchunks),
        in_specs=[small, small, small, big, nxt],
        out_specs=[small, small, big],
        out_shape=[
            jax.ShapeDtypeStruct(q.shape, F32),
            jax.ShapeDtypeStruct(q.shape, F32),
            jax.ShapeDtypeStruct(cache.shape, cache.dtype),
        ],
        scratch_shapes=[pltpu.VMEM((steps, HEADS, HEAD_DIM), F32)] * 3,
        compiler_params=_params("parallel", "arbitrary"),
        name=f"attn_sample_d{dil}",
    )(q, kn, vn, cache, cache)


def _merge_kernel(g_ref, o0_ref, o1_ref, o2_ref, l0_ref, l1_ref, l2_ref, ga_ref, gb_ref,
                  wr_ref, wa_ref, out_ref, att_sc):
    @pl.when(pl.program_id(1) == 0)
    def _():
        l0, l1, l2 = l0_ref[...], l1_ref[...], l2_ref[...]
        mx = jnp.maximum(jnp.maximum(l0, l1), l2)
        e0, e1, e2 = jnp.exp(l0 - mx), jnp.exp(l1 - mx), jnp.exp(l2 - mx)
        inv = 1.0 / (e0 + e1 + e2)
        w0, w1, w2 = e0 * inv, e1 * inv, e2 * inv
        for h in range(HEADS):
            sl = slice(h * HEAD_DIM, (h + 1) * HEAD_DIM)
            att = (w0[:, h:h + 1] * o0_ref[:, sl] + w1[:, h:h + 1] * o1_ref[:, sl]
                   + w2[:, h:h + 1] * o2_ref[:, sl])
            att_sc[:, sl] = att.astype(att_sc.dtype)

    rnn_out = jnp.dot(g_ref[...], wr_ref[...], preferred_element_type=F32)
    att_out = jnp.dot(att_sc[...], wa_ref[...], preferred_element_type=F32)
    mixed = jax.nn.sigmoid(ga_ref[...]) * rnn_out + jax.nn.sigmoid(gb_ref[...]) * att_out
    out_ref[...] = mixed.astype(out_ref.dtype)


def _merge(g, outs, lses, z, w_rnn, w_att, *, col_ga, col_gb, tm, tn):
    t, d = g.shape
    gw = HEADS * HEAD_DIM
    nj = d // tn
    row = lambda i, j: (i, 0)
    return pl.pallas_call(
        _merge_kernel,
        grid=(t // tm, nj),
        in_specs=[
            pl.BlockSpec((tm, d), row),
            pl.BlockSpec((tm, gw), row), pl.BlockSpec((tm, gw), row), pl.BlockSpec((tm, gw), row),
            pl.BlockSpec((tm, LANES), row), pl.BlockSpec((tm, LANES), row),
            pl.BlockSpec((tm, LANES), row),
            pl.BlockSpec((tm, tn), lambda i, j: (i, col_ga + j)),
            pl.BlockSpec((tm, tn), lambda i, j: (i, col_gb + j)),
            pl.BlockSpec((d, tn), lambda i, j: (0, j)),
            pl.BlockSpec((gw, tn), lambda i, j: (0, j)),
        ],
        out_specs=pl.BlockSpec((tm, tn), lambda i, j: (i, j)),
        out_shape=jax.ShapeDtypeStruct((t, d), MXU_DTYPE),
        scratch_shapes=[pltpu.VMEM((tm, gw), MXU_DTYPE)],
        compiler_params=_params("parallel", "arbitrary"),
        name="merge",
    )(g, *outs, *lses, z, z, w_rnn, w_att)


def _oproj_kernel(x_ref, mx_ref, wo_ref, gain_ref, h_ref, hn_ref, hnt_ref):
    h = x_ref[...] + jnp.dot(mx_ref[...], wo_ref[...], preferred_element_type=F32)
    h_ref[...] = h
    hn = _rms(h, gain_ref[...])
    hn_ref[...] = hn.astype(hn_ref.dtype)
    hnt_ref[...] = hn.T.astype(hnt_ref.dtype)


def _oproj(x, mixed, w_o, gain, *, tm):
    t, d = x.shape
    row = lambda i: (i, 0)
    return pl.pallas_call(
        _oproj_kernel,
        grid=(t // tm,),
        in_specs=[
            pl.BlockSpec((tm, d), row),
            pl.BlockSpec((tm, d), row),
            pl.BlockSpec((d, d), lambda i: (0, 0)),
            pl.BlockSpec((1, d), lambda i: (0, 0)),
        ],
        out_specs=[pl.BlockSpec((tm, d), row), pl.BlockSpec((tm, d), row),
                   pl.BlockSpec((d, tm), lambda i: (0, i))],
        out_shape=[jax.ShapeDtypeStruct((t, d), F32), jax.ShapeDtypeStruct((t, d), MXU_DTYPE),
                   jax.ShapeDtypeStruct((d, t), MXU_DTYPE)],
        compiler_params=_params("parallel"),
        name="oproj",
    )(x, mixed, w_o, gain)


def _all_sublanes(x, op):
    for sh in (4, 2, 1):
        x = op(x, pltpu.roll(x, sh, 0))
    return x


def _take_ranked(vals, order, count):
    rest = vals
    rank = jnp.full(vals.shape, float(count), F32)
    tops, firsts = [], []
    for k in range(count):
        m = _all_sublanes(jnp.max(rest, axis=0), jnp.maximum)
        first = _all_sublanes(jnp.min(jnp.where(rest == m[None], order, POS_INF), axis=0),
                              jnp.minimum)
        hit = order == first[None]
        rest = jnp.where(hit, NEG_INF, rest)
        rank = jnp.where(hit, float(k), rank)
        tops.append(m)
        firsts.append(first)
    return tops, firsts, rank


def _sort_pairs(n):
    pairs, p = [], 1
    while p < n:
        k = p
        while k >= 1:
            for j in range(k % p, n - k, 2 * k):
                for i in range(min(k, n - j - k)):
                    if (i + j) // (2 * p) == (i + j + k) // (2 * p):
                        pairs.append((i + j, i + j + k))
            k //= 2
        p *= 2
    return tuple(pairs)


def _bitonic_pairs(n):
    pairs, dist = [], n // 2
    while dist >= 1:
        pairs += [(i, i + dist) for i in range(n) if (i // dist) % 2 == 0]
        dist //= 2
    return tuple(pairs)


_SORT_TOPK = _sort_pairs(PEER_TOPK)
_MERGE_TOPK = _bitonic_pairs(PEER_TOPK)


def _top_values(vals):
    nt = PEER_TOPK
    v = [vals[k] for k in range(nt)]

    def exchange(pairs):
        for a, b in pairs:
            v[a], v[b] = jnp.maximum(v[a], v[b]), jnp.minimum(v[a], v[b])

    exchange(_SORT_TOPK)
    for sh in (4, 2, 1):
        other = [pltpu.roll(x, sh, 0) for x in v]
        v = [jnp.maximum(v[k], other[nt - 1 - k]) for k in range(nt)]
        exchange(_MERGE_TOPK)
    return v


def _route_lanes_by_value(s0, s1):
    nt = PEER_TOPK
    nk, ln = s0.shape
    nv = nk // SUBLANES
    assert nv == nt == 2 * SUBLANES
    s0 = s0.reshape(nv, SUBLANES, ln)
    s1 = s1.reshape(nv, SUBLANES, ln)
    top0 = _top_values(s0)
    top1 = _top_values(s1)
    sub = lax.broadcasted_iota(jnp.int32, (SUBLANES, ln), 0)

    def on_sublanes(rows):
        out = rows[0]
        for b in range(1, SUBLANES):
            out = jnp.where(sub == b, rows[b], out)
        return out

    def count(mask):
        return _all_sublanes(jnp.where(mask, 1.0, 0.0), jnp.add)

    t1lo, t1hi = on_sublanes(top1[:SUBLANES]), on_sublanes(top1[SUBLANES:])
    t0hi = on_sublanes(top0[SUBLANES:])
    pieces = [top0[0] + t1lo, top0[0] + t1hi]
    for a in range(1, SUBLANES):
        pieces.append(jnp.where(sub < nt // (a + 1), top0[a] + t1lo, NEG_INF))
    pieces.append(t0hi + top1[0])
    pad = jnp.full((SUBLANES, ln), NEG_INF, F32)
    best = _top_values(jnp.stack(pieces + [pad] * (nt - len(pieces))))
    cut = best[nt - 1]
    zsum = jnp.ones((SUBLANES, ln), F32)
    for k in range(1, nt):
        zsum = zsum + jnp.exp(best[k] - best[0])

    counts = [count(pieces[0] >= cut) + count(pieces[1] >= cut)]
    counts += [count(pieces[a + 1] >= cut) for a in range(1, SUBLANES)]
    total = count(pieces[SUBLANES + 1] >= cut)
    th_vals = []
    for a in range(SUBLANES):
        total = total + counts[a]
        tv = jnp.full((SUBLANES, ln), POS_INF, F32)
        for m in range(nt // (a + 1)):
            tv = jnp.where(counts[a] == float(m + 1), top1[m], tv)
        th_vals.append(tv)

    member0 = s0 >= top0[nt - 1][None]
    member1 = s1 >= top1[nt - 1][None]
    th = jnp.where(member0 & (s0 + top1[0][None] >= cut[None]), top1[0][None], POS_INF)
    for a in range(SUBLANES):
        th = jnp.where(s0 == top0[a][None], th_vals[a][None], th)
    c = jnp.exp(s0 - top0[0][None]) / zsum[None]
    r1 = jnp.where(member1, s1, NEG_INF)
    e1 = jnp.exp(s1 - top1[0][None])

    gap = top0[0] - top0[1]
    for t in (top0, top1):
        for k in range(nt - 1):
            gap = jnp.minimum(gap, t[k] - t[k + 1])
    n0 = _all_sublanes(jnp.sum(jnp.where(member0, 1.0, 0.0), axis=0), jnp.add)
    n1 = _all_sublanes(jnp.sum(jnp.where(member1, 1.0, 0.0), axis=0), jnp.add)
    tie = (jnp.where(gap <= 0.0, 1.0, 0.0) + jnp.abs(n0 - float(nt)) + jnp.abs(n1 - float(nt))
           + jnp.abs(total - float(nt)))
    return tuple(x.reshape(nk, ln) for x in (th, c, r1, e1)) + (tie,)


def _route_lanes(s0, s1):
    th, c, r1, e1, tie = _route_lanes_by_value(s0, s1)
    return lax.cond(jnp.max(tie) > 0.0, lambda: _route_lanes_by_rank(s0, s1),
                    lambda: (th, c, r1, e1))


def _route_lanes_by_rank(s0, s1):
    nt = PEER_TOPK
    nk, ln = s0.shape
    nv = nk // SUBLANES
    s0 = s0.reshape(nv, SUBLANES, ln)
    s1 = s1.reshape(nv, SUBLANES, ln)
    key = (lax.broadcasted_iota(jnp.int32, (nv, SUBLANES, ln), 0) * SUBLANES
           + lax.broadcasted_iota(jnp.int32, (nv, SUBLANES, ln), 1)).astype(F32)
    top0, _, rank0 = _take_ranked(s0, key, nt)
    top1, _, rank1 = _take_ranked(s1, key, nt)

    sub = lax.broadcasted_iota(jnp.int32, (SUBLANES, ln), 0)
    subf = sub.astype(F32)

    def on_sublanes(rows):
        out = rows[0]
        for b in range(1, SUBLANES):
            out = jnp.where(sub == b, rows[b], out)
        return out

    assert nt == 2 * SUBLANES
    t1lo, t1hi = on_sublanes(top1[:SUBLANES]), on_sublanes(top1[SUBLANES:])
    t0hi = on_sublanes(top0[SUBLANES:])
    vals = [top0[0] + t1lo, top0[0] + t1hi]
    poss = [subf, subf + SUBLANES]
    for a in range(1, SUBLANES):
        vals.append(jnp.where(sub < nt // (a + 1), top0[a] + t1lo, NEG_INF))
        poss.append(subf + a * nt)
    vals.append(t0hi + top1[0])
    poss.append((subf + SUBLANES) * nt)
    best, bpos, _ = _take_ranked(jnp.stack(vals), jnp.stack(poss), nt)

    zsum = jnp.zeros((SUBLANES, ln), F32)
    cnt_lo = jnp.zeros((SUBLANES, ln), F32)
    cnt_hi = jnp.zeros((SUBLANES, ln), F32)
    for k in range(nt):
        zsum = zsum + jnp.exp(best[k] - best[0])
        a_k = jnp.floor(bpos[k] * (1.0 / nt))
        cnt_lo = cnt_lo + jnp.where(subf == a_k, 1.0, 0.0)
        cnt_hi = cnt_hi + jnp.where(subf + SUBLANES == a_k, 1.0, 0.0)

    th = jnp.full((nv, SUBLANES, ln), POS_INF, F32)
    for a in range(nt):
        cnt = cnt_lo if a < SUBLANES else cnt_hi
        n_a = jnp.broadcast_to(cnt[a % SUBLANES:a % SUBLANES + 1], (SUBLANES, ln))
        th = jnp.where(rank0 == float(a), (1.0 - n_a)[None], th)
    c = jnp.exp(s0 - top0[0][None]) / zsum[None]
    r1 = jnp.where(rank1 < float(nt), -rank1, NEG_INF)
    e1 = jnp.exp(s1 - top1[0][None])
    return tuple(x.reshape(nk, ln) for x in (th, c, r1, e1))


def _route_kernel(hn_ref, wq_ref, sk_ref, th_ref, c_ref, r1_ref, e1_ref, q_sc):
    q = jnp.dot(hn_ref[...], wq_ref[...], preferred_element_type=F32)
    tm = hn_ref.shape[0]
    d_half = sk_ref.shape[2]
    for hp in range(2 * PEER_HEADS):
        q_sc[hp] = q[:, hp * d_half:(hp + 1) * d_half]

    def head(h, _):
        def scores(half):
            return lax.dot_general(sk_ref[2 * h + half], q_sc[2 * h + half],
                                   (((1,), (1,)), ((), ())),
                                   preferred_element_type=F32)

        s0, s1 = scores(0), scores(1)
        for lc in range(tm // LANES):
            ls = slice(lc * LANES, (lc + 1) * LANES)
            th, c, r1, e1 = _route_lanes(s0[:, ls], s1[:, ls])
            th_ref[h, :, ls] = th
            c_ref[h, :, ls] = c
            r1_ref[h, :, ls] = r1
            e1_ref[h, :, ls] = e1
        return 0

    lax.fori_loop(0, PEER_HEADS, head, 0)


def _route(hn, w_query, sub_keys, *, tm):
    t, d = hn.shape
    dq = w_query.shape[1]
    slab = jax.ShapeDtypeStruct((PEER_HEADS, N_KEYS, t), F32)
    slab_spec = pl.BlockSpec((PEER_HEADS, N_KEYS, tm), lambda i: (0, 0, i))
    return pl.pallas_call(
        _route_kernel,
        grid=(t // tm,),
        in_specs=[
            pl.BlockSpec((tm, d), lambda i: (i, 0)),
            pl.BlockSpec((d, dq), lambda i: (0, 0)),
            pl.BlockSpec(sub_keys.shape, lambda i: (0, 0, 0)),
        ],
        out_specs=[slab_spec] * 4,
        out_shape=[slab] * 4,
        scratch_shapes=[pltpu.VMEM((2 * PEER_HEADS, tm, sub_keys.shape[2]), F32)],
        compiler_params=_params("parallel"),
        name="peer_route",
    )(hn, w_query, sub_keys)


PEER_ROW_CHUNK = 32


def _peer_kernel(x_ref, u_ref, vt_ref, th_ref, c_ref, r1_ref, e1_ref, y_ref, act_sc, gate_sc,
                 coef_sc):
    e = pl.program_id(1)
    tm = x_ref.shape[1]
    ib = th_ref.shape[1]
    hk = ib // 2
    hr = hk * N_KEYS
    assert tm % LANES == 0 and N_KEYS % PEER_ROW_CHUNK == 0 and ib % 2 == 0

    @pl.when(e == 0)
    def _():
        y_ref[...] = jnp.zeros_like(y_ref)

    def first_matmul(half):
        rows = slice(half * hr, (half + 1) * hr)
        act_sc[rows, :] = jnp.dot(u_ref[rows, :], x_ref[...], preferred_element_type=F32)

    def routing_weights(half):
        for lc in range(tm // LANES):
            ls = slice(lc * LANES, (lc + 1) * LANES)
            for jc in range(N_KEYS // PEER_ROW_CHUNK):
                js = slice(jc * PEER_ROW_CHUNK, (jc + 1) * PEER_ROW_CHUNK)
                gates = [jnp.zeros((PEER_ROW_CHUNK, LANES), F32) for _ in range(hk)]
                for h in range(PEER_HEADS):
                    r1 = r1_ref[h, js, ls]
                    e1 = e1_ref[h, js, ls]
                    for k in range(hk):
                        ii = half * hk + k
                        th = th_ref[h, ii:ii + 1, ls]
                        ch = c_ref[h, ii:ii + 1, ls]
                        gates[k] = gates[k] + jnp.where(r1 >= th, e1 * ch, 0.0)
                for k in range(hk):
                    r0 = (half * hk + k) * N_KEYS + jc * PEER_ROW_CHUNK
                    gate_sc[r0:r0 + PEER_ROW_CHUNK, ls] = gates[k]

    def second_matmul(half):
        rows = slice(half * hr, (half + 1) * hr)
        coef_sc[rows, :] = (_gelu(act_sc[rows, :]) * gate_sc[rows, :]).astype(coef_sc.dtype)
        y_ref[...] += jnp.dot(vt_ref[:, rows], coef_sc[rows, :], preferred_element_type=F32)

    @pl.when(e >= 0)
    def _():
        first_matmul(0)
        routing_weights(0)

    @pl.when(e >= -1)
    def _():
        first_matmul(1)
        routing_weights(1)
        second_matmul(0)

    @pl.when(e >= -2)
    def _():
        second_matmul(1)


def _peer(hnt, u, vt, th, c, r1, e1, *, tm, ib):
    d, t = hnt.shape
    ne = u.shape[0]
    te = ib * N_KEYS
    slab_spec = pl.BlockSpec((PEER_HEADS, N_KEYS, tm), lambda i, e: (0, 0, i))
    first_spec = pl.BlockSpec((PEER_HEADS, ib, tm), lambda i, e: (0, e, i))
    return pl.pallas_call(
        _peer_kernel,
        grid=(t // tm, ne // te),
        in_specs=[
            pl.BlockSpec((d, tm), lambda i, e: (0, i)),
            pl.BlockSpec((te, d), lambda i, e: (e, 0)),
            pl.BlockSpec((d, te), lambda i, e: (0, e)),
            first_spec, first_spec, slab_spec, slab_spec,
        ],
        out_specs=pl.BlockSpec((d, tm), lambda i, e: (0, i)),
        out_shape=jax.ShapeDtypeStruct((d, t), F32),
        scratch_shapes=[pltpu.VMEM((te, tm), F32), pltpu.VMEM((te, tm), F32),
                        pltpu.VMEM((te, tm), MXU_DTYPE)],
        compiler_params=_params("parallel", "arbitrary"),
        name="peer_experts",
    )(hnt, u, vt, th, c, r1, e1)


def _final_kernel(h_ref, yt_ref, gain_ref, out_ref):
    out_ref[...] = _rms(h_ref[...] + yt_ref[...].T, gain_ref[...])


def _final(h, yt, gain, *, tm):
    t, d = h.shape
    return pl.pallas_call(
        _final_kernel,
        grid=(t // tm,),
        in_specs=[pl.BlockSpec((tm, d), lambda i: (i, 0)), pl.BlockSpec((d, tm), lambda i: (0, i)),
                  pl.BlockSpec((1, d), lambda i: (0, 0))],
        out_specs=pl.BlockSpec((tm, d), lambda i: (i, 0)),
        out_shape=jax.ShapeDtypeStruct((t, d), F32),
        compiler_params=_params("parallel"),
        name="final_norm",
    )(h, yt, gain)


def _tile(n, want):
    return min(n, want)


def _token_stages(x2, z, g, outs, lses, w, cols):
    t = x2.shape[0]
    mixed = _merge(g, outs, lses, z, w["w_rnn_out"], w["w_att_out"], col_ga=cols["ga"],
                   col_gb=cols["gb"], tm=_tile(t, 256), tn=HEADS * HEAD_DIM)
    h, hn, hnt = _oproj(x2, mixed, w["w_o"], w["norm_ffn"], tm=_tile(t, 256))
    th, c, s1, e1 = _route(hn, w["w_query"], w["sub_keys"], tm=_tile(t, 256))
    yt = _peer(hnt, w["expert_u"], w["expert_vt"], th, c, s1, e1, tm=_tile(t, 512), ib=SUBLANES)
    return _final(h, yt, w["norm_final"], tm=_tile(t, 256))


def kernel(x_prompt, x_sample, state_conv, state_h, cache_kv_w128, cache_kv_w512, cache_kv_w2048,
           norm_mix, w_in, conv_w, conv_b, w_gate_x, b_gate_x, w_gate_a, b_gate_a, lru_lambda,
           w_rnn_out, w_att_out, w_o, norm_ffn, w_query, sub_keys, expert_u, expert_v, norm_final):
    depth = w_in.shape[0]
    assert depth == 1 and x_prompt.shape[0] == 1
    batch, seq, d = x_prompt.shape
    nb, steps, _ = x_sample.shape
    d_in = w_in.shape[2]
    gw = HEADS * HEAD_DIM
    d_qkv = N_GROUPS * gw
    assert d_in == 4 * d + 3 * d_qkv and d % gw == 0
    caches = (cache_kv_w128, cache_kv_w512, cache_kv_w2048)
    col_q, col_k, col_v = 2 * d // gw, (2 * d + d_qkv) // gw, (2 * d + 2 * d_qkv) // gw
    cols = {"ga": (2 * d + 3 * d_qkv) // gw, "gb": (3 * d + 3 * d_qkv) // gw}

    mx = lambda a: a.astype(MXU_DTYPE)
    row = lambda a: a.reshape(1, -1).astype(F32)
    w = {
        "w_in": mx(w_in[0]), "w_rnn_out": mx(w_rnn_out[0]), "w_att_out": mx(w_att_out[0]),
        "w_o": mx(w_o[0]), "w_query": mx(w_query[0]),
        "sub_keys": sub_keys[0].reshape(2 * PEER_HEADS, N_KEYS, -1),
        "expert_u": mx(expert_u[0]), "expert_vt": mx(expert_v[0]).T,
        "norm_ffn": row(norm_ffn[0]), "norm_final": row(norm_final),
    }
    rnn_w = (conv_w[0], row(conv_b[0]), mx(w_gate_x[0]), row(b_gate_x[0]), mx(w_gate_a[0]),
             row(b_gate_a[0]), row(lru_lambda[0]))
    gain_mix = row(norm_mix[0])

    xp = x_prompt.reshape(seq, d)
    zp = _inproj(xp, gain_mix, w["w_in"], tm=_tile(seq, 1024), tn=gw)
    g_p, past_p, h_p = _rnn_prompt(zp, *rnn_w, jnp.zeros((SUBLANES, d), F32),
                                   jnp.zeros((1, d), F32), d=d, tt=_tile(seq, 256))
    outs, lses, kv_p = [], [], []
    for gi, (win, dil) in enumerate(ATT_GROUPS):
        o, lse = _attn_prompt(zp, gi, dil, col_q=col_q, col_k=col_k, col_v=col_v)
        outs.append(o)
        lses.append(lse)
        keep = min(win, seq)
        kcol = (col_k + gi) * gw
        vcol = (col_v + gi) * gw
        kv = jnp.stack([zp[seq - keep:, kcol:kcol + gw], zp[seq - keep:, vcol:vcol + gw]], axis=1)
        kv_p.append(kv.reshape(1, 1, keep, 2, HEADS, HEAD_DIM))
    y_p = _token_stages(xp, zp, g_p, outs, lses, w, cols).reshape(batch, seq, d)
    conv_p = past_p[SUBLANES - (CONV_W - 1):].reshape(1, 1, CONV_W - 1, d)
    h_p = h_p.reshape(1, 1, d)

    ts = nb * steps
    xs = x_sample.reshape(ts, d)
    zs = _inproj(xs, gain_mix, w["w_in"], tm=ts, tn=gw)
    g_s, conv_s, h_s = _rnn_sample(zs.reshape(nb, steps * d_in), *rnn_w,
                                   state_conv[0].reshape(nb, (CONV_W - 1) * d), state_h[0],
                                   d=d, steps=steps)
    g_s = g_s.reshape(ts, d)
    qkv = zs[:, 2 * d:2 * d + 3 * d_qkv].reshape(nb, steps, 3, N_GROUPS, HEADS, HEAD_DIM)
    outs, lses, kv_s = [], [], []
    for gi, (win, dil) in enumerate(ATT_GROUPS):
        cache = caches[gi][0]
        o, lse, new_cache = _attn_sample(qkv[:, :, 0, gi], qkv[:, :, 1, gi], qkv[:, :, 2, gi], cache,
                                         dil, rch=min(cache.shape[1], 512))
        outs.append(o.reshape(ts, gw))
        lse = lse[..., 0].reshape(ts, HEADS)
        lses.append(jnp.pad(lse, ((0, 0), (0, LANES - HEADS))))
        kv_s.append(new_cache[None])
    y_s = _token_stages(xs, zs, g_s, outs, lses, w, cols).reshape(nb, steps, d)
    conv_s = conv_s.reshape(1, nb, CONV_W - 1, d)
    h_s = h_s.reshape(1, nb, d)

    return (y_p, y_s, conv_p, h_p, kv_p[0], kv_p[1], kv_p[2],
            conv_s, h_s, kv_s[0], kv_s[1], kv_s[2])
```

```python
import functools
import math

import jax
import jax.numpy as jnp
from jax import lax
from jax.experimental import pallas as pl
from jax.experimental.pallas import tpu as pltpu

F32 = jnp.float32
MXU_DTYPE = jnp.bfloat16

RNN_BLOCKS = 8
CONV_W = 4
LRU_C = 8.0
ATT_GROUPS = ((128, 1), (512, 4), (2048, 16))
N_GROUPS = len(ATT_GROUPS)
HEADS = 8
HEAD_DIM = 128
BAND = 128
PEER_HEADS = 8
N_KEYS = 128
PEER_TOPK = 16
NORM_EPS = 1e-6

LANES = 128
SUBLANES = 8
VMEM_LIMIT_BYTES = 56 * 1024 * 1024

NEG_INF = float("-inf")
POS_INF = float("inf")


def _params(*sem):
    return pltpu.CompilerParams(dimension_semantics=sem, vmem_limit_bytes=VMEM_LIMIT_BYTES)


def _gelu(x):
    return 0.5 * x * (1.0 + lax.erf(x * math.sqrt(0.5)))


def _rms(x, gain):
    ms = jnp.mean(x * x, axis=-1, keepdims=True)
    return x * lax.rsqrt(ms + NORM_EPS) * gain


def _inproj_kernel(x_ref, g_ref, w_ref, z_ref, xn_sc):
    @pl.when(pl.program_id(1) == 0)
    def _():
        xn_sc[...] = _rms(x_ref[...], g_ref[...]).astype(MXU_DTYPE)

    z_ref[...] = jnp.dot(xn_sc[...], w_ref[...], preferred_element_type=F32)


def _inproj(x, gain, w, *, tm, tn):
    m, k = x.shape
    n = w.shape[1]
    return pl.pallas_call(
        _inproj_kernel,
        grid=(m // tm, n // tn),
        in_specs=[
            pl.BlockSpec((tm, k), lambda i, j: (i, 0)),
            pl.BlockSpec((1, k), lambda i, j: (0, 0)),
            pl.BlockSpec((k, tn), lambda i, j: (0, j)),
        ],
        out_specs=pl.BlockSpec((tm, tn), lambda i, j: (i, j)),
        out_shape=jax.ShapeDtypeStruct((m, n), F32),
        scratch_shapes=[pltpu.VMEM((tm, k), MXU_DTYPE)],
        compiler_params=_params("parallel", "arbitrary"),
        name="inproj",
    )(x, gain, w)


def _lru_gates(xc, wgx, bgx, wga, bga, log_sig):
    xb = xc.astype(MXU_DTYPE)
    gx = jax.nn.sigmoid(jnp.dot(xb, wgx, preferred_element_type=F32) + bgx)
    ga = jax.nn.sigmoid(jnp.dot(xb, wga, preferred_element_type=F32) + bga)
    a = jnp.exp(LRU_C * ga * log_sig)
    b = jnp.sqrt(1.0 - a * a) * gx * xc
    return a, b


def _log_sigmoid(x):
    return -(jnp.maximum(-x, 0.0) + jnp.log1p(jnp.exp(-jnp.abs(x))))


def _rnn_prompt_kernel(xr_ref, yr_ref, cw_ref, cb_ref, wgx_ref, bgx_ref, wga_ref, bga_ref, lam_ref,
                       p0_ref, h0_ref, g_ref, pn_ref, hn_ref, p_sc, h_sc, a_sc, b_sc):
    tt, d = xr_ref.shape
    bw = d // RNN_BLOCKS

    @pl.when(pl.program_id(0) == 0)
    def _():
        p_sc[...] = p0_ref[...]
        h_sc[...] = h0_ref[...]

    x = xr_ref[...]
    past = p_sc[...]
    row8 = lax.broadcasted_iota(jnp.int32, (SUBLANES, d), 0)
    xc = cb_ref[...] + cw_ref[0:1, :] * x
    for j in range(1, CONV_W):
        xs = pltpu.roll(x, j, 0)
        head = jnp.where(row8 < j, pltpu.roll(past, j, 0), xs[:SUBLANES])
        xs = jnp.concatenate([head, xs[SUBLANES:]], axis=0)
        xc = xc + cw_ref[j:j + 1, :] * xs
    p_sc[...] = x[tt - SUBLANES:]
    pn_ref[...] = x[tt - SUBLANES:]

    log_sig = _log_sigmoid(lam_ref[...])
    rowm = lax.broadcasted_iota(jnp.int32, (tt, bw), 0) & (SUBLANES - 1)
    for n in range(RNN_BLOCKS):
        sl = slice(n * bw, (n + 1) * bw)
        a, b = _lru_gates(xc[:, sl], wgx_ref[n], bgx_ref[:, sl], wga_ref[n], bga_ref[:, sl],
                          log_sig[:, sl])
        for sh in (1, 2, 4):
            ok = rowm >= sh
            a_sh = pltpu.roll(a, sh, 0)
            b_sh = pltpu.roll(b, sh, 0)
            b = jnp.where(ok, a * b_sh + b, b)
            a = jnp.where(ok, a * a_sh, a)
        a_sc[:, sl] = a
        b_sc[:, sl] = b

    def group(gi, h):
        rows = pl.ds(pl.multiple_of(gi * SUBLANES, SUBLANES), SUBLANES)
        hh = a_sc[rows, :] * h + b_sc[rows, :]
        b_sc[rows, :] = hh
        return hh[SUBLANES - 1:SUBLANES, :]

    h_last = lax.fori_loop(0, tt // SUBLANES, group, h_sc[...])
    h_sc[...] = h_last
    hn_ref[...] = h_last
    g_ref[...] = (_gelu(yr_ref[...]) * b_sc[...]).astype(g_ref.dtype)


def _rnn_prompt(z, conv_w, conv_b, wgx, bgx, wga, bga, lam, past8, h0, *, d, tt):
    t = z.shape[0]
    full2 = lambda i: (0, 0)
    full3 = lambda i: (0, 0, 0)
    bw = d // RNN_BLOCKS
    return pl.pallas_call(
        _rnn_prompt_kernel,
        grid=(t // tt,),
        in_specs=[
            pl.BlockSpec((tt, d), lambda i: (i, 0)),
            pl.BlockSpec((tt, d), lambda i: (i, 1)),
            pl.BlockSpec((CONV_W, d), full2),
            pl.BlockSpec((1, d), full2),
            pl.BlockSpec((RNN_BLOCKS, bw, bw), full3),
            pl.BlockSpec((1, d), full2),
            pl.BlockSpec((RNN_BLOCKS, bw, bw), full3),
            pl.BlockSpec((1, d), full2),
            pl.BlockSpec((1, d), full2),
            pl.BlockSpec((SUBLANES, d), full2),
            pl.BlockSpec((1, d), full2),
        ],
        out_specs=[
            pl.BlockSpec((tt, d), lambda i: (i, 0)),
            pl.BlockSpec((SUBLANES, d), full2),
            pl.BlockSpec((1, d), full2),
        ],
        out_shape=[
            jax.ShapeDtypeStruct((t, d), MXU_DTYPE),
            jax.ShapeDtypeStruct((SUBLANES, d), F32),
            jax.ShapeDtypeStruct((1, d), F32),
        ],
        scratch_shapes=[pltpu.VMEM((SUBLANES, d), F32), pltpu.VMEM((1, d), F32),
                        pltpu.VMEM((tt, d), F32), pltpu.VMEM((tt, d), F32)],
        compiler_params=_params("arbitrary"),
        name="rnn_prompt",
    )(z, z, conv_w, conv_b, wgx, bgx, wga, bga, lam, past8, h0)


def _rnn_sample_kernel(z_ref, cw_ref, cb_ref, wgx_ref, bgx_ref, wga_ref, bga_ref, lam_ref,
                       cs_ref, h0_ref, g_ref, cn_ref, hn_ref, *, d, d_in, steps):
    bw = d // RNN_BLOCKS
    log_sig = _log_sigmoid(lam_ref[...])
    hist = [cs_ref[:, j * d:(j + 1) * d] for j in range(CONV_W - 1)]
    h = [h0_ref[:, n * bw:(n + 1) * bw] for n in range(RNN_BLOCKS)]
    for s in range(steps):
        x = z_ref[:, s * d_in:s * d_in + d]
        hist.append(x)
        xc = cb_ref[...]
        for j in range(CONV_W):
            xc = xc + cw_ref[j:j + 1, :] * hist[-1 - j]
        for n in range(RNN_BLOCKS):
            sl = slice(n * bw, (n + 1) * bw)
            a, b = _lru_gates(xc[:, sl], wgx_ref[n], bgx_ref[:, sl], wga_ref[n], bga_ref[:, sl],
                              log_sig[:, sl])
            h[n] = a * h[n] + b
            y = z_ref[:, s * d_in + d + n * bw:s * d_in + d + (n + 1) * bw]
            g_ref[:, s * d + n * bw:s * d + (n + 1) * bw] = (_gelu(y) * h[n]).astype(g_ref.dtype)
    for j in range(CONV_W - 1):
        cn_ref[:, j * d:(j + 1) * d] = hist[len(hist) - (CONV_W - 1) + j]
    for n in range(RNN_BLOCKS):
        hn_ref[:, n * bw:(n + 1) * bw] = h[n]


def _rnn_sample(z2, conv_w, conv_b, wgx, bgx, wga, bga, lam, conv_state, h0, *, d, steps):
    nb = z2.shape[0]
    d_in = z2.shape[1] // steps
    return pl.pallas_call(
        functools.partial(_rnn_sample_kernel, d=d, d_in=d_in, steps=steps),
        out_shape=[
            jax.ShapeDtypeStruct((nb, steps * d), MXU_DTYPE),
            jax.ShapeDtypeStruct((nb, (CONV_W - 1) * d), F32),
            jax.ShapeDtypeStruct((nb, d), F32),
        ],
        compiler_params=pltpu.CompilerParams(vmem_limit_bytes=VMEM_LIMIT_BYTES),
        name="rnn_sample",
    )(z2, conv_w, conv_b, wgx, bgx, wga, bga, lam, conv_state, h0)


def _attn_prompt_kernel(q_ref, kc_ref, kp_ref, vc_ref, vp_ref, o_ref, lse_ref, *, dil, hpb, nbs):
    nblk = pl.program_id(0)
    hchunk = pl.program_id(1)
    span = BAND * dil
    qi = lax.broadcasted_iota(jnp.int32, (BAND, 2 * BAND), 0)
    ki = lax.broadcasted_iota(jnp.int32, (BAND, 2 * BAND), 1)
    window = (ki >= qi) & (ki <= qi + BAND)
    first_key = jnp.where(nblk > 0, 0, BAND)
    lane = lax.broadcasted_iota(jnp.int32, (BAND, LANES), 1)
    scale = HEAD_DIM ** -0.5

    @pl.when(hchunk == 0)
    def _():
        lse_ref[...] = jnp.zeros_like(lse_ref)

    def band_rows(b, r):
        return pl.ds(b * span + r, BAND, stride=dil) if dil > 1 else slice(b * span, (b + 1) * span)

    for b in range(nbs):
        mask = window & (ki >= first_key) if b == 0 else window
        for r in range(dil):
            rows = band_rows(b, r)
            if b == 0:
                k_prev, v_prev, prows = kp_ref, vp_ref, band_rows(0, r)
            else:
                k_prev, v_prev, prows = kc_ref, vc_ref, band_rows(b - 1, r)
            lse_tile = lse_ref[rows, :]
            for hh in range(hpb):
                sl = slice(hh * HEAD_DIM, (hh + 1) * HEAD_DIM)
                qh = (q_ref[rows, sl] * scale).astype(MXU_DTYPE)
                kk = jnp.concatenate([k_prev[prows, sl], kc_ref[rows, sl]], axis=0).astype(MXU_DTYPE)
                vv = jnp.concatenate([v_prev[prows, sl], vc_ref[rows, sl]], axis=0).astype(MXU_DTYPE)
                s = lax.dot_general(qh, kk, (((1,), (1,)), ((), ())), preferred_element_type=F32)
                s = jnp.where(mask, s, NEG_INF)
                m = jnp.max(s, axis=-1, keepdims=True)
                p = jnp.exp(s - m)
                den = jnp.sum(p, axis=-1, keepdims=True)
                o = jnp.dot(p.astype(MXU_DTYPE), vv, preferred_element_type=F32)
                o_ref[rows, sl] = o / den
                lse_tile = jnp.where(lane == hchunk * hpb + hh, m + jnp.log(den), lse_tile)
            lse_ref[rows, :] = lse_tile


ATTN_BANDS_PER_STEP = 4
ATTN_MAX_BLOCK_ROWS = 4096


def _attn_prompt(z, gi, dil, *, col_q, col_k, col_v):
    t = z.shape[0]
    gw = HEADS * HEAD_DIM
    hpb = HEADS if dil == 1 else 1
    cw = hpb * HEAD_DIM
    span = BAND * dil
    nbs = math.gcd(ATTN_BANDS_PER_STEP, t // span)
    while nbs > 1 and nbs * span > ATTN_MAX_BLOCK_ROWS:
        nbs //= 2
    rb = nbs * span
    per = gw // cw
    cur = lambda c: (lambda n, h: (n, (c + gi) * per + h))
    prev = lambda c: (lambda n, h: (jnp.maximum(n * nbs - 1, 0), (c + gi) * per + h))
    return pl.pallas_call(
        functools.partial(_attn_prompt_kernel, dil=dil, hpb=hpb, nbs=nbs),
        grid=(t // rb, HEADS // hpb),
        in_specs=[
            pl.BlockSpec((rb, cw), cur(col_q)),
            pl.BlockSpec((rb, cw), cur(col_k)),
            pl.BlockSpec((span, cw), prev(col_k)),
            pl.BlockSpec((rb, cw), cur(col_v)),
            pl.BlockSpec((span, cw), prev(col_v)),
        ],
        out_specs=[
            pl.BlockSpec((rb, cw), lambda n, h: (n, h)),
            pl.BlockSpec((rb, LANES), lambda n, h: (n, 0)),
        ],
        out_shape=[
            jax.ShapeDtypeStruct((t, gw), F32),
            jax.ShapeDtypeStruct((t, LANES), F32),
        ],
        compiler_params=_params("parallel", "arbitrary"),
        name=f"attn_prompt_g{gi}",
    )(z, z, z, z, z)


def _attn_sample_kernel(q_ref, kn_ref, vn_ref, c_ref, nx_ref, o_ref, lse_ref, new_ref,
                        m_sc, den_sc, acc_sc, *, dil, steps):
    ci = pl.program_id(1)
    nchunks = pl.num_programs(1)
    rch = c_ref.shape[1]

    def attend(s, krows, vrows, valid, first):
        q = q_ref[0, s] * (HEAD_DIM ** -0.5)
        sc = jnp.sum(q[None] * krows, axis=-1, keepdims=True)
        if valid is not None:
            sc = jnp.where(valid, sc, NEG_INF)
        m_blk = jnp.max(sc, axis=0)
        if first:
            m_new = m_blk
            p = jnp.exp(sc - m_new[None])
            den = jnp.sum(p, axis=0)
            acc = jnp.sum(p * vrows, axis=0)
        else:
            m_old = m_sc[s][:, 0:1]
            m_new = jnp.maximum(m_old, m_blk)
            alpha = jnp.exp(m_old - m_new)
            p = jnp.exp(sc - m_new[None])
            den = alpha * den_sc[s][:, 0:1] + jnp.sum(p, axis=0)
            acc = alpha * acc_sc[s] + jnp.sum(p * vrows, axis=0)
        m_sc[s] = jnp.broadcast_to(m_new, (HEADS, HEAD_DIM))
        den_sc[s] = jnp.broadcast_to(den, (HEADS, HEAD_DIM))
        acc_sc[s] = acc

    @pl.when(ci == 0)
    def _():
        for s in range(steps):
            lo = 0 if dil == 1 else s
            attend(s, kn_ref[0, lo:s + 1], vn_ref[0, lo:s + 1], None, True)

    for s in range(steps):
        if dil == 1:
            ridx = lax.broadcasted_iota(jnp.int32, (rch, 1, 1), 0) + ci * rch
            attend(s, c_ref[0, :, 0], c_ref[0, :, 1], ridx >= s, False)
        else:
            n = rch // dil
            attend(s, c_ref[0, pl.ds(s, n, stride=dil), 0], c_ref[0, pl.ds(s, n, stride=dil), 1],
                   None, False)

    @pl.when(ci == nchunks - 1)
    def _():
        for s in range(steps):
            o_ref[0, s] = acc_sc[s] / den_sc[s]
            lse_ref[0, s] = m_sc[s] + jnp.log(den_sc[s])

    new_ref[0, 0:rch - steps] = c_ref[0, steps:rch]

    @pl.when(ci < nchunks - 1)
    def _():
        new_ref[0, rch - steps:rch] = nx_ref[0]

    @pl.when(ci == nchunks - 1)
    def _():
        new_ref[0, rch - steps:rch, 0] = kn_ref[0]
        new_ref[0, rch - steps:rch, 1] = vn_ref[0]


def _attn_sample(q, kn, vn, cache, dil, *, rch):
    nb, steps = q.shape[0], q.shape[1]
    lg = cache.shape[1]
    nchunks = lg // rch
    per = rch // steps
    last = lg // steps - 1
    small = pl.BlockSpec((1, steps, HEADS, HEAD_DIM), lambda b, c: (b, 0, 0, 0))
    big = pl.BlockSpec((1, rch, 2, HEADS, HEAD_DIM), lambda b, c: (b, c, 0, 0, 0))
    nxt = pl.BlockSpec((1, steps, 2, HEADS, HEAD_DIM),
                       lambda b, c: (b, jnp.minimum((c + 1) * per, last), 0, 0, 0))
    return pl.pallas_call(
        functools.partial(_attn_sample_kernel, dil=dil, steps=steps),
        grid=(nb, nchunks),
        in_specs=[small, small, small, big, nxt],
        out_specs=[small, small, big],
        out_shape=[
            jax.ShapeDtypeStruct(q.shape, F32),
            jax.ShapeDtypeStruct(q.shape, F32),
            jax.ShapeDtypeStruct(cache.shape, cache.dtype),
        ],
        scratch_shapes=[pltpu.VMEM((steps, HEADS, HEAD_DIM), F32)] * 3,
        compiler_params=_params("parallel", "arbitrary"),
        name=f"attn_sample_d{dil}",
    )(q, kn, vn, cache, cache)


def _merge_kernel(g_ref, o0_ref, o1_ref, o2_ref, l0_ref, l1_ref, l2_ref, ga_ref, gb_ref,
                  wr_ref, wa_ref, out_ref, att_sc):
    @pl.when(pl.program_id(1) == 0)
    def _():
        l0, l1, l2 = l0_ref[...], l1_ref[...], l2_ref[...]
        mx = jnp.maximum(jnp.maximum(l0, l1), l2)
        e0, e1, e2 = jnp.exp(l0 - mx), jnp.exp(l1 - mx), jnp.exp(l2 - mx)
        inv = 1.0 / (e0 + e1 + e2)
        w0, w1, w2 = e0 * inv, e1 * inv, e2 * inv
        for h in range(HEADS):
            sl = slice(h * HEAD_DIM, (h + 1) * HEAD_DIM)
            att = (w0[:, h:h + 1] * o0_ref[:, sl] + w1[:, h:h + 1] * o1_ref[:, sl]
                   + w2[:, h:h + 1] * o2_ref[:, sl])
            att_sc[:, sl] = att.astype(att_sc.dtype)

    rnn_out = jnp.dot(g_ref[...], wr_ref[...], preferred_element_type=F32)
    att_out = jnp.dot(att_sc[...], wa_ref[...], preferred_element_type=F32)
    mixed = jax.nn.sigmoid(ga_ref[...]) * rnn_out + jax.nn.sigmoid(gb_ref[...]) * att_out
    out_ref[...] = mixed.astype(out_ref.dtype)


def _merge(g, outs, lses, z, w_rnn, w_att, *, col_ga, col_gb, tm, tn):
    t, d = g.shape
    gw = HEADS * HEAD_DIM
    nj = d // tn
    row = lambda i, j: (i, 0)
    return pl.pallas_call(
        _merge_kernel,
        grid=(t // tm, nj),
        in_specs=[
            pl.BlockSpec((tm, d), row),
            pl.BlockSpec((tm, gw), row), pl.BlockSpec((tm, gw), row), pl.BlockSpec((tm, gw), row),
            pl.BlockSpec((tm, LANES), row), pl.BlockSpec((tm, LANES), row),
            pl.BlockSpec((tm, LANES), row),
            pl.BlockSpec((tm, tn), lambda i, j: (i, col_ga + j)),
            pl.BlockSpec((tm, tn), lambda i, j: (i, col_gb + j)),
            pl.BlockSpec((d, tn), lambda i, j: (0, j)),
            pl.BlockSpec((gw, tn), lambda i, j: (0, j)),
        ],
        out_specs=pl.BlockSpec((tm, tn), lambda i, j: (i, j)),
        out_shape=jax.ShapeDtypeStruct((t, d), MXU_DTYPE),
        scratch_shapes=[pltpu.VMEM((tm, gw), MXU_DTYPE)],
        compiler_params=_params("parallel", "arbitrary"),
        name="merge",
    )(g, *outs, *lses, z, z, w_rnn, w_att)


def _oproj_kernel(x_ref, mx_ref, wo_ref, gain_ref, h_ref, hn_ref, hnt_ref):
    h = x_ref[...] + jnp.dot(mx_ref[...], wo_ref[...], preferred_element_type=F32)
    h_ref[...] = h
    hn = _rms(h, gain_ref[...])
    hn_ref[...] = hn.astype(hn_ref.dtype)
    hnt_ref[...] = hn.T.astype(hnt_ref.dtype)


def _oproj(x, mixed, w_o, gain, *, tm):
    t, d = x.shape
    row = lambda i: (i, 0)
    return pl.pallas_call(
        _oproj_kernel,
        grid=(t // tm,),
        in_specs=[
            pl.BlockSpec((tm, d), row),
            pl.BlockSpec((tm, d), row),
            pl.BlockSpec((d, d), lambda i: (0, 0)),
            pl.BlockSpec((1, d), lambda i: (0, 0)),
        ],
        out_specs=[pl.BlockSpec((tm, d), row), pl.BlockSpec((tm, d), row),
                   pl.BlockSpec((d, tm), lambda i: (0, i))],
        out_shape=[jax.ShapeDtypeStruct((t, d), F32), jax.ShapeDtypeStruct((t, d), MXU_DTYPE),
                   jax.ShapeDtypeStruct((d, t), MXU_DTYPE)],
        compiler_params=_params("parallel"),
        name="oproj",
    )(x, mixed, w_o, gain)


def _all_sublanes(x, op):
    for sh in (4, 2, 1):
        x = op(x, pltpu.roll(x, sh, 0))
    return x


def _take_ranked(vals, order, count):
    rest = vals
    rank = jnp.full(vals.shape, float(count), F32)
    tops, firsts = [], []
    for k in range(count):
        m = _all_sublanes(jnp.max(rest, axis=0), jnp.maximum)
        first = _all_sublanes(jnp.min(jnp.where(rest == m[None], order, POS_INF), axis=0),
                              jnp.minimum)
        hit = order == first[None]
        rest = jnp.where(hit, NEG_INF, rest)
        rank = jnp.where(hit, float(k), rank)
        tops.append(m)
        firsts.append(first)
    return tops, firsts, rank


def _sort_pairs(n):
    pairs, p = [], 1
    while p < n:
        k = p
        while k >= 1:
            for j in range(k % p, n - k, 2 * k):
                for i in range(min(k, n - j - k)):
                    if (i + j) // (2 * p) == (i + j + k) // (2 * p):
                        pairs.append((i + j, i + j + k))
            k //= 2
        p *= 2
    return tuple(pairs)


def _bitonic_pairs(n):
    pairs, dist = [], n // 2
    while dist >= 1:
        pairs += [(i, i + dist) for i in range(n) if (i // dist) % 2 == 0]
        dist //= 2
    return tuple(pairs)


_SORT_TOPK = _sort_pairs(PEER_TOPK)
_MERGE_TOPK = _bitonic_pairs(PEER_TOPK)


def _top_values(vals):
    nt = PEER_TOPK
    v = [vals[k] for k in range(nt)]

    def exchange(pairs):
        for a, b in pairs:
            v[a], v[b] = jnp.maximum(v[a], v[b]), jnp.minimum(v[a], v[b])

    exchange(_SORT_TOPK)
    for sh in (4, 2, 1):
        other = [pltpu.roll(x, sh, 0) for x in v]
        v = [jnp.maximum(v[k], other[nt - 1 - k]) for k in range(nt)]
        exchange(_MERGE_TOPK)
    return v


def _route_lanes_by_value(s0, s1):
    nt = PEER_TOPK
    nk, ln = s0.shape
    nv = nk // SUBLANES
    assert nv == nt == 2 * SUBLANES
    s0 = s0.reshape(nv, SUBLANES, ln)
    s1 = s1.reshape(nv, SUBLANES, ln)
    top0 = _top_values(s0)
    top1 = _top_values(s1)
    sub = lax.broadcasted_iota(jnp.int32, (SUBLANES, ln), 0)

    def on_sublanes(rows):
        out = rows[0]
        for b in range(1, SUBLANES):
            out = jnp.where(sub == b, rows[b], out)
        return out

    def count(mask):
        return _all_sublanes(jnp.where(mask, 1.0, 0.0), jnp.add)

    t1lo, t1hi = on_sublanes(top1[:SUBLANES]), on_sublanes(top1[SUBLANES:])
    t0hi = on_sublanes(top0[SUBLANES:])
    pieces = [top0[0] + t1lo, top0[0] + t1hi]
    for a in range(1, SUBLANES):
        pieces.append(jnp.where(sub < nt // (a + 1), top0[a] + t1lo, NEG_INF))
    pieces.append(t0hi + top1[0])
    pad = jnp.full((SUBLANES, ln), NEG_INF, F32)
    best = _top_values(jnp.stack(pieces + [pad] * (nt - len(pieces))))
    cut = best[nt - 1]
    zsum = jnp.ones((SUBLANES, ln), F32)
    for k in range(1, nt):
        zsum = zsum + jnp.exp(best[k] - best[0])

    counts = [count(pieces[0] >= cut) + count(pieces[1] >= cut)]
    counts += [count(pieces[a + 1] >= cut) for a in range(1, SUBLANES)]
    total = count(pieces[SUBLANES + 1] >= cut)
    th_vals = []
    for a in range(SUBLANES):
        total = total + counts[a]
        tv = jnp.full((SUBLANES, ln), POS_INF, F32)
        for m in range(nt // (a + 1)):
            tv = jnp.where(counts[a] == float(m + 1), top1[m], tv)
        th_vals.append(tv)

    member0 = s0 >= top0[nt - 1][None]
    member1 = s1 >= top1[nt - 1][None]
    th = jnp.where(member0 & (s0 + top1[0][None] >= cut[None]), top1[0][None], POS_INF)
    for a in range(SUBLANES):
        th = jnp.where(s0 == top0[a][None], th_vals[a][None], th)
    c = jnp.exp(s0 - top0[0][None]) / zsum[None]
    r1 = jnp.where(member1, s1, NEG_INF)
    e1 = jnp.exp(s1 - top1[0][None])

    gap = top0[0] - top0[1]
    for t in (top0, top1):
        for k in range(nt - 1):
            gap = jnp.minimum(gap, t[k] - t[k + 1])
    n0 = _all_sublanes(jnp.sum(jnp.where(member0, 1.0, 0.0), axis=0), jnp.add)
    n1 = _all_sublanes(jnp.sum(jnp.where(member1, 1.0, 0.0), axis=0), jnp.add)
    tie = (jnp.where(gap <= 0.0, 1.0, 0.0) + jnp.abs(n0 - float(nt)) + jnp.abs(n1 - float(nt))
           + jnp.abs(total - float(nt)))
    return tuple(x.reshape(nk, ln) for x in (th, c, r1, e1)) + (tie,)


def _route_lanes(s0, s1):
    th, c, r1, e1, tie = _route_lanes_by_value(s0, s1)
    return lax.cond(jnp.max(tie) > 0.0, lambda: _route_lanes_by_rank(s0, s1),
                    lambda: (th, c, r1, e1))


def _route_lanes_by_rank(s0, s1):
    nt = PEER_TOPK
    nk, ln = s0.shape
    nv = nk // SUBLANES
    s0 = s0.reshape(nv, SUBLANES, ln)
    s1 = s1.reshape(nv, SUBLANES, ln)
    key = (lax.broadcasted_iota(jnp.int32, (nv, SUBLANES, ln), 0) * SUBLANES
           + lax.broadcasted_iota(jnp.int32, (nv, SUBLANES, ln), 1)).astype(F32)
    top0, _, rank0 = _take_ranked(s0, key, nt)
    top1, _, rank1 = _take_ranked(s1, key, nt)

    sub = lax.broadcasted_iota(jnp.int32, (SUBLANES, ln), 0)
    subf = sub.astype(F32)

    def on_sublanes(rows):
        out = rows[0]
        for b in range(1, SUBLANES):
            out = jnp.where(sub == b, rows[b], out)
        return out

    assert nt == 2 * SUBLANES
    t1lo, t1hi = on_sublanes(top1[:SUBLANES]), on_sublanes(top1[SUBLANES:])
    t0hi = on_sublanes(top0[SUBLANES:])
    vals = [top0[0] + t1lo, top0[0] + t1hi]
    poss = [subf, subf + SUBLANES]
    for a in range(1, SUBLANES):
        vals.append(jnp.where(sub < nt // (a + 1), top0[a] + t1lo, NEG_INF))
        poss.append(subf + a * nt)
    vals.append(t0hi + top1[0])
    poss.append((subf + SUBLANES) * nt)
    best, bpos, _ = _take_ranked(jnp.stack(vals), jnp.stack(poss), nt)

    zsum = jnp.zeros((SUBLANES, ln), F32)
    cnt_lo = jnp.zeros((SUBLANES, ln), F32)
    cnt_hi = jnp.zeros((SUBLANES, ln), F32)
    for k in range(nt):
        zsum = zsum + jnp.exp(best[k] - best[0])
        a_k = jnp.floor(bpos[k] * (1.0 / nt))
        cnt_lo = cnt_lo + jnp.where(subf == a_k, 1.0, 0.0)
        cnt_hi = cnt_hi + jnp.where(subf + SUBLANES == a_k, 1.0, 0.0)

    th = jnp.full((nv, SUBLANES, ln), POS_INF, F32)
    for a in range(nt):
        cnt = cnt_lo if a < SUBLANES else cnt_hi
        n_a = jnp.broadcast_to(cnt[a % SUBLANES:a % SUBLANES + 1], (SUBLANES, ln))
        th = jnp.where(rank0 == float(a), (1.0 - n_a)[None], th)
    c = jnp.exp(s0 - top0[0][None]) / zsum[None]
    r1 = jnp.where(rank1 < float(nt), -rank1, NEG_INF)
    e1 = jnp.exp(s1 - top1[0][None])
    return tuple(x.reshape(nk, ln) for x in (th, c, r1, e1))


def _route_kernel(hn_ref, wq_ref, sk_ref, th_ref, c_ref, r1_ref, e1_ref, q_sc):
    q = jnp.dot(hn_ref[...], wq_ref[...], preferred_element_type=F32)
    tm = hn_ref.shape[0]
    d_half = sk_ref.shape[2]
    for hp in range(2 * PEER_HEADS):
        q_sc[hp] = q[:, hp * d_half:(hp + 1) * d_half]

    def head(h, _):
        def scores(half):
            return lax.dot_general(sk_ref[2 * h + half], q_sc[2 * h + half],
                                   (((1,), (1,)), ((), ())),
                                   preferred_element_type=F32)

        s0, s1 = scores(0), scores(1)
        for lc in range(tm // LANES):
            ls = slice(lc * LANES, (lc + 1) * LANES)
            th, c, r1, e1 = _route_lanes(s0[:, ls], s1[:, ls])
            th_ref[h, :, ls] = th
            c_ref[h, :, ls] = c
            r1_ref[h, :, ls] = r1
            e1_ref[h, :, ls] = e1
        return 0

    lax.fori_loop(0, PEER_HEADS, head, 0)


def _route(hn, w_query, sub_keys, *, tm):
    t, d = hn.shape
    dq = w_query.shape[1]
    slab = jax.ShapeDtypeStruct((PEER_HEADS, N_KEYS, t), F32)
    slab_spec = pl.BlockSpec((PEER_HEADS, N_KEYS, tm), lambda i: (0, 0, i))
    return pl.pallas_call(
        _route_kernel,
        grid=(t // tm,),
        in_specs=[
            pl.BlockSpec((tm, d), lambda i: (i, 0)),
            pl.BlockSpec((d, dq), lambda i: (0, 0)),
            pl.BlockSpec(sub_keys.shape, lambda i: (0, 0, 0)),
        ],
        out_specs=[slab_spec] * 4,
        out_shape=[slab] * 4,
        scratch_shapes=[pltpu.VMEM((2 * PEER_HEADS, tm, sub_keys.shape[2]), F32)],
        compiler_params=_params("parallel"),
        name="peer_route",
    )(hn, w_query, sub_keys)


PEER_ROW_CHUNK = 32


def _peer_kernel(x_ref, u_ref, vt_ref, th_ref, c_ref, r1_ref, e1_ref, y_ref, act_sc, gate_sc,
                 coef_sc):
    e = pl.program_id(1)
    tm = x_ref.shape[1]
    ib = th_ref.shape[1]
    hk = ib // 2
    hr = hk * N_KEYS
    assert tm % LANES == 0 and N_KEYS % PEER_ROW_CHUNK == 0 and ib % 2 == 0

    @pl.when(e == 0)
    def _():
        y_ref[...] = jnp.zeros_like(y_ref)

    def first_matmul(half):
        rows = slice(half * hr, (half + 1) * hr)
        act_sc[rows, :] = jnp.dot(u_ref[rows, :], x_ref[...], preferred_element_type=F32)

    def routing_weights(half):
        for lc in range(tm // LANES):
            ls = slice(lc * LANES, (lc + 1) * LANES)
            for jc in range(N_KEYS // PEER_ROW_CHUNK):
                js = slice(jc * PEER_ROW_CHUNK, (jc + 1) * PEER_ROW_CHUNK)
                gates = [jnp.zeros((PEER_ROW_CHUNK, LANES), F32) for _ in range(hk)]
                for h in range(PEER_HEADS):
                    r1 = r1_ref[h, js, ls]
                    e1 = e1_ref[h, js, ls]
                    for k in range(hk):
                        ii = half * hk + k
                        th = th_ref[h, ii:ii + 1, ls]
                        ch = c_ref[h, ii:ii + 1, ls]
                        gates[k] = gates[k] + jnp.where(r1 >= th, e1 * ch, 0.0)
                for k in range(hk):
                    r0 = (half * hk + k) * N_KEYS + jc * PEER_ROW_CHUNK
                    gate_sc[r0:r0 + PEER_ROW_CHUNK, ls] = gates[k]

    def second_matmul(half):
        rows = slice(half * hr, (half + 1) * hr)
        coef_sc[rows, :] = (_gelu(act_sc[rows, :]) * gate_sc[rows, :]).astype(coef_sc.dtype)
        y_ref[...] += jnp.dot(vt_ref[:, rows], coef_sc[rows, :], preferred_element_type=F32)

    @pl.when(e >= 0)
    def _():
        first_matmul(0)
        routing_weights(0)

    @pl.when(e >= -1)
    def _():
        first_matmul(1)
        routing_weights(1)
        second_matmul(0)

    @pl.when(e >= -2)
    def _():
        second_matmul(1)


def _peer(hnt, u, vt, th, c, r1, e1, *, tm, ib):
    d, t = hnt.shape
    ne = u.shape[0]
    te = ib * N_KEYS
    slab_spec = pl.BlockSpec((PEER_HEADS, N_KEYS, tm), lambda i, e: (0, 0, i))
    first_spec = pl.BlockSpec((PEER_HEADS, ib, tm), lambda i, e: (0, e, i))
    return pl.pallas_call(
        _peer_kernel,
        grid=(t // tm, ne // te),
        in_specs=[
            pl.BlockSpec((d, tm), lambda i, e: (0, i)),
            pl.BlockSpec((te, d), lambda i, e: (e, 0)),
            pl.BlockSpec((d, te), lambda i, e: (0, e)),
            first_spec, first_spec, slab_spec, slab_spec,
        ],
        out_specs=pl.BlockSpec((d, tm), lambda i, e: (0, i)),
        out_shape=jax.ShapeDtypeStruct((d, t), F32),
        scratch_shapes=[pltpu.VMEM((te, tm), F32), pltpu.VMEM((te, tm), F32),
                        pltpu.VMEM((te, tm), MXU_DTYPE)],
        compiler_params=_params("parallel", "arbitrary"),
        name="peer_experts",
    )(hnt, u, vt, th, c, r1, e1)


def _final_kernel(h_ref, yt_ref, gain_ref, out_ref):
    out_ref[...] = _rms(h_ref[...] + yt_ref[...].T, gain_ref[...])


def _final(h, yt, gain, *, tm):
    t, d = h.shape
    return pl.pallas_call(
        _final_kernel,
        grid=(t // tm,),
        in_specs=[pl.BlockSpec((tm, d), lambda i: (i, 0)), pl.BlockSpec((d, tm), lambda i: (0, i)),
                  pl.BlockSpec((1, d), lambda i: (0, 0))],
        out_specs=pl.BlockSpec((tm, d), lambda i: (i, 0)),
        out_shape=jax.ShapeDtypeStruct((t, d), F32),
        compiler_params=_params("parallel"),
        name="final_norm",
    )(h, yt, gain)


TOKEN_TILES = {"inproj": 1024, "rnn": 256, "merge": 512, "oproj": 256, "route": 256,
               "peer": 512, "final": 256}


def _tile(n, stage):
    return min(n, TOKEN_TILES[stage])


def _token_stages(x2, z, g, outs, lses, w, cols):
    t = x2.shape[0]
    mixed = _merge(g, outs, lses, z, w["w_rnn_out"], w["w_att_out"], col_ga=cols["ga"],
                   col_gb=cols["gb"], tm=_tile(t, "merge"), tn=HEADS * HEAD_DIM)
    h, hn, hnt = _oproj(x2, mixed, w["w_o"], w["norm_ffn"], tm=_tile(t, "oproj"))
    th, c, r1, e1 = _route(hn, w["w_query"], w["sub_keys"], tm=_tile(t, "route"))
    yt = _peer(hnt, w["expert_u"], w["expert_vt"], th, c, r1, e1, tm=_tile(t, "peer"),
               ib=SUBLANES)
    return _final(h, yt, w["norm_final"], tm=_tile(t, "final"))


def kernel(x_prompt, x_sample, state_conv, state_h, cache_kv_w128, cache_kv_w512, cache_kv_w2048,
           norm_mix, w_in, conv_w, conv_b, w_gate_x, b_gate_x, w_gate_a, b_gate_a, lru_lambda,
           w_rnn_out, w_att_out, w_o, norm_ffn, w_query, sub_keys, expert_u, expert_v, norm_final):
    depth = w_in.shape[0]
    assert depth == 1 and x_prompt.shape[0] == 1
    batch, seq, d = x_prompt.shape
    nb, steps, _ = x_sample.shape
    d_in = w_in.shape[2]
    gw = HEADS * HEAD_DIM
    d_qkv = N_GROUPS * gw
    assert d_in == 4 * d + 3 * d_qkv and d % gw == 0
    caches = (cache_kv_w128, cache_kv_w512, cache_kv_w2048)
    col_q, col_k, col_v = 2 * d // gw, (2 * d + d_qkv) // gw, (2 * d + 2 * d_qkv) // gw
    cols = {"ga": (2 * d + 3 * d_qkv) // gw, "gb": (3 * d + 3 * d_qkv) // gw}

    mx = lambda a: a.astype(MXU_DTYPE)
    row = lambda a: a.reshape(1, -1).astype(F32)
    w = {
        "w_in": mx(w_in[0]), "w_rnn_out": mx(w_rnn_out[0]), "w_att_out": mx(w_att_out[0]),
        "w_o": mx(w_o[0]), "w_query": mx(w_query[0]),
        "sub_keys": sub_keys[0].reshape(2 * PEER_HEADS, N_KEYS, -1),
        "expert_u": mx(expert_u[0]), "expert_vt": mx(expert_v[0]).T,
        "norm_ffn": row(norm_ffn[0]), "norm_final": row(norm_final),
    }
    rnn_w = (conv_w[0], row(conv_b[0]), mx(w_gate_x[0]), row(b_gate_x[0]), mx(w_gate_a[0]),
             row(b_gate_a[0]), row(lru_lambda[0]))
    gain_mix = row(norm_mix[0])

    xp = x_prompt.reshape(seq, d)
    zp = _inproj(xp, gain_mix, w["w_in"], tm=_tile(seq, "inproj"), tn=gw)
    g_p, past_p, h_p = _rnn_prompt(zp, *rnn_w, jnp.zeros((SUBLANES, d), F32),
                                   jnp.zeros((1, d), F32), d=d, tt=_tile(seq, "rnn"))
    outs, lses, kv_p = [], [], []
    for gi, (win, dil) in enumerate(ATT_GROUPS):
        o, lse = _attn_prompt(zp, gi, dil, col_q=col_q, col_k=col_k, col_v=col_v)
        outs.append(o)
        lses.append(lse)
        keep = min(win, seq)
        kcol = (col_k + gi) * gw
        vcol = (col_v + gi) * gw
        kv = jnp.stack([zp[seq - keep:, kcol:kcol + gw], zp[seq - keep:, vcol:vcol + gw]], axis=1)
        kv_p.append(kv.reshape(1, 1, keep, 2, HEADS, HEAD_DIM))
    y_p = _token_stages(xp, zp, g_p, outs, lses, w, cols).reshape(batch, seq, d)
    conv_p = past_p[SUBLANES - (CONV_W - 1):].reshape(1, 1, CONV_W - 1, d)
    h_p = h_p.reshape(1, 1, d)

    ts = nb * steps
    xs = x_sample.reshape(ts, d)
    zs = _inproj(xs, gain_mix, w["w_in"], tm=ts, tn=gw)
    g_s, conv_s, h_s = _rnn_sample(zs.reshape(nb, steps * d_in), *rnn_w,
                                   state_conv[0].reshape(nb, (CONV_W - 1) * d), state_h[0],
                                   d=d, steps=steps)
    g_s = g_s.reshape(ts, d)
    qkv = zs[:, 2 * d:2 * d + 3 * d_qkv].reshape(nb, steps, 3, N_GROUPS, HEADS, HEAD_DIM)
    outs, lses, kv_s = [], [], []
    for gi, (win, dil) in enumerate(ATT_GROUPS):
        cache = caches[gi][0]
        o, lse, new_cache = _attn_sample(qkv[:, :, 0, gi], qkv[:, :, 1, gi], qkv[:, :, 2, gi], cache,
                                         dil, rch=min(cache.shape[1], 512))
        outs.append(o.reshape(ts, gw))
        lse = lse[..., 0].reshape(ts, HEADS)
        lses.append(jnp.pad(lse, ((0, 0), (0, LANES - HEADS))))
        kv_s.append(new_cache[None])
    y_s = _token_stages(xs, zs, g_s, outs, lses, w, cols).reshape(nb, steps, d)
    conv_s = conv_s.reshape(1, nb, CONV_W - 1, d)
    h_s = h_s.reshape(1, nb, d)

    return (y_p, y_s, conv_p, h_p, kv_p[0], kv_p[1], kv_p[2],
            conv_s, h_s, kv_s[0], kv_s[1], kv_s[2])
```

```python
import functools
import math

import jax
import jax.numpy as jnp
from jax import lax
from jax.experimental import pallas as pl
from jax.experimental.pallas import tpu as pltpu

F32 = jnp.float32
MXU_DTYPE = jnp.bfloat16

RNN_BLOCKS = 8
CONV_W = 4
LRU_C = 8.0
ATT_GROUPS = ((128, 1), (512, 4), (2048, 16))
N_GROUPS = len(ATT_GROUPS)
HEADS = 8
HEAD_DIM = 128
BAND = 128
PEER_HEADS = 8
N_KEYS = 128
PEER_TOPK = 16
NORM_EPS = 1e-6

LANES = 128
SUBLANES = 8
VMEM_LIMIT_BYTES = 56 * 1024 * 1024

NEG_INF = float("-inf")
POS_INF = float("inf")


def _params(*sem):
    return pltpu.CompilerParams(dimension_semantics=sem, vmem_limit_bytes=VMEM_LIMIT_BYTES)


def _gelu(x):
    return 0.5 * x * (1.0 + lax.erf(x * math.sqrt(0.5)))


def _rms(x, gain):
    ms = jnp.mean(x * x, axis=-1, keepdims=True)
    return x * lax.rsqrt(ms + NORM_EPS) * gain


def _inproj_kernel(x_ref, g_ref, w_ref, z_ref, xn_sc):
    @pl.when(pl.program_id(1) == 0)
    def _():
        xn_sc[...] = _rms(x_ref[...], g_ref[...]).astype(MXU_DTYPE)

    z_ref[...] = jnp.dot(xn_sc[...], w_ref[...], preferred_element_type=F32)


def _inproj(x, gain, w, *, tm, tn):
    m, k = x.shape
    n = w.shape[1]
    return pl.pallas_call(
        _inproj_kernel,
        grid=(m // tm, n // tn),
        in_specs=[
            pl.BlockSpec((tm, k), lambda i, j: (i, 0)),
            pl.BlockSpec((1, k), lambda i, j: (0, 0)),
            pl.BlockSpec((k, tn), lambda i, j: (0, j)),
        ],
        out_specs=pl.BlockSpec((tm, tn), lambda i, j: (i, j)),
        out_shape=jax.ShapeDtypeStruct((m, n), F32),
        scratch_shapes=[pltpu.VMEM((tm, k), MXU_DTYPE)],
        compiler_params=_params("parallel", "arbitrary"),
        name="inproj",
    )(x, gain, w)


def _lru_gates(xc, wgx, bgx, wga, bga, log_sig):
    xb = xc.astype(MXU_DTYPE)
    gx = jax.nn.sigmoid(jnp.dot(xb, wgx, preferred_element_type=F32) + bgx)
    ga = jax.nn.sigmoid(jnp.dot(xb, wga, preferred_element_type=F32) + bga)
    a = jnp.exp(LRU_C * ga * log_sig)
    b = jnp.sqrt(1.0 - a * a) * gx * xc
    return a, b


def _log_sigmoid(x):
    return -(jnp.maximum(-x, 0.0) + jnp.log1p(jnp.exp(-jnp.abs(x))))


def _rnn_prompt_kernel(xr_ref, yr_ref, cw_ref, cb_ref, wgx_ref, bgx_ref, wga_ref, bga_ref, lam_ref,
                       p0_ref, h0_ref, g_ref, pn_ref, hn_ref, p_sc, h_sc, a_sc, b_sc):
    tt, d = xr_ref.shape
    bw = d // RNN_BLOCKS

    @pl.when(pl.program_id(0) == 0)
    def _():
        p_sc[...] = p0_ref[...]
        h_sc[...] = h0_ref[...]

    x = xr_ref[...]
    past = p_sc[...]
    row8 = lax.broadcasted_iota(jnp.int32, (SUBLANES, d), 0)
    xc = cb_ref[...] + cw_ref[0:1, :] * x
    for j in range(1, CONV_W):
        xs = pltpu.roll(x, j, 0)
        head = jnp.where(row8 < j, pltpu.roll(past, j, 0), xs[:SUBLANES])
        xs = jnp.concatenate([head, xs[SUBLANES:]], axis=0)
        xc = xc + cw_ref[j:j + 1, :] * xs
    p_sc[...] = x[tt - SUBLANES:]
    pn_ref[...] = x[tt - SUBLANES:]

    log_sig = _log_sigmoid(lam_ref[...])
    rowm = lax.broadcasted_iota(jnp.int32, (tt, bw), 0) & (SUBLANES - 1)
    for n in range(RNN_BLOCKS):
        sl = slice(n * bw, (n + 1) * bw)
        a, b = _lru_gates(xc[:, sl], wgx_ref[n], bgx_ref[:, sl], wga_ref[n], bga_ref[:, sl],
                          log_sig[:, sl])
        for sh in (1, 2, 4):
            ok = rowm >= sh
            a_sh = pltpu.roll(a, sh, 0)
            b_sh = pltpu.roll(b, sh, 0)
            b = jnp.where(ok, a * b_sh + b, b)
            a = jnp.where(ok, a * a_sh, a)
        a_sc[:, sl] = a
        b_sc[:, sl] = b

    def group(gi, h):
        rows = pl.ds(pl.multiple_of(gi * SUBLANES, SUBLANES), SUBLANES)
        hh = a_sc[rows, :] * h + b_sc[rows, :]
        b_sc[rows, :] = hh
        return hh[SUBLANES - 1:SUBLANES, :]

    h_last = lax.fori_loop(0, tt // SUBLANES, group, h_sc[...])
    h_sc[...] = h_last
    hn_ref[...] = h_last
    g_ref[...] = (_gelu(yr_ref[...]) * b_sc[...]).astype(g_ref.dtype)


def _rnn_prompt(z, conv_w, conv_b, wgx, bgx, wga, bga, lam, past8, h0, *, d, tt):
    t = z.shape[0]
    full2 = lambda i: (0, 0)
    full3 = lambda i: (0, 0, 0)
    bw = d // RNN_BLOCKS
    return pl.pallas_call(
        _rnn_prompt_kernel,
        grid=(t // tt,),
        in_specs=[
            pl.BlockSpec((tt, d), lambda i: (i, 0)),
            pl.BlockSpec((tt, d), lambda i: (i, 1)),
            pl.BlockSpec((CONV_W, d), full2),
            pl.BlockSpec((1, d), full2),
            pl.BlockSpec((RNN_BLOCKS, bw, bw), full3),
            pl.BlockSpec((1, d), full2),
            pl.BlockSpec((RNN_BLOCKS, bw, bw), full3),
            pl.BlockSpec((1, d), full2),
            pl.BlockSpec((1, d), full2),
            pl.BlockSpec((SUBLANES, d), full2),
            pl.BlockSpec((1, d), full2),
        ],
        out_specs=[
            pl.BlockSpec((tt, d), lambda i: (i, 0)),
            pl.BlockSpec((SUBLANES, d), full2),
            pl.BlockSpec((1, d), full2),
        ],
        out_shape=[
            jax.ShapeDtypeStruct((t, d), MXU_DTYPE),
            jax.ShapeDtypeStruct((SUBLANES, d), F32),
            jax.ShapeDtypeStruct((1, d), F32),
        ],
        scratch_shapes=[pltpu.VMEM((SUBLANES, d), F32), pltpu.VMEM((1, d), F32),
                        pltpu.VMEM((tt, d), F32), pltpu.VMEM((tt, d), F32)],
        compiler_params=_params("arbitrary"),
        name="rnn_prompt",
    )(z, z, conv_w, conv_b, wgx, bgx, wga, bga, lam, past8, h0)


def _rnn_sample_kernel(z_ref, cw_ref, cb_ref, wgx_ref, bgx_ref, wga_ref, bga_ref, lam_ref,
                       cs_ref, h0_ref, g_ref, cn_ref, hn_ref, *, d, d_in, steps):
    bw = d // RNN_BLOCKS
    log_sig = _log_sigmoid(lam_ref[...])
    hist = [cs_ref[:, j * d:(j + 1) * d] for j in range(CONV_W - 1)]
    h = [h0_ref[:, n * bw:(n + 1) * bw] for n in range(RNN_BLOCKS)]
    for s in range(steps):
        x = z_ref[:, s * d_in:s * d_in + d]
        hist.append(x)
        xc = cb_ref[...]
        for j in range(CONV_W):
            xc = xc + cw_ref[j:j + 1, :] * hist[-1 - j]
        for n in range(RNN_BLOCKS):
            sl = slice(n * bw, (n + 1) * bw)
            a, b = _lru_gates(xc[:, sl], wgx_ref[n], bgx_ref[:, sl], wga_ref[n], bga_ref[:, sl],
                              log_sig[:, sl])
            h[n] = a * h[n] + b
            y = z_ref[:, s * d_in + d + n * bw:s * d_in + d + (n + 1) * bw]
            g_ref[:, s * d + n * bw:s * d + (n + 1) * bw] = (_gelu(y) * h[n]).astype(g_ref.dtype)
    for j in range(CONV_W - 1):
        cn_ref[:, j * d:(j + 1) * d] = hist[len(hist) - (CONV_W - 1) + j]
    for n in range(RNN_BLOCKS):
        hn_ref[:, n * bw:(n + 1) * bw] = h[n]


def _rnn_sample(z2, conv_w, conv_b, wgx, bgx, wga, bga, lam, conv_state, h0, *, d, steps):
    nb = z2.shape[0]
    d_in = z2.shape[1] // steps
    return pl.pallas_call(
        functools.partial(_rnn_sample_kernel, d=d, d_in=d_in, steps=steps),
        out_shape=[
            jax.ShapeDtypeStruct((nb, steps * d), MXU_DTYPE),
            jax.ShapeDtypeStruct((nb, (CONV_W - 1) * d), F32),
            jax.ShapeDtypeStruct((nb, d), F32),
        ],
        compiler_params=pltpu.CompilerParams(vmem_limit_bytes=VMEM_LIMIT_BYTES),
        name="rnn_sample",
    )(z2, conv_w, conv_b, wgx, bgx, wga, bga, lam, conv_state, h0)


def _attn_prompt_kernel(q_ref, kc_ref, kp_ref, vc_ref, vp_ref, o_ref, lse_ref, *, dil, hpb, nbs):
    nblk = pl.program_id(0)
    hchunk = pl.program_id(1)
    span = BAND * dil
    qi = lax.broadcasted_iota(jnp.int32, (BAND, 2 * BAND), 0)
    ki = lax.broadcasted_iota(jnp.int32, (BAND, 2 * BAND), 1)
    window = (ki >= qi) & (ki <= qi + BAND)
    first_key = jnp.where(nblk > 0, 0, BAND)
    lane = lax.broadcasted_iota(jnp.int32, (BAND, LANES), 1)
    scale = HEAD_DIM ** -0.5

    @pl.when(hchunk == 0)
    def _():
        lse_ref[...] = jnp.zeros_like(lse_ref)

    def band_rows(b, r):
        return pl.ds(b * span + r, BAND, stride=dil) if dil > 1 else slice(b * span, (b + 1) * span)

    for b in range(nbs):
        mask = window & (ki >= first_key) if b == 0 else window
        for r in range(dil):
            rows = band_rows(b, r)
            if b == 0:
                k_prev, v_prev, prows = kp_ref, vp_ref, band_rows(0, r)
            else:
                k_prev, v_prev, prows = kc_ref, vc_ref, band_rows(b - 1, r)
            lse_tile = lse_ref[rows, :]
            for hh in range(hpb):
                sl = slice(hh * HEAD_DIM, (hh + 1) * HEAD_DIM)
                qh = (q_ref[rows, sl] * scale).astype(MXU_DTYPE)
                kk = jnp.concatenate([k_prev[prows, sl], kc_ref[rows, sl]], axis=0).astype(MXU_DTYPE)
                vv = jnp.concatenate([v_prev[prows, sl], vc_ref[rows, sl]], axis=0).astype(MXU_DTYPE)
                s = lax.dot_general(qh, kk, (((1,), (1,)), ((), ())), preferred_element_type=F32)
                s = jnp.where(mask, s, NEG_INF)
                m = jnp.max(s, axis=-1, keepdims=True)
                p = jnp.exp(s - m)
                den = jnp.sum(p, axis=-1, keepdims=True)
                o = jnp.dot(p.astype(MXU_DTYPE), vv, preferred_element_type=F32)
                o_ref[rows, sl] = o / den
                lse_tile = jnp.where(lane == hchunk * hpb + hh, m + jnp.log(den), lse_tile)
            lse_ref[rows, :] = lse_tile


ATTN_BANDS_PER_STEP = 4
ATTN_MAX_BLOCK_ROWS = 4096


def _attn_prompt(z, gi, dil, *, col_q, col_k, col_v):
    t = z.shape[0]
    gw = HEADS * HEAD_DIM
    hpb = HEADS if dil == 1 else 1
    cw = hpb * HEAD_DIM
    span = BAND * dil
    nbs = math.gcd(ATTN_BANDS_PER_STEP, t // span)
    while nbs > 1 and nbs * span > ATTN_MAX_BLOCK_ROWS:
        nbs //= 2
    rb = nbs * span
    per = gw // cw
    cur = lambda c: (lambda n, h: (n, (c + gi) * per + h))
    prev = lambda c: (lambda n, h: (jnp.maximum(n * nbs - 1, 0), (c + gi) * per + h))
    return pl.pallas_call(
        functools.partial(_attn_prompt_kernel, dil=dil, hpb=hpb, nbs=nbs),
        grid=(t // rb, HEADS // hpb),
        in_specs=[
            pl.BlockSpec((rb, cw), cur(col_q)),
            pl.BlockSpec((rb, cw), cur(col_k)),
            pl.BlockSpec((span, cw), prev(col_k)),
            pl.BlockSpec((rb, cw), cur(col_v)),
            pl.BlockSpec((span, cw), prev(col_v)),
        ],
        out_specs=[
            pl.BlockSpec((rb, cw), lambda n, h: (n, h)),
            pl.BlockSpec((rb, LANES), lambda n, h: (n, 0)),
        ],
        out_shape=[
            jax.ShapeDtypeStruct((t, gw), F32),
            jax.ShapeDtypeStruct((t, LANES), F32),
        ],
        compiler_params=_params("parallel", "arbitrary"),
        name=f"attn_prompt_g{gi}",
    )(z, z, z, z, z)


def _attn_sample_kernel(q_ref, kn_ref, vn_ref, c_ref, nx_ref, o_ref, lse_ref, new_ref,
                        m_sc, den_sc, acc_sc, *, dil, steps):
    ci = pl.program_id(1)
    nchunks = pl.num_programs(1)
    rch = c_ref.shape[1]

    def attend(s, krows, vrows, valid, first):
        q = q_ref[0, s] * (HEAD_DIM ** -0.5)
        sc = jnp.sum(q[None] * krows, axis=-1, keepdims=True)
        if valid is not None:
            sc = jnp.where(valid, sc, NEG_INF)
        m_blk = jnp.max(sc, axis=0)
        if first:
            m_new = m_blk
            p = jnp.exp(sc - m_new[None])
            den = jnp.sum(p, axis=0)
            acc = jnp.sum(p * vrows, axis=0)
        else:
            m_old = m_sc[s][:, 0:1]
            m_new = jnp.maximum(m_old, m_blk)
            alpha = jnp.exp(m_old - m_new)
            p = jnp.exp(sc - m_new[None])
            den = alpha * den_sc[s][:, 0:1] + jnp.sum(p, axis=0)
            acc = alpha * acc_sc[s] + jnp.sum(p * vrows, axis=0)
        m_sc[s] = jnp.broadcast_to(m_new, (HEADS, HEAD_DIM))
        den_sc[s] = jnp.broadcast_to(den, (HEADS, HEAD_DIM))
        acc_sc[s] = acc

    @pl.when(ci == 0)
    def _():
        for s in range(steps):
            lo = 0 if dil == 1 else s
            attend(s, kn_ref[0, lo:s + 1], vn_ref[0, lo:s + 1], None, True)

    for s in range(steps):
        if dil == 1:
            ridx = lax.broadcasted_iota(jnp.int32, (rch, 1, 1), 0) + ci * rch
            attend(s, c_ref[0, :, 0], c_ref[0, :, 1], ridx >= s, False)
        else:
            n = rch // dil
            attend(s, c_ref[0, pl.ds(s, n, stride=dil), 0], c_ref[0, pl.ds(s, n, stride=dil), 1],
                   None, False)

    @pl.when(ci == nchunks - 1)
    def _():
        for s in range(steps):
            o_ref[0, s] = acc_sc[s] / den_sc[s]
            lse_ref[0, s] = m_sc[s] + jnp.log(den_sc[s])

    new_ref[0, 0:rch - steps] = c_ref[0, steps:rch]

    @pl.when(ci < nchunks - 1)
    def _():
        new_ref[0, rch - steps:rch] = nx_ref[0]

    @pl.when(ci == nchunks - 1)
    def _():
        new_ref[0, rch - steps:rch, 0] = kn_ref[0]
        new_ref[0, rch - steps:rch, 1] = vn_ref[0]


def _attn_sample(q, kn, vn, cache, dil, *, rch):
    nb, steps = q.shape[0], q.shape[1]
    lg = cache.shape[1]
    nchunks = lg // rch
    per = rch // steps
    last = lg // steps - 1
    small = pl.BlockSpec((1, steps, HEADS, HEAD_DIM), lambda b, c: (b, 0, 0, 0))
    big = pl.BlockSpec((1, rch, 2, HEADS, HEAD_DIM), lambda b, c: (b, c, 0, 0, 0))
    nxt = pl.BlockSpec((1, steps, 2, HEADS, HEAD_DIM),
                       lambda b, c: (b, jnp.minimum((c + 1) * per, last), 0, 0, 0))
    return pl.pallas_call(
        functools.partial(_attn_sample_kernel, dil=dil, steps=steps),
        grid=(nb, nchunks),
        in_specs=[small, small, small, big, nxt],
        out_specs=[small, small, big],
        out_shape=[
            jax.ShapeDtypeStruct(q.shape, F32),
            jax.ShapeDtypeStruct(q.shape, F32),
            jax.ShapeDtypeStruct(cache.shape, cache.dtype),
        ],
        scratch_shapes=[pltpu.VMEM((steps, HEADS, HEAD_DIM), F32)] * 3,
        compiler_params=_params("parallel", "arbitrary"),
        name=f"attn_sample_d{dil}",
    )(q, kn, vn, cache, cache)


def _merge_kernel(g_ref, o0_ref, o1_ref, o2_ref, l0_ref, l1_ref, l2_ref, ga_ref, gb_ref,
                  wr_ref, wa_ref, out_ref, att_sc):
    @pl.when(pl.program_id(1) == 0)
    def _():
        l0, l1, l2 = l0_ref[...], l1_ref[...], l2_ref[...]
        mx = jnp.maximum(jnp.maximum(l0, l1), l2)
        e0, e1, e2 = jnp.exp(l0 - mx), jnp.exp(l1 - mx), jnp.exp(l2 - mx)
        inv = 1.0 / (e0 + e1 + e2)
        w0, w1, w2 = e0 * inv, e1 * inv, e2 * inv
        for h in range(HEADS):
            sl = slice(h * HEAD_DIM, (h + 1) * HEAD_DIM)
            att = (w0[:, h:h + 1] * o0_ref[:, sl] + w1[:, h:h + 1] * o1_ref[:, sl]
                   + w2[:, h:h + 1] * o2_ref[:, sl])
            att_sc[:, sl] = att.astype(att_sc.dtype)

    rnn_out = jnp.dot(g_ref[...], wr_ref[...], preferred_element_type=F32)
    att_out = jnp.dot(att_sc[...], wa_ref[...], preferred_element_type=F32)
    mixed = jax.nn.sigmoid(ga_ref[...]) * rnn_out + jax.nn.sigmoid(gb_ref[...]) * att_out
    out_ref[...] = mixed.astype(out_ref.dtype)


def _merge(g, outs, lses, z, w_rnn, w_att, *, col_ga, col_gb, tm, tn):
    t, d = g.shape
    gw = HEADS * HEAD_DIM
    nj = d // tn
    row = lambda i, j: (i, 0)
    return pl.pallas_call(
        _merge_kernel,
        grid=(t // tm, nj),
        in_specs=[
            pl.BlockSpec((tm, d), row),
            pl.BlockSpec((tm, gw), row), pl.BlockSpec((tm, gw), row), pl.BlockSpec((tm, gw), row),
            pl.BlockSpec((tm, LANES), row), pl.BlockSpec((tm, LANES), row),
            pl.BlockSpec((tm, LANES), row),
            pl.BlockSpec((tm, tn), lambda i, j: (i, col_ga + j)),
            pl.BlockSpec((tm, tn), lambda i, j: (i, col_gb + j)),
            pl.BlockSpec((d, tn), lambda i, j: (0, j)),
            pl.BlockSpec((gw, tn), lambda i, j: (0, j)),
        ],
        out_specs=pl.BlockSpec((tm, tn), lambda i, j: (i, j)),
        out_shape=jax.ShapeDtypeStruct((t, d), MXU_DTYPE),
        scratch_shapes=[pltpu.VMEM((tm, gw), MXU_DTYPE)],
        compiler_params=_params("parallel", "arbitrary"),
        name="merge",
    )(g, *outs, *lses, z, z, w_rnn, w_att)


def _oproj_kernel(x_ref, mx_ref, wo_ref, gain_ref, h_ref, hn_ref, hnt_ref):
    h = x_ref[...] + jnp.dot(mx_ref[...], wo_ref[...], preferred_element_type=F32)
    h_ref[...] = h
    hn = _rms(h, gain_ref[...])
    hn_ref[...] = hn.astype(hn_ref.dtype)
    hnt_ref[...] = hn.T.astype(hnt_ref.dtype)


def _oproj(x, mixed, w_o, gain, *, tm):
    t, d = x.shape
    row = lambda i: (i, 0)
    return pl.pallas_call(
        _oproj_kernel,
        grid=(t // tm,),
        in_specs=[
            pl.BlockSpec((tm, d), row),
            pl.BlockSpec((tm, d), row),
            pl.BlockSpec((d, d), lambda i: (0, 0)),
            pl.BlockSpec((1, d), lambda i: (0, 0)),
        ],
        out_specs=[pl.BlockSpec((tm, d), row), pl.BlockSpec((tm, d), row),
                   pl.BlockSpec((d, tm), lambda i: (0, i))],
        out_shape=[jax.ShapeDtypeStruct((t, d), F32), jax.ShapeDtypeStruct((t, d), MXU_DTYPE),
                   jax.ShapeDtypeStruct((d, t), MXU_DTYPE)],
        compiler_params=_params("parallel"),
        name="oproj",
    )(x, mixed, w_o, gain)


def _all_sublanes(x, op):
    for sh in (4, 2, 1):
        x = op(x, pltpu.roll(x, sh, 0))
    return x


def _take_ranked(vals, order, count):
    rest = vals
    rank = jnp.full(vals.shape, float(count), F32)
    tops, firsts = [], []
    for k in range(count):
        m = _all_sublanes(jnp.max(rest, axis=0), jnp.maximum)
        first = _all_sublanes(jnp.min(jnp.where(rest == m[None], order, POS_INF), axis=0),
                              jnp.minimum)
        hit = order == first[None]
        rest = jnp.where(hit, NEG_INF, rest)
        rank = jnp.where(hit, float(k), rank)
        tops.append(m)
        firsts.append(first)
    return tops, firsts, rank


def _sort_pairs(n):
    pairs, p = [], 1
    while p < n:
        k = p
        while k >= 1:
            for j in range(k % p, n - k, 2 * k):
                for i in range(min(k, n - j - k)):
                    if (i + j) // (2 * p) == (i + j + k) // (2 * p):
                        pairs.append((i + j, i + j + k))
            k //= 2
        p *= 2
    return tuple(pairs)


def _bitonic_pairs(n):
    pairs, dist = [], n // 2
    while dist >= 1:
        pairs += [(i, i + dist) for i in range(n) if (i // dist) % 2 == 0]
        dist //= 2
    return tuple(pairs)


_SORT_TOPK = _sort_pairs(PEER_TOPK)
_MERGE_TOPK = _bitonic_pairs(PEER_TOPK)


def _top_values(vals):
    nt = PEER_TOPK
    v = [vals[k] for k in range(nt)]

    def exchange(pairs):
        for a, b in pairs:
            v[a], v[b] = jnp.maximum(v[a], v[b]), jnp.minimum(v[a], v[b])

    exchange(_SORT_TOPK)
    for sh in (4, 2, 1):
        other = [pltpu.roll(x, sh, 0) for x in v]
        v = [jnp.maximum(v[k], other[nt - 1 - k]) for k in range(nt)]
        exchange(_MERGE_TOPK)
    return v


def _route_lanes_by_value(s0, s1):
    nt = PEER_TOPK
    nk, ln = s0.shape
    nv = nk // SUBLANES
    assert nv == nt == 2 * SUBLANES
    s0 = s0.reshape(nv, SUBLANES, ln)
    s1 = s1.reshape(nv, SUBLANES, ln)
    top0 = _top_values(s0)
    top1 = _top_values(s1)
    sub = lax.broadcasted_iota(jnp.int32, (SUBLANES, ln), 0)

    def on_sublanes(rows):
        out = rows[0]
        for b in range(1, SUBLANES):
            out = jnp.where(sub == b, rows[b], out)
        return out

    def count(mask):
        return _all_sublanes(jnp.where(mask, 1.0, 0.0), jnp.add)

    t1lo, t1hi = on_sublanes(top1[:SUBLANES]), on_sublanes(top1[SUBLANES:])
    t0hi = on_sublanes(top0[SUBLANES:])
    pieces = [top0[0] + t1lo, top0[0] + t1hi]
    for a in range(1, SUBLANES):
        pieces.append(jnp.where(sub < nt // (a + 1), top0[a] + t1lo, NEG_INF))
    pieces.append(t0hi + top1[0])
    pad = jnp.full((SUBLANES, ln), NEG_INF, F32)
    best = _top_values(jnp.stack(pieces + [pad] * (nt - len(pieces))))
    cut = best[nt - 1]
    zsum = jnp.ones((SUBLANES, ln), F32)
    for k in range(1, nt):
        zsum = zsum + jnp.exp(best[k] - best[0])

    counts = [count(pieces[0] >= cut) + count(pieces[1] >= cut)]
    counts += [count(pieces[a + 1] >= cut) for a in range(1, SUBLANES)]
    total = count(pieces[SUBLANES + 1] >= cut)
    th_vals = []
    for a in range(SUBLANES):
        total = total + counts[a]
        tv = jnp.full((SUBLANES, ln), POS_INF, F32)
        for m in range(nt // (a + 1)):
            tv = jnp.where(counts[a] == float(m + 1), top1[m], tv)
        th_vals.append(tv)

    member0 = s0 >= top0[nt - 1][None]
    member1 = s1 >= top1[nt - 1][None]
    th = jnp.where(member0 & (s0 + top1[0][None] >= cut[None]), top1[0][None], POS_INF)
    for a in range(SUBLANES):
        th = jnp.where(s0 == top0[a][None], th_vals[a][None], th)
    c = jnp.exp(s0 - top0[0][None]) / zsum[None]
    r1 = jnp.where(member1, s1, NEG_INF)
    e1 = jnp.exp(s1 - top1[0][None])

    gap = top0[0] - top0[1]
    for t in (top0, top1):
        for k in range(nt - 1):
            gap = jnp.minimum(gap, t[k] - t[k + 1])
    n0 = _all_sublanes(jnp.sum(jnp.where(member0, 1.0, 0.0), axis=0), jnp.add)
    n1 = _all_sublanes(jnp.sum(jnp.where(member1, 1.0, 0.0), axis=0), jnp.add)
    tie = (jnp.where(gap <= 0.0, 1.0, 0.0) + jnp.abs(n0 - float(nt)) + jnp.abs(n1 - float(nt))
           + jnp.abs(total - float(nt)))
    return tuple(x.reshape(nk, ln) for x in (th, c, r1, e1)) + (tie,)


def _route_lanes(s0, s1):
    th, c, r1, e1, tie = _route_lanes_by_value(s0, s1)
    return lax.cond(jnp.max(tie) > 0.0, lambda: _route_lanes_by_rank(s0, s1),
                    lambda: (th, c, r1, e1))


def _route_lanes_by_rank(s0, s1):
    nt = PEER_TOPK
    nk, ln = s0.shape
    nv = nk // SUBLANES
    s0 = s0.reshape(nv, SUBLANES, ln)
    s1 = s1.reshape(nv, SUBLANES, ln)
    key = (lax.broadcasted_iota(jnp.int32, (nv, SUBLANES, ln), 0) * SUBLANES
           + lax.broadcasted_iota(jnp.int32, (nv, SUBLANES, ln), 1)).astype(F32)
    top0, _, rank0 = _take_ranked(s0, key, nt)
    top1, _, rank1 = _take_ranked(s1, key, nt)

    sub = lax.broadcasted_iota(jnp.int32, (SUBLANES, ln), 0)
    subf = sub.astype(F32)

    def on_sublanes(rows):
        out = rows[0]
        for b in range(1, SUBLANES):
            out = jnp.where(sub == b, rows[b], out)
        return out

    assert nt == 2 * SUBLANES
    t1lo, t1hi = on_sublanes(top1[:SUBLANES]), on_sublanes(top1[SUBLANES:])
    t0hi = on_sublanes(top0[SUBLANES:])
    vals = [top0[0] + t1lo, top0[0] + t1hi]
    poss = [subf, subf + SUBLANES]
    for a in range(1, SUBLANES):
        vals.append(jnp.where(sub < nt // (a + 1), top0[a] + t1lo, NEG_INF))
        poss.append(subf + a * nt)
    vals.append(t0hi + top1[0])
    poss.append((subf + SUBLANES) * nt)
    best, bpos, _ = _take_ranked(jnp.stack(vals), jnp.stack(poss), nt)

    zsum = jnp.zeros((SUBLANES, ln), F32)
    cnt_lo = jnp.zeros((SUBLANES, ln), F32)
    cnt_hi = jnp.zeros((SUBLANES, ln), F32)
    for k in range(nt):
        zsum = zsum + jnp.exp(best[k] - best[0])
        a_k = jnp.floor(bpos[k] * (1.0 / nt))
        cnt_lo = cnt_lo + jnp.where(subf == a_k, 1.0, 0.0)
        cnt_hi = cnt_hi + jnp.where(subf + SUBLANES == a_k, 1.0, 0.0)

    th = jnp.full((nv, SUBLANES, ln), POS_INF, F32)
    for a in range(nt):
        cnt = cnt_lo if a < SUBLANES else cnt_hi
        n_a = jnp.broadcast_to(cnt[a % SUBLANES:a % SUBLANES + 1], (SUBLANES, ln))
        th = jnp.where(rank0 == float(a), (1.0 - n_a)[None], th)
    c = jnp.exp(s0 - top0[0][None]) / zsum[None]
    r1 = jnp.where(rank1 < float(nt), -rank1, NEG_INF)
    e1 = jnp.exp(s1 - top1[0][None])
    return tuple(x.reshape(nk, ln) for x in (th, c, r1, e1))


def _route_kernel(hn_ref, wq_ref, sk_ref, th_ref, c_ref, r1_ref, e1_ref, q_sc):
    q = jnp.dot(hn_ref[...], wq_ref[...], preferred_element_type=F32)
    tm = hn_ref.shape[0]
    d_half = sk_ref.shape[2]
    for hp in range(2 * PEER_HEADS):
        q_sc[hp] = q[:, hp * d_half:(hp + 1) * d_half]

    def head(h, _):
        def scores(half):
            return lax.dot_general(sk_ref[2 * h + half], q_sc[2 * h + half],
                                   (((1,), (1,)), ((), ())),
                                   preferred_element_type=F32)

        s0, s1 = scores(0), scores(1)
        for lc in range(tm // LANES):
            ls = slice(lc * LANES, (lc + 1) * LANES)
            th, c, r1, e1 = _route_lanes(s0[:, ls], s1[:, ls])
            th_ref[h, :, ls] = th
            c_ref[h, :, ls] = c
            r1_ref[h, :, ls] = r1
            e1_ref[h, :, ls] = e1
        return 0

    lax.fori_loop(0, PEER_HEADS, head, 0)


def _route(hn, w_query, sub_keys, *, tm):
    t, d = hn.shape
    dq = w_query.shape[1]
    slab = jax.ShapeDtypeStruct((PEER_HEADS, N_KEYS, t), F32)
    slab_spec = pl.BlockSpec((PEER_HEADS, N_KEYS, tm), lambda i: (0, 0, i))
    return pl.pallas_call(
        _route_kernel,
        grid=(t // tm,),
        in_specs=[
            pl.BlockSpec((tm, d), lambda i: (i, 0)),
            pl.BlockSpec((d, dq), lambda i: (0, 0)),
            pl.BlockSpec(sub_keys.shape, lambda i: (0, 0, 0)),
        ],
        out_specs=[slab_spec] * 4,
        out_shape=[slab] * 4,
        scratch_shapes=[pltpu.VMEM((2 * PEER_HEADS, tm, sub_keys.shape[2]), F32)],
        compiler_params=_params("parallel"),
        name="peer_route",
    )(hn, w_query, sub_keys)


PEER_ROW_CHUNK = 32


def _peer_kernel(x_ref, u_ref, vt_ref, th_ref, c_ref, r1_ref, e1_ref, y_ref, act_sc, gate_sc,
                 coef_sc):
    e = pl.program_id(1)
    tm = x_ref.shape[1]
    ib = th_ref.shape[1]
    hk = ib // 2
    hr = hk * N_KEYS
    assert tm % LANES == 0 and N_KEYS % PEER_ROW_CHUNK == 0 and ib % 2 == 0

    @pl.when(e == 0)
    def _():
        y_ref[...] = jnp.zeros_like(y_ref)

    def first_matmul(half):
        rows = slice(half * hr, (half + 1) * hr)
        act_sc[rows, :] = jnp.dot(u_ref[rows, :], x_ref[...], preferred_element_type=F32)

    def routing_weights(half):
        for lc in range(tm // LANES):
            ls = slice(lc * LANES, (lc + 1) * LANES)
            for jc in range(N_KEYS // PEER_ROW_CHUNK):
                js = slice(jc * PEER_ROW_CHUNK, (jc + 1) * PEER_ROW_CHUNK)
                gates = [jnp.zeros((PEER_ROW_CHUNK, LANES), F32) for _ in range(hk)]
                for h in range(PEER_HEADS):
                    r1 = r1_ref[h, js, ls]
                    e1 = e1_ref[h, js, ls]
                    for k in range(hk):
                        ii = half * hk + k
                        th = th_ref[h, ii:ii + 1, ls]
                        ch = c_ref[h, ii:ii + 1, ls]
                        gates[k] = gates[k] + jnp.where(r1 >= th, e1 * ch, 0.0)
                for k in range(hk):
                    r0 = (half * hk + k) * N_KEYS + jc * PEER_ROW_CHUNK
                    gate_sc[r0:r0 + PEER_ROW_CHUNK, ls] = gates[k]

    def second_matmul(half):
        rows = slice(half * hr, (half + 1) * hr)
        coef_sc[rows, :] = (_gelu(act_sc[rows, :]) * gate_sc[rows, :]).astype(coef_sc.dtype)
        y_ref[...] += jnp.dot(vt_ref[:, rows], coef_sc[rows, :], preferred_element_type=F32)

    @pl.when(e >= 0)
    def _():
        first_matmul(0)
        routing_weights(0)

    @pl.when(e >= -1)
    def _():
        first_matmul(1)
        routing_weights(1)
        second_matmul(0)

    @pl.when(e >= -2)
    def _():
        second_matmul(1)


MXU_TILE = 256
PEER_CHAIN_ROWS = 32
PEER_CHAIN_KEYS = 2


def _zero_after(x):
    bits = pltpu.bitcast(x, jnp.uint32)
    z = lax.shift_right_logical(lax.shift_right_logical(bits, jnp.uint32(16)), jnp.uint32(16))
    return z[0:1, :].astype(F32).astype(MXU_DTYPE)


def _peer_chain_kernel(x_ref, u_ref, vt_ref, th_ref, c_ref, r1_ref, e1_ref, y_ref, gate_sc):
    e = pl.program_id(1)
    d, tm = x_ref.shape
    ib = th_ref.shape[1]
    nkt = d // MXU_TILE
    nj = N_KEYS // PEER_CHAIN_ROWS
    nchunks = (tm // LANES) * nj
    assert nchunks % nkt == 0 and MXU_TILE % LANES == 0

    @pl.when(e == 0)
    def _():
        y_ref[...] = jnp.zeros_like(y_ref)

    te = u_ref.shape[0]
    pairs = ib // PEER_CHAIN_KEYS
    npieces = nchunks * pairs
    nrb = npieces // nkt
    rbs = te // nrb
    assert npieces % nkt == 0 and te % nrb == 0

    def gate_piece(pi):
        ci, kp = pi // pairs, pi % pairs
        ls = slice((ci // nj) * LANES, (ci // nj + 1) * LANES)
        js = slice((ci % nj) * PEER_CHAIN_ROWS, (ci % nj + 1) * PEER_CHAIN_ROWS)
        keys = range(kp * PEER_CHAIN_KEYS, (kp + 1) * PEER_CHAIN_KEYS)
        gates = {ii: jnp.zeros((PEER_CHAIN_ROWS, LANES), F32) for ii in keys}
        for h in range(PEER_HEADS):
            r1 = r1_ref[h, js, ls]
            e1 = e1_ref[h, js, ls]
            for ii in keys:
                th = th_ref[h, ii:ii + 1, ls]
                ch = c_ref[h, ii:ii + 1, ls]
                gates[ii] = gates[ii] + jnp.where(r1 >= th, e1 * ch, 0.0)
        total = None
        for ii in keys:
            r0 = ii * N_KEYS + (ci % nj) * PEER_CHAIN_ROWS
            gate_sc[r0:r0 + PEER_CHAIN_ROWS, ls] = gates[ii]
            total = gates[ii] if total is None else total + gates[ii]
        return jnp.sum(total.reshape(PEER_CHAIN_ROWS // SUBLANES, SUBLANES, LANES), axis=0)

    slices = []
    for kt in range(nkt):
        blocks = []
        for rb in range(nrb):
            zero = _zero_after(gate_piece(kt * nrb + rb))
            zero = jnp.concatenate([zero] * (MXU_TILE // LANES), axis=1)
            blocks.append(u_ref[rb * rbs:(rb + 1) * rbs, kt * MXU_TILE:(kt + 1) * MXU_TILE] + zero)
        slices.append(jnp.concatenate(blocks, axis=0))
    act = jnp.dot(jnp.concatenate(slices, axis=1), x_ref[...], preferred_element_type=F32)
    coef = (_gelu(act) * gate_sc[...]).astype(MXU_DTYPE)
    y_ref[...] += jnp.dot(vt_ref[...], coef, preferred_element_type=F32)


def _peer(hnt, u, vt, th, c, r1, e1, *, tm, ib):
    d, t = hnt.shape
    ne = u.shape[0]
    te = ib * N_KEYS
    slab_spec = pl.BlockSpec((PEER_HEADS, N_KEYS, tm), lambda i, e: (0, 0, i))
    first_spec = pl.BlockSpec((PEER_HEADS, ib, tm), lambda i, e: (0, e, i))
    if tm % (4 * LANES) == 0:
        return pl.pallas_call(
            _peer_chain_kernel,
            grid=(t // tm, ne // te),
            in_specs=[
                pl.BlockSpec((d, tm), lambda i, e: (0, i)),
                pl.BlockSpec((te, d), lambda i, e: (e, 0)),
                pl.BlockSpec((d, te), lambda i, e: (0, e)),
                first_spec, first_spec, slab_spec, slab_spec,
            ],
            out_specs=pl.BlockSpec((d, tm), lambda i, e: (0, i)),
            out_shape=jax.ShapeDtypeStruct((d, t), F32),
            scratch_shapes=[pltpu.VMEM((te, tm), F32)],
            compiler_params=_params("parallel", "arbitrary"),
            name="peer_experts_chain",
        )(hnt, u, vt, th, c, r1, e1)
    return pl.pallas_call(
        _peer_kernel,
        grid=(t // tm, ne // te),
        in_specs=[
            pl.BlockSpec((d, tm), lambda i, e: (0, i)),
            pl.BlockSpec((te, d), lambda i, e: (e, 0)),
            pl.BlockSpec((d, te), lambda i, e: (0, e)),
            first_spec, first_spec, slab_spec, slab_spec,
        ],
        out_specs=pl.BlockSpec((d, tm), lambda i, e: (0, i)),
        out_shape=jax.ShapeDtypeStruct((d, t), F32),
        scratch_shapes=[pltpu.VMEM((te, tm), F32), pltpu.VMEM((te, tm), F32),
                        pltpu.VMEM((te, tm), MXU_DTYPE)],
        compiler_params=_params("parallel", "arbitrary"),
        name="peer_experts",
    )(hnt, u, vt, th, c, r1, e1)


def _final_kernel(h_ref, yt_ref, gain_ref, out_ref):
    out_ref[...] = _rms(h_ref[...] + yt_ref[...].T, gain_ref[...])


def _final(h, yt, gain, *, tm):
    t, d = h.shape
    return pl.pallas_call(
        _final_kernel,
        grid=(t // tm,),
        in_specs=[pl.BlockSpec((tm, d), lambda i: (i, 0)), pl.BlockSpec((d, tm), lambda i: (0, i)),
                  pl.BlockSpec((1, d), lambda i: (0, 0))],
        out_specs=pl.BlockSpec((tm, d), lambda i: (i, 0)),
        out_shape=jax.ShapeDtypeStruct((t, d), F32),
        compiler_params=_params("parallel"),
        name="final_norm",
    )(h, yt, gain)


TOKEN_TILES = {"inproj": 1024, "rnn": 256, "merge": 512, "oproj": 256, "route": 256,
               "peer": 512, "final": 256}


def _tile(n, stage):
    return min(n, TOKEN_TILES[stage])


def _token_stages(x2, z, g, outs, lses, w, cols):
    t = x2.shape[0]
    mixed = _merge(g, outs, lses, z, w["w_rnn_out"], w["w_att_out"], col_ga=cols["ga"],
                   col_gb=cols["gb"], tm=_tile(t, "merge"), tn=HEADS * HEAD_DIM)
    h, hn, hnt = _oproj(x2, mixed, w["w_o"], w["norm_ffn"], tm=_tile(t, "oproj"))
    th, c, r1, e1 = _route(hn, w["w_query"], w["sub_keys"], tm=_tile(t, "route"))
    yt = _peer(hnt, w["expert_u"], w["expert_vt"], th, c, r1, e1, tm=_tile(t, "peer"),
               ib=SUBLANES)
    return _final(h, yt, w["norm_final"], tm=_tile(t, "final"))


def kernel(x_prompt, x_sample, state_conv, state_h, cache_kv_w128, cache_kv_w512, cache_kv_w2048,
           norm_mix, w_in, conv_w, conv_b, w_gate_x, b_gate_x, w_gate_a, b_gate_a, lru_lambda,
           w_rnn_out, w_att_out, w_o, norm_ffn, w_query, sub_keys, expert_u, expert_v, norm_final):
    depth = w_in.shape[0]
    assert depth == 1 and x_prompt.shape[0] == 1
    batch, seq, d = x_prompt.shape
    nb, steps, _ = x_sample.shape
    d_in = w_in.shape[2]
    gw = HEADS * HEAD_DIM
    d_qkv = N_GROUPS * gw
    assert d_in == 4 * d + 3 * d_qkv and d % gw == 0
    caches = (cache_kv_w128, cache_kv_w512, cache_kv_w2048)
    col_q, col_k, col_v = 2 * d // gw, (2 * d + d_qkv) // gw, (2 * d + 2 * d_qkv) // gw
    cols = {"ga": (2 * d + 3 * d_qkv) // gw, "gb": (3 * d + 3 * d_qkv) // gw}

    mx = lambda a: a.astype(MXU_DTYPE)
    row = lambda a: a.reshape(1, -1).astype(F32)
    w = {
        "w_in": mx(w_in[0]), "w_rnn_out": mx(w_rnn_out[0]), "w_att_out": mx(w_att_out[0]),
        "w_o": mx(w_o[0]), "w_query": mx(w_query[0]),
        "sub_keys": sub_keys[0].reshape(2 * PEER_HEADS, N_KEYS, -1),
        "expert_u": mx(expert_u[0]), "expert_vt": mx(expert_v[0]).T,
        "norm_ffn": row(norm_ffn[0]), "norm_final": row(norm_final),
    }
    rnn_w = (conv_w[0], row(conv_b[0]), mx(w_gate_x[0]), row(b_gate_x[0]), mx(w_gate_a[0]),
             row(b_gate_a[0]), row(lru_lambda[0]))
    gain_mix = row(norm_mix[0])

    xp = x_prompt.reshape(seq, d)
    zp = _inproj(xp, gain_mix, w["w_in"], tm=_tile(seq, "inproj"), tn=gw)
    g_p, past_p, h_p = _rnn_prompt(zp, *rnn_w, jnp.zeros((SUBLANES, d), F32),
                                   jnp.zeros((1, d), F32), d=d, tt=_tile(seq, "rnn"))
    outs, lses, kv_p = [], [], []
    for gi, (win, dil) in enumerate(ATT_GROUPS):
        o, lse = _attn_prompt(zp, gi, dil, col_q=col_q, col_k=col_k, col_v=col_v)
        outs.append(o)
        lses.append(lse)
        keep = min(win, seq)
        kcol = (col_k + gi) * gw
        vcol = (col_v + gi) * gw
        kv = jnp.stack([zp[seq - keep:, kcol:kcol + gw], zp[seq - keep:, vcol:vcol + gw]], axis=1)
        kv_p.append(kv.reshape(1, 1, keep, 2, HEADS, HEAD_DIM))
    y_p = _token_stages(xp, zp, g_p, outs, lses, w, cols).reshape(batch, seq, d)
    conv_p = past_p[SUBLANES - (CONV_W - 1):].reshape(1, 1, CONV_W - 1, d)
    h_p = h_p.reshape(1, 1, d)

    ts = nb * steps
    xs = x_sample.reshape(ts, d)
    zs = _inproj(xs, gain_mix, w["w_in"], tm=ts, tn=gw)
    g_s, conv_s, h_s = _rnn_sample(zs.reshape(nb, steps * d_in), *rnn_w,
                                   state_conv[0].reshape(nb, (CONV_W - 1) * d), state_h[0],
                                   d=d, steps=steps)
    g_s = g_s.reshape(ts, d)
    qkv = zs[:, 2 * d:2 * d + 3 * d_qkv].reshape(nb, steps, 3, N_GROUPS, HEADS, HEAD_DIM)
    outs, lses, kv_s = [], [], []
    for gi, (win, dil) in enumerate(ATT_GROUPS):
        cache = caches[gi][0]
        o, lse, new_cache = _attn_sample(qkv[:, :, 0, gi], qkv[:, :, 1, gi], qkv[:, :, 2, gi], cache,
                                         dil, rch=min(cache.shape[1], 512))
        outs.append(o.reshape(ts, gw))
        lse = lse[..., 0].reshape(ts, HEADS)
        lses.append(jnp.pad(lse, ((0, 0), (0, LANES - HEADS))))
        kv_s.append(new_cache[None])
    y_s = _token_stages(xs, zs, g_s, outs, lses, w, cols).reshape(nb, steps, d)
    conv_s = conv_s.reshape(1, nb, CONV_W - 1, d)
    h_s = h_s.reshape(1, nb, d)

    return (y_p, y_s, conv_p, h_p, kv_p[0], kv_p[1], kv_p[2],
            conv_s, h_s, kv_s[0], kv_s[1], kv_s[2])
```

```python
import functools
import math

import jax
import jax.numpy as jnp
from jax import lax
from jax.experimental import pallas as pl
from jax.experimental.pallas import tpu as pltpu

F32 = jnp.float32
MXU_DTYPE = jnp.bfloat16

RNN_BLOCKS = 8
CONV_W = 4
LRU_C = 8.0
ATT_GROUPS = ((128, 1), (512, 4), (2048, 16))
N_GROUPS = len(ATT_GROUPS)
HEADS = 8
HEAD_DIM = 128
BAND = 128
PEER_HEADS = 8
N_KEYS = 128
PEER_TOPK = 16
NORM_EPS = 1e-6

LANES = 128
SUBLANES = 8
VMEM_LIMIT_BYTES = 56 * 1024 * 1024

NEG_INF = float("-inf")
POS_INF = float("inf")


def _params(*sem):
    return pltpu.CompilerParams(dimension_semantics=sem, vmem_limit_bytes=VMEM_LIMIT_BYTES)


def _gelu(x):
    return 0.5 * x * (1.0 + lax.erf(x * math.sqrt(0.5)))


def _rms(x, gain):
    ms = jnp.mean(x * x, axis=-1, keepdims=True)
    return x * lax.rsqrt(ms + NORM_EPS) * gain


def _inproj_kernel(x_ref, g_ref, w_ref, z_ref, xn_sc):
    @pl.when(pl.program_id(1) == 0)
    def _():
        xn_sc[...] = _rms(x_ref[...], g_ref[...]).astype(MXU_DTYPE)

    z_ref[...] = jnp.dot(xn_sc[...], w_ref[...], preferred_element_type=F32)


def _inproj(x, gain, w, *, tm, tn):
    m, k = x.shape
    n = w.shape[1]
    return pl.pallas_call(
        _inproj_kernel,
        grid=(m // tm, n // tn),
        in_specs=[
            pl.BlockSpec((tm, k), lambda i, j: (i, 0)),
            pl.BlockSpec((1, k), lambda i, j: (0, 0)),
            pl.BlockSpec((k, tn), lambda i, j: (0, j)),
        ],
        out_specs=pl.BlockSpec((tm, tn), lambda i, j: (i, j)),
        out_shape=jax.ShapeDtypeStruct((m, n), F32),
        scratch_shapes=[pltpu.VMEM((tm, k), MXU_DTYPE)],
        compiler_params=_params("parallel", "arbitrary"),
        name="inproj",
    )(x, gain, w)


def _lru_gates(xc, wgx, bgx, wga, bga, log_sig):
    xb = xc.astype(MXU_DTYPE)
    gx = jax.nn.sigmoid(jnp.dot(xb, wgx, preferred_element_type=F32) + bgx)
    ga = jax.nn.sigmoid(jnp.dot(xb, wga, preferred_element_type=F32) + bga)
    a = jnp.exp(LRU_C * ga * log_sig)
    b = jnp.sqrt(1.0 - a * a) * gx * xc
    return a, b


def _log_sigmoid(x):
    return -(jnp.maximum(-x, 0.0) + jnp.log1p(jnp.exp(-jnp.abs(x))))


def _rnn_prompt_kernel(xr_ref, yr_ref, cw_ref, cb_ref, wgx_ref, bgx_ref, wga_ref, bga_ref, lam_ref,
                       p0_ref, h0_ref, g_ref, pn_ref, hn_ref, p_sc, h_sc, a_sc, b_sc):
    tt, d = xr_ref.shape
    bw = d // RNN_BLOCKS

    @pl.when(pl.program_id(0) == 0)
    def _():
        p_sc[...] = p0_ref[...]
        h_sc[...] = h0_ref[...]

    x = xr_ref[...]
    past = p_sc[...]
    row8 = lax.broadcasted_iota(jnp.int32, (SUBLANES, d), 0)
    xc = cb_ref[...] + cw_ref[0:1, :] * x
    for j in range(1, CONV_W):
        xs = pltpu.roll(x, j, 0)
        head = jnp.where(row8 < j, pltpu.roll(past, j, 0), xs[:SUBLANES])
        xs = jnp.concatenate([head, xs[SUBLANES:]], axis=0)
        xc = xc + cw_ref[j:j + 1, :] * xs
    p_sc[...] = x[tt - SUBLANES:]
    pn_ref[...] = x[tt - SUBLANES:]

    log_sig = _log_sigmoid(lam_ref[...])
    rowm = lax.broadcasted_iota(jnp.int32, (tt, bw), 0) & (SUBLANES - 1)
    for n in range(RNN_BLOCKS):
        sl = slice(n * bw, (n + 1) * bw)
        a, b = _lru_gates(xc[:, sl], wgx_ref[n], bgx_ref[:, sl], wga_ref[n], bga_ref[:, sl],
                          log_sig[:, sl])
        for sh in (1, 2, 4):
            ok = rowm >= sh
            a_sh = pltpu.roll(a, sh, 0)
            b_sh = pltpu.roll(b, sh, 0)
            b = jnp.where(ok, a * b_sh + b, b)
            a = jnp.where(ok, a * a_sh, a)
        a_sc[:, sl] = a
        b_sc[:, sl] = b

    def group(gi, h):
        rows = pl.ds(pl.multiple_of(gi * SUBLANES, SUBLANES), SUBLANES)
        hh = a_sc[rows, :] * h + b_sc[rows, :]
        b_sc[rows, :] = hh
        return hh[SUBLANES - 1:SUBLANES, :]

    h_last = lax.fori_loop(0, tt // SUBLANES, group, h_sc[...])
    h_sc[...] = h_last
    hn_ref[...] = h_last
    g_ref[...] = (_gelu(yr_ref[...]) * b_sc[...]).astype(g_ref.dtype)


def _rnn_prompt(z, conv_w, conv_b, wgx, bgx, wga, bga, lam, past8, h0, *, d, tt):
    t = z.shape[0]
    full2 = lambda i: (0, 0)
    full3 = lambda i: (0, 0, 0)
    bw = d // RNN_BLOCKS
    return pl.pallas_call(
        _rnn_prompt_kernel,
        grid=(t // tt,),
        in_specs=[
            pl.BlockSpec((tt, d), lambda i: (i, 0)),
            pl.BlockSpec((tt, d), lambda i: (i, 1)),
            pl.BlockSpec((CONV_W, d), full2),
            pl.BlockSpec((1, d), full2),
            pl.BlockSpec((RNN_BLOCKS, bw, bw), full3),
            pl.BlockSpec((1, d), full2),
            pl.BlockSpec((RNN_BLOCKS, bw, bw), full3),
            pl.BlockSpec((1, d), full2),
            pl.BlockSpec((1, d), full2),
            pl.BlockSpec((SUBLANES, d), full2),
            pl.BlockSpec((1, d), full2),
        ],
        out_specs=[
            pl.BlockSpec((tt, d), lambda i: (i, 0)),
            pl.BlockSpec((SUBLANES, d), full2),
            pl.BlockSpec((1, d), full2),
        ],
        out_shape=[
            jax.ShapeDtypeStruct((t, d), MXU_DTYPE),
            jax.ShapeDtypeStruct((SUBLANES, d), F32),
            jax.ShapeDtypeStruct((1, d), F32),
        ],
        scratch_shapes=[pltpu.VMEM((SUBLANES, d), F32), pltpu.VMEM((1, d), F32),
                        pltpu.VMEM((tt, d), F32), pltpu.VMEM((tt, d), F32)],
        compiler_params=_params("arbitrary"),
        name="rnn_prompt",
    )(z, z, conv_w, conv_b, wgx, bgx, wga, bga, lam, past8, h0)


def _rnn_sample_kernel(z_ref, cw_ref, cb_ref, wgx_ref, bgx_ref, wga_ref, bga_ref, lam_ref,
                       cs_ref, h0_ref, g_ref, cn_ref, hn_ref, *, d, d_in, steps):
    bw = d // RNN_BLOCKS
    log_sig = _log_sigmoid(lam_ref[...])
    hist = [cs_ref[:, j * d:(j + 1) * d] for j in range(CONV_W - 1)]
    h = [h0_ref[:, n * bw:(n + 1) * bw] for n in range(RNN_BLOCKS)]
    for s in range(steps):
        x = z_ref[:, s * d_in:s * d_in + d]
        hist.append(x)
        xc = cb_ref[...]
        for j in range(CONV_W):
            xc = xc + cw_ref[j:j + 1, :] * hist[-1 - j]
        for n in range(RNN_BLOCKS):
            sl = slice(n * bw, (n + 1) * bw)
            a, b = _lru_gates(xc[:, sl], wgx_ref[n], bgx_ref[:, sl], wga_ref[n], bga_ref[:, sl],
                              log_sig[:, sl])
            h[n] = a * h[n] + b
            y = z_ref[:, s * d_in + d + n * bw:s * d_in + d + (n + 1) * bw]
            g_ref[:, s * d + n * bw:s * d + (n + 1) * bw] = (_gelu(y) * h[n]).astype(g_ref.dtype)
    for j in range(CONV_W - 1):
        cn_ref[:, j * d:(j + 1) * d] = hist[len(hist) - (CONV_W - 1) + j]
    for n in range(RNN_BLOCKS):
        hn_ref[:, n * bw:(n + 1) * bw] = h[n]


def _rnn_sample(z2, conv_w, conv_b, wgx, bgx, wga, bga, lam, conv_state, h0, *, d, steps):
    nb = z2.shape[0]
    d_in = z2.shape[1] // steps
    return pl.pallas_call(
        functools.partial(_rnn_sample_kernel, d=d, d_in=d_in, steps=steps),
        out_shape=[
            jax.ShapeDtypeStruct((nb, steps * d), MXU_DTYPE),
            jax.ShapeDtypeStruct((nb, (CONV_W - 1) * d), F32),
            jax.ShapeDtypeStruct((nb, d), F32),
        ],
        compiler_params=pltpu.CompilerParams(vmem_limit_bytes=VMEM_LIMIT_BYTES),
        name="rnn_sample",
    )(z2, conv_w, conv_b, wgx, bgx, wga, bga, lam, conv_state, h0)


def _attn_prompt_kernel(q_ref, kc_ref, kp_ref, vc_ref, vp_ref, o_ref, lse_ref, *, dil, hpb, nbs):
    nblk = pl.program_id(0)
    hchunk = pl.program_id(1)
    span = BAND * dil
    qi = lax.broadcasted_iota(jnp.int32, (BAND, 2 * BAND), 0)
    ki = lax.broadcasted_iota(jnp.int32, (BAND, 2 * BAND), 1)
    window = (ki >= qi) & (ki <= qi + BAND)
    first_key = jnp.where(nblk > 0, 0, BAND)
    lane = lax.broadcasted_iota(jnp.int32, (BAND, LANES), 1)
    scale = HEAD_DIM ** -0.5

    @pl.when(hchunk == 0)
    def _():
        lse_ref[...] = jnp.zeros_like(lse_ref)

    def band_rows(b, r):
        return pl.ds(b * span + r, BAND, stride=dil) if dil > 1 else slice(b * span, (b + 1) * span)

    for b in range(nbs):
        mask = window & (ki >= first_key) if b == 0 else window
        for r in range(dil):
            rows = band_rows(b, r)
            if b == 0:
                k_prev, v_prev, prows = kp_ref, vp_ref, band_rows(0, r)
            else:
                k_prev, v_prev, prows = kc_ref, vc_ref, band_rows(b - 1, r)
            lse_tile = lse_ref[rows, :]
            for hh in range(hpb):
                sl = slice(hh * HEAD_DIM, (hh + 1) * HEAD_DIM)
                qh = (q_ref[rows, sl] * scale).astype(MXU_DTYPE)
                kk = jnp.concatenate([k_prev[prows, sl], kc_ref[rows, sl]], axis=0).astype(MXU_DTYPE)
                vv = jnp.concatenate([v_prev[prows, sl], vc_ref[rows, sl]], axis=0).astype(MXU_DTYPE)
                s = lax.dot_general(qh, kk, (((1,), (1,)), ((), ())), preferred_element_type=F32)
                s = jnp.where(mask, s, NEG_INF)
                m = jnp.max(s, axis=-1, keepdims=True)
                p = jnp.exp(s - m)
                den = jnp.sum(p, axis=-1, keepdims=True)
                o = jnp.dot(p.astype(MXU_DTYPE), vv, preferred_element_type=F32)
                o_ref[rows, sl] = o / den
                lse_tile = jnp.where(lane == hchunk * hpb + hh, m + jnp.log(den), lse_tile)
            lse_ref[rows, :] = lse_tile


ATTN_BANDS_PER_STEP = 4
ATTN_MAX_BLOCK_ROWS = 4096


def _attn_prompt(z, gi, dil, *, col_q, col_k, col_v):
    t = z.shape[0]
    gw = HEADS * HEAD_DIM
    hpb = HEADS if dil == 1 else 1
    cw = hpb * HEAD_DIM
    span = BAND * dil
    nbs = math.gcd(ATTN_BANDS_PER_STEP, t // span)
    while nbs > 1 and nbs * span > ATTN_MAX_BLOCK_ROWS:
        nbs //= 2
    rb = nbs * span
    per = gw // cw
    cur = lambda c: (lambda n, h: (n, (c + gi) * per + h))
    prev = lambda c: (lambda n, h: (jnp.maximum(n * nbs - 1, 0), (c + gi) * per + h))
    return pl.pallas_call(
        functools.partial(_attn_prompt_kernel, dil=dil, hpb=hpb, nbs=nbs),
        grid=(t // rb, HEADS // hpb),
        in_specs=[
            pl.BlockSpec((rb, cw), cur(col_q)),
            pl.BlockSpec((rb, cw), cur(col_k)),
            pl.BlockSpec((span, cw), prev(col_k)),
            pl.BlockSpec((rb, cw), cur(col_v)),
            pl.BlockSpec((span, cw), prev(col_v)),
        ],
        out_specs=[
            pl.BlockSpec((rb, cw), lambda n, h: (n, h)),
            pl.BlockSpec((rb, LANES), lambda n, h: (n, 0)),
        ],
        out_shape=[
            jax.ShapeDtypeStruct((t, gw), F32),
            jax.ShapeDtypeStruct((t, LANES), F32),
        ],
        compiler_params=_params("parallel", "arbitrary"),
        name=f"attn_prompt_g{gi}",
    )(z, z, z, z, z)


def _attn_sample_kernel(q_ref, kn_ref, vn_ref, c_ref, nx_ref, o_ref, lse_ref, new_ref,
                        m_sc, den_sc, acc_sc, *, dil, steps):
    ci = pl.program_id(1)
    nchunks = pl.num_programs(1)
    rch = c_ref.shape[1]

    def attend(s, krows, vrows, valid, first):
        q = q_ref[0, s] * (HEAD_DIM ** -0.5)
        sc = jnp.sum(q[None] * krows, axis=-1, keepdims=True)
        if valid is not None:
            sc = jnp.where(valid, sc, NEG_INF)
        m_blk = jnp.max(sc, axis=0)
        if first:
            m_new = m_blk
            p = jnp.exp(sc - m_new[None])
            den = jnp.sum(p, axis=0)
            acc = jnp.sum(p * vrows, axis=0)
        else:
            m_old = m_sc[s][:, 0:1]
            m_new = jnp.maximum(m_old, m_blk)
            alpha = jnp.exp(m_old - m_new)
            p = jnp.exp(sc - m_new[None])
            den = alpha * den_sc[s][:, 0:1] + jnp.sum(p, axis=0)
            acc = alpha * acc_sc[s] + jnp.sum(p * vrows, axis=0)
        m_sc[s] = jnp.broadcast_to(m_new, (HEADS, HEAD_DIM))
        den_sc[s] = jnp.broadcast_to(den, (HEADS, HEAD_DIM))
        acc_sc[s] = acc

    @pl.when(ci == 0)
    def _():
        for s in range(steps):
            lo = 0 if dil == 1 else s
            attend(s, kn_ref[0, lo:s + 1], vn_ref[0, lo:s + 1], None, True)

    for s in range(steps):
        if dil == 1:
            ridx = lax.broadcasted_iota(jnp.int32, (rch, 1, 1), 0) + ci * rch
            attend(s, c_ref[0, :, 0], c_ref[0, :, 1], ridx >= s, False)
        else:
            n = rch // dil
            attend(s, c_ref[0, pl.ds(s, n, stride=dil), 0], c_ref[0, pl.ds(s, n, stride=dil), 1],
                   None, False)

    @pl.when(ci == nchunks - 1)
    def _():
        for s in range(steps):
            o_ref[0, s] = acc_sc[s] / den_sc[s]
            lse_ref[0, s] = m_sc[s] + jnp.log(den_sc[s])

    new_ref[0, 0:rch - steps] = c_ref[0, steps:rch]

    @pl.when(ci < nchunks - 1)
    def _():
        new_ref[0, rch - steps:rch] = nx_ref[0]

    @pl.when(ci == nchunks - 1)
    def _():
        new_ref[0, rch - steps:rch, 0] = kn_ref[0]
        new_ref[0, rch - steps:rch, 1] = vn_ref[0]


def _attn_sample(q, kn, vn, cache, dil, *, rch):
    nb, steps = q.shape[0], q.shape[1]
    lg = cache.shape[1]
    nchunks = lg // rch
    per = rch // steps
    last = lg // steps - 1
    small = pl.BlockSpec((1, steps, HEADS, HEAD_DIM), lambda b, c: (b, 0, 0, 0))
    big = pl.BlockSpec((1, rch, 2, HEADS, HEAD_DIM), lambda b, c: (b, c, 0, 0, 0))
    nxt = pl.BlockSpec((1, steps, 2, HEADS, HEAD_DIM),
                       lambda b, c: (b, jnp.minimum((c + 1) * per, last), 0, 0, 0))
    return pl.pallas_call(
        functools.partial(_attn_sample_kernel, dil=dil, steps=steps),
        grid=(nb, nchunks),
        in_specs=[small, small, small, big, nxt],
        out_specs=[small, small, big],
        out_shape=[
            jax.ShapeDtypeStruct(q.shape, F32),
            jax.ShapeDtypeStruct(q.shape, F32),
            jax.ShapeDtypeStruct(cache.shape, cache.dtype),
        ],
        scratch_shapes=[pltpu.VMEM((steps, HEADS, HEAD_DIM), F32)] * 3,
        compiler_params=_params("parallel", "arbitrary"),
        name=f"attn_sample_d{dil}",
    )(q, kn, vn, cache, cache)


def _merge_kernel(g_ref, o0_ref, o1_ref, o2_ref, l0_ref, l1_ref, l2_ref, ga_ref, gb_ref,
                  wr_ref, wa_ref, out_ref, att_sc):
    @pl.when(pl.program_id(1) == 0)
    def _():
        l0, l1, l2 = l0_ref[...], l1_ref[...], l2_ref[...]
        mx = jnp.maximum(jnp.maximum(l0, l1), l2)
        e0, e1, e2 = jnp.exp(l0 - mx), jnp.exp(l1 - mx), jnp.exp(l2 - mx)
        inv = 1.0 / (e0 + e1 + e2)
        w0, w1, w2 = e0 * inv, e1 * inv, e2 * inv
        for h in range(HEADS):
            sl = slice(h * HEAD_DIM, (h + 1) * HEAD_DIM)
            att = (w0[:, h:h + 1] * o0_ref[:, sl] + w1[:, h:h + 1] * o1_ref[:, sl]
                   + w2[:, h:h + 1] * o2_ref[:, sl])
            att_sc[:, sl] = att.astype(att_sc.dtype)

    rnn_out = jnp.dot(g_ref[...], wr_ref[...], preferred_element_type=F32)
    att_out = jnp.dot(att_sc[...], wa_ref[...], preferred_element_type=F32)
    mixed = jax.nn.sigmoid(ga_ref[...]) * rnn_out + jax.nn.sigmoid(gb_ref[...]) * att_out
    out_ref[...] = mixed.astype(out_ref.dtype)


def _merge(g, outs, lses, z, w_rnn, w_att, *, col_ga, col_gb, tm, tn):
    t, d = g.shape
    gw = HEADS * HEAD_DIM
    nj = d // tn
    row = lambda i, j: (i, 0)
    return pl.pallas_call(
        _merge_kernel,
        grid=(t // tm, nj),
        in_specs=[
            pl.BlockSpec((tm, d), row),
            pl.BlockSpec((tm, gw), row), pl.BlockSpec((tm, gw), row), pl.BlockSpec((tm, gw), row),
            pl.BlockSpec((tm, LANES), row), pl.BlockSpec((tm, LANES), row),
            pl.BlockSpec((tm, LANES), row),
            pl.BlockSpec((tm, tn), lambda i, j: (i, col_ga + j)),
            pl.BlockSpec((tm, tn), lambda i, j: (i, col_gb + j)),
            pl.BlockSpec((d, tn), lambda i, j: (0, j)),
            pl.BlockSpec((gw, tn), lambda i, j: (0, j)),
        ],
        out_specs=pl.BlockSpec((tm, tn), lambda i, j: (i, j)),
        out_shape=jax.ShapeDtypeStruct((t, d), MXU_DTYPE),
        scratch_shapes=[pltpu.VMEM((tm, gw), MXU_DTYPE)],
        compiler_params=_params("parallel", "arbitrary"),
        name="merge",
    )(g, *outs, *lses, z, z, w_rnn, w_att)


def _oproj_kernel(x_ref, mx_ref, wo_ref, gain_ref, h_ref, hn_ref, hnt_ref):
    h = x_ref[...] + jnp.dot(mx_ref[...], wo_ref[...], preferred_element_type=F32)
    h_ref[...] = h
    hn = _rms(h, gain_ref[...])
    hn_ref[...] = hn.astype(hn_ref.dtype)
    hnt_ref[...] = hn.T.astype(hnt_ref.dtype)


def _oproj(x, mixed, w_o, gain, *, tm):
    t, d = x.shape
    row = lambda i: (i, 0)
    return pl.pallas_call(
        _oproj_kernel,
        grid=(t // tm,),
        in_specs=[
            pl.BlockSpec((tm, d), row),
            pl.BlockSpec((tm, d), row),
            pl.BlockSpec((d, d), lambda i: (0, 0)),
            pl.BlockSpec((1, d), lambda i: (0, 0)),
        ],
        out_specs=[pl.BlockSpec((tm, d), row), pl.BlockSpec((tm, d), row),
                   pl.BlockSpec((d, tm), lambda i: (0, i))],
        out_shape=[jax.ShapeDtypeStruct((t, d), F32), jax.ShapeDtypeStruct((t, d), MXU_DTYPE),
                   jax.ShapeDtypeStruct((d, t), MXU_DTYPE)],
        compiler_params=_params("parallel"),
        name="oproj",
    )(x, mixed, w_o, gain)


def _all_sublanes(x, op):
    for sh in (4, 2, 1):
        x = op(x, pltpu.roll(x, sh, 0))
    return x


def _take_ranked(vals, order, count):
    rest = vals
    rank = jnp.full(vals.shape, float(count), F32)
    tops, firsts = [], []
    for k in range(count):
        m = _all_sublanes(jnp.max(rest, axis=0), jnp.maximum)
        first = _all_sublanes(jnp.min(jnp.where(rest == m[None], order, POS_INF), axis=0),
                              jnp.minimum)
        hit = order == first[None]
        rest = jnp.where(hit, NEG_INF, rest)
        rank = jnp.where(hit, float(k), rank)
        tops.append(m)
        firsts.append(first)
    return tops, firsts, rank


def _sort_pairs(n):
    pairs, p = [], 1
    while p < n:
        k = p
        while k >= 1:
            for j in range(k % p, n - k, 2 * k):
                for i in range(min(k, n - j - k)):
                    if (i + j) // (2 * p) == (i + j + k) // (2 * p):
                        pairs.append((i + j, i + j + k))
            k //= 2
        p *= 2
    return tuple(pairs)


def _bitonic_pairs(n):
    pairs, dist = [], n // 2
    while dist >= 1:
        pairs += [(i, i + dist) for i in range(n) if (i // dist) % 2 == 0]
        dist //= 2
    return tuple(pairs)


_SORT_TOPK = _sort_pairs(PEER_TOPK)
_MERGE_TOPK = _bitonic_pairs(PEER_TOPK)


def _top_values(vals):
    nt = PEER_TOPK
    v = [vals[k] for k in range(nt)]

    def exchange(pairs):
        for a, b in pairs:
            v[a], v[b] = jnp.maximum(v[a], v[b]), jnp.minimum(v[a], v[b])

    exchange(_SORT_TOPK)
    for sh in (4, 2, 1):
        other = [pltpu.roll(x, sh, 0) for x in v]
        v = [jnp.maximum(v[k], other[nt - 1 - k]) for k in range(nt)]
        exchange(_MERGE_TOPK)
    return v


def _route_lanes_by_value(s0, s1):
    nt = PEER_TOPK
    nk, ln = s0.shape
    nv = nk // SUBLANES
    assert nv == nt == 2 * SUBLANES
    s0 = s0.reshape(nv, SUBLANES, ln)
    s1 = s1.reshape(nv, SUBLANES, ln)
    top0 = _top_values(s0)
    top1 = _top_values(s1)
    sub = lax.broadcasted_iota(jnp.int32, (SUBLANES, ln), 0)

    def on_sublanes(rows):
        out = rows[0]
        for b in range(1, SUBLANES):
            out = jnp.where(sub == b, rows[b], out)
        return out

    def count(mask):
        return _all_sublanes(jnp.where(mask, 1.0, 0.0), jnp.add)

    t1lo, t1hi = on_sublanes(top1[:SUBLANES]), on_sublanes(top1[SUBLANES:])
    t0hi = on_sublanes(top0[SUBLANES:])
    pieces = [top0[0] + t1lo, top0[0] + t1hi]
    for a in range(1, SUBLANES):
        pieces.append(jnp.where(sub < nt // (a + 1), top0[a] + t1lo, NEG_INF))
    pieces.append(t0hi + top1[0])
    pad = jnp.full((SUBLANES, ln), NEG_INF, F32)
    best = _top_values(jnp.stack(pieces + [pad] * (nt - len(pieces))))
    cut = best[nt - 1]
    zsum = jnp.ones((SUBLANES, ln), F32)
    for k in range(1, nt):
        zsum = zsum + jnp.exp(best[k] - best[0])

    counts = [count(pieces[0] >= cut) + count(pieces[1] >= cut)]
    counts += [count(pieces[a + 1] >= cut) for a in range(1, SUBLANES)]
    total = count(pieces[SUBLANES + 1] >= cut)
    th_vals = []
    for a in range(SUBLANES):
        total = total + counts[a]
        tv = jnp.full((SUBLANES, ln), POS_INF, F32)
        for m in range(nt // (a + 1)):
            tv = jnp.where(counts[a] == float(m + 1), top1[m], tv)
        th_vals.append(tv)

    member0 = s0 >= top0[nt - 1][None]
    member1 = s1 >= top1[nt - 1][None]
    th = jnp.where(member0 & (s0 + top1[0][None] >= cut[None]), top1[0][None], POS_INF)
    for a in range(SUBLANES):
        th = jnp.where(s0 == top0[a][None], th_vals[a][None], th)
    c = jnp.exp(s0 - top0[0][None]) / zsum[None]
    r1 = jnp.where(member1, s1, NEG_INF)
    e1 = jnp.exp(s1 - top1[0][None])

    gap = top0[0] - top0[1]
    for t in (top0, top1):
        for k in range(nt - 1):
            gap = jnp.minimum(gap, t[k] - t[k + 1])
    n0 = _all_sublanes(jnp.sum(jnp.where(member0, 1.0, 0.0), axis=0), jnp.add)
    n1 = _all_sublanes(jnp.sum(jnp.where(member1, 1.0, 0.0), axis=0), jnp.add)
    tie = (jnp.where(gap <= 0.0, 1.0, 0.0) + jnp.abs(n0 - float(nt)) + jnp.abs(n1 - float(nt))
           + jnp.abs(total - float(nt)))
    return tuple(x.reshape(nk, ln) for x in (th, c, r1, e1)) + (tie,)


def _route_lanes(s0, s1):
    th, c, r1, e1, tie = _route_lanes_by_value(s0, s1)
    return lax.cond(jnp.max(tie) > 0.0, lambda: _route_lanes_by_rank(s0, s1),
                    lambda: (th, c, r1, e1))


def _route_lanes_by_rank(s0, s1):
    nt = PEER_TOPK
    nk, ln = s0.shape
    nv = nk // SUBLANES
    s0 = s0.reshape(nv, SUBLANES, ln)
    s1 = s1.reshape(nv, SUBLANES, ln)
    key = (lax.broadcasted_iota(jnp.int32, (nv, SUBLANES, ln), 0) * SUBLANES
           + lax.broadcasted_iota(jnp.int32, (nv, SUBLANES, ln), 1)).astype(F32)
    top0, _, rank0 = _take_ranked(s0, key, nt)
    top1, _, rank1 = _take_ranked(s1, key, nt)

    sub = lax.broadcasted_iota(jnp.int32, (SUBLANES, ln), 0)
    subf = sub.astype(F32)

    def on_sublanes(rows):
        out = rows[0]
        for b in range(1, SUBLANES):
            out = jnp.where(sub == b, rows[b], out)
        return out

    assert nt == 2 * SUBLANES
    t1lo, t1hi = on_sublanes(top1[:SUBLANES]), on_sublanes(top1[SUBLANES:])
    t0hi = on_sublanes(top0[SUBLANES:])
    vals = [top0[0] + t1lo, top0[0] + t1hi]
    poss = [subf, subf + SUBLANES]
    for a in range(1, SUBLANES):
        vals.append(jnp.where(sub < nt // (a + 1), top0[a] + t1lo, NEG_INF))
        poss.append(subf + a * nt)
    vals.append(t0hi + top1[0])
    poss.append((subf + SUBLANES) * nt)
    best, bpos, _ = _take_ranked(jnp.stack(vals), jnp.stack(poss), nt)

    zsum = jnp.zeros((SUBLANES, ln), F32)
    cnt_lo = jnp.zeros((SUBLANES, ln), F32)
    cnt_hi = jnp.zeros((SUBLANES, ln), F32)
    for k in range(nt):
        zsum = zsum + jnp.exp(best[k] - best[0])
        a_k = jnp.floor(bpos[k] * (1.0 / nt))
        cnt_lo = cnt_lo + jnp.where(subf == a_k, 1.0, 0.0)
        cnt_hi = cnt_hi + jnp.where(subf + SUBLANES == a_k, 1.0, 0.0)

    th = jnp.full((nv, SUBLANES, ln), POS_INF, F32)
    for a in range(nt):
        cnt = cnt_lo if a < SUBLANES else cnt_hi
        n_a = jnp.broadcast_to(cnt[a % SUBLANES:a % SUBLANES + 1], (SUBLANES, ln))
        th = jnp.where(rank0 == float(a), (1.0 - n_a)[None], th)
    c = jnp.exp(s0 - top0[0][None]) / zsum[None]
    r1 = jnp.where(rank1 < float(nt), -rank1, NEG_INF)
    e1 = jnp.exp(s1 - top1[0][None])
    return tuple(x.reshape(nk, ln) for x in (th, c, r1, e1))


def _route_kernel(hn_ref, wq_ref, sk_ref, th_ref, c_ref, r1_ref, e1_ref, q_sc):
    q = jnp.dot(hn_ref[...], wq_ref[...], preferred_element_type=F32)
    tm = hn_ref.shape[0]
    d_half = sk_ref.shape[2]
    for hp in range(2 * PEER_HEADS):
        q_sc[hp] = q[:, hp * d_half:(hp + 1) * d_half]

    def head(h, _):
        def scores(half):
            return lax.dot_general(sk_ref[2 * h + half], q_sc[2 * h + half],
                                   (((1,), (1,)), ((), ())),
                                   preferred_element_type=F32)

        s0, s1 = scores(0), scores(1)
        for lc in range(tm // LANES):
            ls = slice(lc * LANES, (lc + 1) * LANES)
            th, c, r1, e1 = _route_lanes(s0[:, ls], s1[:, ls])
            th_ref[h, :, ls] = th
            c_ref[h, :, ls] = c
            r1_ref[h, :, ls] = r1
            e1_ref[h, :, ls] = e1
        return 0

    lax.fori_loop(0, PEER_HEADS, head, 0)


def _route(hn, w_query, sub_keys, *, tm):
    t, d = hn.shape
    dq = w_query.shape[1]
    slab = jax.ShapeDtypeStruct((PEER_HEADS, N_KEYS, t), F32)
    slab_spec = pl.BlockSpec((PEER_HEADS, N_KEYS, tm), lambda i: (0, 0, i))
    return pl.pallas_call(
        _route_kernel,
        grid=(t // tm,),
        in_specs=[
            pl.BlockSpec((tm, d), lambda i: (i, 0)),
            pl.BlockSpec((d, dq), lambda i: (0, 0)),
            pl.BlockSpec(sub_keys.shape, lambda i: (0, 0, 0)),
        ],
        out_specs=[slab_spec] * 4,
        out_shape=[slab] * 4,
        scratch_shapes=[pltpu.VMEM((2 * PEER_HEADS, tm, sub_keys.shape[2]), F32)],
        compiler_params=_params("parallel"),
        name="peer_route",
    )(hn, w_query, sub_keys)


PEER_ROW_CHUNK = 32


def _peer_kernel(x_ref, u_ref, vt_ref, th_ref, c_ref, r1_ref, e1_ref, y_ref, act_sc, gate_sc,
                 coef_sc):
    e = pl.program_id(1)
    tm = x_ref.shape[1]
    ib = th_ref.shape[1]
    hk = ib // 2
    hr = hk * N_KEYS
    assert tm % LANES == 0 and N_KEYS % PEER_ROW_CHUNK == 0 and ib % 2 == 0

    @pl.when(e == 0)
    def _():
        y_ref[...] = jnp.zeros_like(y_ref)

    def first_matmul(half):
        rows = slice(half * hr, (half + 1) * hr)
        act_sc[rows, :] = jnp.dot(u_ref[rows, :], x_ref[...], preferred_element_type=F32)

    def routing_weights(half):
        for lc in range(tm // LANES):
            ls = slice(lc * LANES, (lc + 1) * LANES)
            for jc in range(N_KEYS // PEER_ROW_CHUNK):
                js = slice(jc * PEER_ROW_CHUNK, (jc + 1) * PEER_ROW_CHUNK)
                gates = [jnp.zeros((PEER_ROW_CHUNK, LANES), F32) for _ in range(hk)]
                for h in range(PEER_HEADS):
                    r1 = r1_ref[h, js, ls]
                    e1 = e1_ref[h, js, ls]
                    for k in range(hk):
                        ii = half * hk + k
                        th = th_ref[h, ii:ii + 1, ls]
                        ch = c_ref[h, ii:ii + 1, ls]
                        gates[k] = gates[k] + jnp.where(r1 >= th, e1 * ch, 0.0)
                for k in range(hk):
                    r0 = (half * hk + k) * N_KEYS + jc * PEER_ROW_CHUNK
                    gate_sc[r0:r0 + PEER_ROW_CHUNK, ls] = gates[k]

    def second_matmul(half):
        rows = slice(half * hr, (half + 1) * hr)
        coef_sc[rows, :] = (_gelu(act_sc[rows, :]) * gate_sc[rows, :]).astype(coef_sc.dtype)
        y_ref[...] += jnp.dot(vt_ref[:, rows], coef_sc[rows, :], preferred_element_type=F32)

    @pl.when(e >= 0)
    def _():
        first_matmul(0)
        routing_weights(0)

    @pl.when(e >= -1)
    def _():
        first_matmul(1)
        routing_weights(1)
        second_matmul(0)

    @pl.when(e >= -2)
    def _():
        second_matmul(1)


MXU_TILE = 256
PEER_CHAIN_ROWS = 16
PEER_CHAIN_KEYS = 4
PEER_CHAIN_BLOCK_ROWS = 128


def _zero_after(x):
    bits = pltpu.bitcast(x, jnp.uint32)
    z = lax.shift_right_logical(lax.shift_right_logical(bits, jnp.uint32(16)), jnp.uint32(16))
    return z[0:1, :].astype(F32).astype(MXU_DTYPE)


def _chained(blocks, pieces):
    nb, npc = len(blocks), len(pieces)
    out = []
    for b in range(nb):
        dep = None
        for p in range(b * npc // nb, (b + 1) * npc // nb):
            v = pieces[p]()
            dep = v if dep is None else dep + v
        blk = blocks[b]()
        if dep is not None:
            zero = _zero_after(dep)
            blk = blk + jnp.concatenate([zero] * (blk.shape[1] // LANES), axis=1)
        out.append(blk)
    return out


def _peer_chain_kernel(x_ref, u_ref, vt_ref, th_ref, c_ref, r1_ref, e1_ref, y_ref, gate_sc):
    e = pl.program_id(1)
    d, tm = x_ref.shape
    te = u_ref.shape[0]
    ib = th_ref.shape[1]
    nkt = d // MXU_TILE
    nj = N_KEYS // PEER_CHAIN_ROWS
    nchunks = (tm // LANES) * nj
    assert ib % PEER_CHAIN_KEYS == 0 and te % PEER_CHAIN_BLOCK_ROWS == 0

    @pl.when(e == 0)
    def _():
        y_ref[...] = jnp.zeros_like(y_ref)

    def gate_piece(ci, keys):
        def run():
            ls = slice((ci // nj) * LANES, (ci // nj + 1) * LANES)
            js = slice((ci % nj) * PEER_CHAIN_ROWS, (ci % nj + 1) * PEER_CHAIN_ROWS)
            gates = {ii: jnp.zeros((PEER_CHAIN_ROWS, LANES), F32) for ii in keys}
            for h in range(PEER_HEADS):
                r1 = r1_ref[h, js, ls]
                e1 = e1_ref[h, js, ls]
                for ii in keys:
                    th = th_ref[h, ii:ii + 1, ls]
                    ch = c_ref[h, ii:ii + 1, ls]
                    gates[ii] = gates[ii] + jnp.where(r1 >= th, e1 * ch, 0.0)
            total = None
            for ii in keys:
                r0 = ii * N_KEYS + (ci % nj) * PEER_CHAIN_ROWS
                gate_sc[r0:r0 + PEER_CHAIN_ROWS, ls] = gates[ii]
                total = gates[ii] if total is None else total + gates[ii]
            return jnp.sum(total.reshape(PEER_CHAIN_ROWS // SUBLANES, SUBLANES, LANES), axis=0)
        return run

    pieces = [gate_piece(ci, tuple(range(k0, k0 + PEER_CHAIN_KEYS)))
              for ci in range(nchunks) for k0 in range(0, ib, PEER_CHAIN_KEYS)]

    rows_per = PEER_CHAIN_BLOCK_ROWS
    nrb = te // rows_per
    u_blocks = [(lambda kt=kt, rb=rb: u_ref[rb * rows_per:(rb + 1) * rows_per,
                                            kt * MXU_TILE:(kt + 1) * MXU_TILE])
                for kt in range(nkt) for rb in range(nrb)]
    blocks = _chained(u_blocks, pieces)
    lhs = jnp.concatenate([jnp.concatenate(blocks[kt * nrb:(kt + 1) * nrb], axis=0)
                           for kt in range(nkt)], axis=1)
    act = jnp.dot(lhs, x_ref[...], preferred_element_type=F32)
    coef = (_gelu(act) * gate_sc[...]).astype(MXU_DTYPE)
    y_ref[...] += jnp.dot(vt_ref[...], coef, preferred_element_type=F32)


def _peer(hnt, u, vt, th, c, r1, e1, *, tm, ib):
    d, t = hnt.shape
    ne = u.shape[0]
    te = ib * N_KEYS
    slab_spec = pl.BlockSpec((PEER_HEADS, N_KEYS, tm), lambda i, e: (0, 0, i))
    first_spec = pl.BlockSpec((PEER_HEADS, ib, tm), lambda i, e: (0, e, i))
    if tm % (4 * LANES) == 0:
        return pl.pallas_call(
            _peer_chain_kernel,
            grid=(t // tm, ne // te),
            in_specs=[
                pl.BlockSpec((d, tm), lambda i, e: (0, i)),
                pl.BlockSpec((te, d), lambda i, e: (e, 0)),
                pl.BlockSpec((d, te), lambda i, e: (0, e)),
                first_spec, first_spec, slab_spec, slab_spec,
            ],
            out_specs=pl.BlockSpec((d, tm), lambda i, e: (0, i)),
            out_shape=jax.ShapeDtypeStruct((d, t), F32),
            scratch_shapes=[pltpu.VMEM((te, tm), F32)],
            compiler_params=_params("parallel", "arbitrary"),
            name="peer_experts_chain",
        )(hnt, u, vt, th, c, r1, e1)
    return pl.pallas_call(
        _peer_kernel,
        grid=(t // tm, ne // te),
        in_specs=[
            pl.BlockSpec((d, tm), lambda i, e: (0, i)),
            pl.BlockSpec((te, d), lambda i, e: (e, 0)),
            pl.BlockSpec((d, te), lambda i, e: (0, e)),
            first_spec, first_spec, slab_spec, slab_spec,
        ],
        out_specs=pl.BlockSpec((d, tm), lambda i, e: (0, i)),
        out_shape=jax.ShapeDtypeStruct((d, t), F32),
        scratch_shapes=[pltpu.VMEM((te, tm), F32), pltpu.VMEM((te, tm), F32),
                        pltpu.VMEM((te, tm), MXU_DTYPE)],
        compiler_params=_params("parallel", "arbitrary"),
        name="peer_experts",
    )(hnt, u, vt, th, c, r1, e1)


def _final_kernel(h_ref, yt_ref, gain_ref, out_ref):
    out_ref[...] = _rms(h_ref[...] + yt_ref[...].T, gain_ref[...])


def _final(h, yt, gain, *, tm):
    t, d = h.shape
    return pl.pallas_call(
        _final_kernel,
        grid=(t // tm,),
        in_specs=[pl.BlockSpec((tm, d), lambda i: (i, 0)), pl.BlockSpec((d, tm), lambda i: (0, i)),
                  pl.BlockSpec((1, d), lambda i: (0, 0))],
        out_specs=pl.BlockSpec((tm, d), lambda i: (i, 0)),
        out_shape=jax.ShapeDtypeStruct((t, d), F32),
        compiler_params=_params("parallel"),
        name="final_norm",
    )(h, yt, gain)


TOKEN_TILES = {"inproj": 1024, "rnn": 256, "merge": 512, "oproj": 256, "route": 256,
               "peer": 512, "final": 256}


def _tile(n, stage):
    return min(n, TOKEN_TILES[stage])


def _token_stages(x2, z, g, outs, lses, w, cols):
    t = x2.shape[0]
    mixed = _merge(g, outs, lses, z, w["w_rnn_out"], w["w_att_out"], col_ga=cols["ga"],
                   col_gb=cols["gb"], tm=_tile(t, "merge"), tn=HEADS * HEAD_DIM)
    h, hn, hnt = _oproj(x2, mixed, w["w_o"], w["norm_ffn"], tm=_tile(t, "oproj"))
    th, c, r1, e1 = _route(hn, w["w_query"], w["sub_keys"], tm=_tile(t, "route"))
    yt = _peer(hnt, w["expert_u"], w["expert_vt"], th, c, r1, e1, tm=_tile(t, "peer"),
               ib=SUBLANES)
    return _final(h, yt, w["norm_final"], tm=_tile(t, "final"))


def kernel(x_prompt, x_sample, state_conv, state_h, cache_kv_w128, cache_kv_w512, cache_kv_w2048,
           norm_mix, w_in, conv_w, conv_b, w_gate_x, b_gate_x, w_gate_a, b_gate_a, lru_lambda,
           w_rnn_out, w_att_out, w_o, norm_ffn, w_query, sub_keys, expert_u, expert_v, norm_final):
    depth = w_in.shape[0]
    assert depth == 1 and x_prompt.shape[0] == 1
    batch, seq, d = x_prompt.shape
    nb, steps, _ = x_sample.shape
    d_in = w_in.shape[2]
    gw = HEADS * HEAD_DIM
    d_qkv = N_GROUPS * gw
    assert d_in == 4 * d + 3 * d_qkv and d % gw == 0
    caches = (cache_kv_w128, cache_kv_w512, cache_kv_w2048)
    col_q, col_k, col_v = 2 * d // gw, (2 * d + d_qkv) // gw, (2 * d + 2 * d_qkv) // gw
    cols = {"ga": (2 * d + 3 * d_qkv) // gw, "gb": (3 * d + 3 * d_qkv) // gw}

    mx = lambda a: a.astype(MXU_DTYPE)
    row = lambda a: a.reshape(1, -1).astype(F32)
    w = {
        "w_in": mx(w_in[0]), "w_rnn_out": mx(w_rnn_out[0]), "w_att_out": mx(w_att_out[0]),
        "w_o": mx(w_o[0]), "w_query": mx(w_query[0]),
        "sub_keys": sub_keys[0].reshape(2 * PEER_HEADS, N_KEYS, -1),
        "expert_u": mx(expert_u[0]), "expert_vt": mx(expert_v[0]).T,
        "norm_ffn": row(norm_ffn[0]), "norm_final": row(norm_final),
    }
    rnn_w = (conv_w[0], row(conv_b[0]), mx(w_gate_x[0]), row(b_gate_x[0]), mx(w_gate_a[0]),
             row(b_gate_a[0]), row(lru_lambda[0]))
    gain_mix = row(norm_mix[0])

    xp = x_prompt.reshape(seq, d)
    zp = _inproj(xp, gain_mix, w["w_in"], tm=_tile(seq, "inproj"), tn=gw)
    g_p, past_p, h_p = _rnn_prompt(zp, *rnn_w, jnp.zeros((SUBLANES, d), F32),
                                   jnp.zeros((1, d), F32), d=d, tt=_tile(seq, "rnn"))
    outs, lses, kv_p = [], [], []
    for gi, (win, dil) in enumerate(ATT_GROUPS):
        o, lse = _attn_prompt(zp, gi, dil, col_q=col_q, col_k=col_k, col_v=col_v)
        outs.append(o)
        lses.append(lse)
        keep = min(win, seq)
        kcol = (col_k + gi) * gw
        vcol = (col_v + gi) * gw
        kv = jnp.stack([zp[seq - keep:, kcol:kcol + gw], zp[seq - keep:, vcol:vcol + gw]], axis=1)
        kv_p.append(kv.reshape(1, 1, keep, 2, HEADS, HEAD_DIM))
    y_p = _token_stages(xp, zp, g_p, outs, lses, w, cols).reshape(batch, seq, d)
    conv_p = past_p[SUBLANES - (CONV_W - 1):].reshape(1, 1, CONV_W - 1, d)
    h_p = h_p.reshape(1, 1, d)

    ts = nb * steps
    xs = x_sample.reshape(ts, d)
    zs = _inproj(xs, gain_mix, w["w_in"], tm=ts, tn=gw)
    g_s, conv_s, h_s = _rnn_sample(zs.reshape(nb, steps * d_in), *rnn_w,
                                   state_conv[0].reshape(nb, (CONV_W - 1) * d), state_h[0],
                                   d=d, steps=steps)
    g_s = g_s.reshape(ts, d)
    qkv = zs[:, 2 * d:2 * d + 3 * d_qkv].reshape(nb, steps, 3, N_GROUPS, HEADS, HEAD_DIM)
    outs, lses, kv_s = [], [], []
    for gi, (win, dil) in enumerate(ATT_GROUPS):
        cache = caches[gi][0]
        o, lse, new_cache = _attn_sample(qkv[:, :, 0, gi], qkv[:, :, 1, gi], qkv[:, :, 2, gi], cache,
                                         dil, rch=min(cache.shape[1], 512))
        outs.append(o.reshape(ts, gw))
        lse = lse[..., 0].reshape(ts, HEADS)
        lses.append(jnp.pad(lse, ((0, 0), (0, LANES - HEADS))))
        kv_s.append(new_cache[None])
    y_s = _token_stages(xs, zs, g_s, outs, lses, w, cols).reshape(nb, steps, d)
    conv_s = conv_s.reshape(1, nb, CONV_W - 1, d)
    h_s = h_s.reshape(1, nb, d)

    return (y_p, y_s, conv_p, h_p, kv_p[0], kv_p[1], kv_p[2],
            conv_s, h_s, kv_s[0], kv_s[1], kv_s[2])
```

```python
import functools
import math

import jax
import jax.numpy as jnp
from jax import lax
from jax.experimental import pallas as pl
from jax.experimental.pallas import tpu as pltpu

F32 = jnp.float32
MXU_DTYPE = jnp.bfloat16

RNN_BLOCKS = 8
CONV_W = 4
LRU_C = 8.0
ATT_GROUPS = ((128, 1), (512, 4), (2048, 16))
N_GROUPS = len(ATT_GROUPS)
HEADS = 8
HEAD_DIM = 128
BAND = 128
PEER_HEADS = 8
N_KEYS = 128
PEER_TOPK = 16
NORM_EPS = 1e-6

LANES = 128
SUBLANES = 8
VMEM_LIMIT_BYTES = 56 * 1024 * 1024

NEG_INF = float("-inf")
POS_INF = float("inf")


def _params(*sem):
    return pltpu.CompilerParams(dimension_semantics=sem, vmem_limit_bytes=VMEM_LIMIT_BYTES)


def _gelu(x):
    return 0.5 * x * (1.0 + lax.erf(x * math.sqrt(0.5)))


def _rms(x, gain):
    ms = jnp.mean(x * x, axis=-1, keepdims=True)
    return x * lax.rsqrt(ms + NORM_EPS) * gain


def _inproj_kernel(x_ref, g_ref, w_ref, z_ref, xn_sc):
    @pl.when(pl.program_id(1) == 0)
    def _():
        xn_sc[...] = _rms(x_ref[...], g_ref[...]).astype(MXU_DTYPE)

    z_ref[...] = jnp.dot(xn_sc[...], w_ref[...], preferred_element_type=F32)


def _inproj(x, gain, w, *, tm, tn):
    m, k = x.shape
    n = w.shape[1]
    return pl.pallas_call(
        _inproj_kernel,
        grid=(m // tm, n // tn),
        in_specs=[
            pl.BlockSpec((tm, k), lambda i, j: (i, 0)),
            pl.BlockSpec((1, k), lambda i, j: (0, 0)),
            pl.BlockSpec((k, tn), lambda i, j: (0, j)),
        ],
        out_specs=pl.BlockSpec((tm, tn), lambda i, j: (i, j)),
        out_shape=jax.ShapeDtypeStruct((m, n), F32),
        scratch_shapes=[pltpu.VMEM((tm, k), MXU_DTYPE)],
        compiler_params=_params("parallel", "arbitrary"),
        name="inproj",
    )(x, gain, w)


def _lru_gates(xc, wgx, bgx, wga, bga, log_sig):
    xb = xc.astype(MXU_DTYPE)
    gx = jax.nn.sigmoid(jnp.dot(xb, wgx, preferred_element_type=F32) + bgx)
    ga = jax.nn.sigmoid(jnp.dot(xb, wga, preferred_element_type=F32) + bga)
    a = jnp.exp(LRU_C * ga * log_sig)
    b = jnp.sqrt(1.0 - a * a) * gx * xc
    return a, b


def _log_sigmoid(x):
    return -(jnp.maximum(-x, 0.0) + jnp.log1p(jnp.exp(-jnp.abs(x))))


def _rnn_prompt_kernel(xr_ref, yr_ref, cw_ref, cb_ref, wgx_ref, bgx_ref, wga_ref, bga_ref, lam_ref,
                       p0_ref, h0_ref, g_ref, pn_ref, hn_ref, p_sc, h_sc, a_sc, b_sc):
    tt, d = xr_ref.shape
    bw = d // RNN_BLOCKS

    @pl.when(pl.program_id(0) == 0)
    def _():
        p_sc[...] = p0_ref[...]
        h_sc[...] = h0_ref[...]

    x = xr_ref[...]
    past = p_sc[...]
    row8 = lax.broadcasted_iota(jnp.int32, (SUBLANES, d), 0)
    xc = cb_ref[...] + cw_ref[0:1, :] * x
    for j in range(1, CONV_W):
        xs = pltpu.roll(x, j, 0)
        head = jnp.where(row8 < j, pltpu.roll(past, j, 0), xs[:SUBLANES])
        xs = jnp.concatenate([head, xs[SUBLANES:]], axis=0)
        xc = xc + cw_ref[j:j + 1, :] * xs
    p_sc[...] = x[tt - SUBLANES:]
    pn_ref[...] = x[tt - SUBLANES:]

    log_sig = _log_sigmoid(lam_ref[...])
    rowm = lax.broadcasted_iota(jnp.int32, (tt, bw), 0) & (SUBLANES - 1)
    for n in range(RNN_BLOCKS):
        sl = slice(n * bw, (n + 1) * bw)
        a, b = _lru_gates(xc[:, sl], wgx_ref[n], bgx_ref[:, sl], wga_ref[n], bga_ref[:, sl],
                          log_sig[:, sl])
        for sh in (1, 2, 4):
            ok = rowm >= sh
            a_sh = pltpu.roll(a, sh, 0)
            b_sh = pltpu.roll(b, sh, 0)
            b = jnp.where(ok, a * b_sh + b, b)
            a = jnp.where(ok, a * a_sh, a)
        a_sc[:, sl] = a
        b_sc[:, sl] = b

    def group(gi, h):
        rows = pl.ds(pl.multiple_of(gi * SUBLANES, SUBLANES), SUBLANES)
        hh = a_sc[rows, :] * h + b_sc[rows, :]
        b_sc[rows, :] = hh
        return hh[SUBLANES - 1:SUBLANES, :]

    h_last = lax.fori_loop(0, tt // SUBLANES, group, h_sc[...])
    h_sc[...] = h_last
    hn_ref[...] = h_last
    g_ref[...] = (_gelu(yr_ref[...]) * b_sc[...]).astype(g_ref.dtype)


def _rnn_prompt(z, conv_w, conv_b, wgx, bgx, wga, bga, lam, past8, h0, *, d, tt):
    t = z.shape[0]
    full2 = lambda i: (0, 0)
    full3 = lambda i: (0, 0, 0)
    bw = d // RNN_BLOCKS
    return pl.pallas_call(
        _rnn_prompt_kernel,
        grid=(t // tt,),
        in_specs=[
            pl.BlockSpec((tt, d), lambda i: (i, 0)),
            pl.BlockSpec((tt, d), lambda i: (i, 1)),
            pl.BlockSpec((CONV_W, d), full2),
            pl.BlockSpec((1, d), full2),
            pl.BlockSpec((RNN_BLOCKS, bw, bw), full3),
            pl.BlockSpec((1, d), full2),
            pl.BlockSpec((RNN_BLOCKS, bw, bw), full3),
            pl.BlockSpec((1, d), full2),
            pl.BlockSpec((1, d), full2),
            pl.BlockSpec((SUBLANES, d), full2),
            pl.BlockSpec((1, d), full2),
        ],
        out_specs=[
            pl.BlockSpec((tt, d), lambda i: (i, 0)),
            pl.BlockSpec((SUBLANES, d), full2),
            pl.BlockSpec((1, d), full2),
        ],
        out_shape=[
            jax.ShapeDtypeStruct((t, d), MXU_DTYPE),
            jax.ShapeDtypeStruct((SUBLANES, d), F32),
            jax.ShapeDtypeStruct((1, d), F32),
        ],
        scratch_shapes=[pltpu.VMEM((SUBLANES, d), F32), pltpu.VMEM((1, d), F32),
                        pltpu.VMEM((tt, d), F32), pltpu.VMEM((tt, d), F32)],
        compiler_params=_params("arbitrary"),
        name="rnn_prompt",
    )(z, z, conv_w, conv_b, wgx, bgx, wga, bga, lam, past8, h0)


def _rnn_sample_kernel(z_ref, cw_ref, cb_ref, wgx_ref, bgx_ref, wga_ref, bga_ref, lam_ref,
                       cs_ref, h0_ref, g_ref, cn_ref, hn_ref, *, d, d_in, steps):
    bw = d // RNN_BLOCKS
    log_sig = _log_sigmoid(lam_ref[...])
    hist = [cs_ref[:, j * d:(j + 1) * d] for j in range(CONV_W - 1)]
    h = [h0_ref[:, n * bw:(n + 1) * bw] for n in range(RNN_BLOCKS)]
    for s in range(steps):
        x = z_ref[:, s * d_in:s * d_in + d]
        hist.append(x)
        xc = cb_ref[...]
        for j in range(CONV_W):
            xc = xc + cw_ref[j:j + 1, :] * hist[-1 - j]
        for n in range(RNN_BLOCKS):
            sl = slice(n * bw, (n + 1) * bw)
            a, b = _lru_gates(xc[:, sl], wgx_ref[n], bgx_ref[:, sl], wga_ref[n], bga_ref[:, sl],
                              log_sig[:, sl])
            h[n] = a * h[n] + b
            y = z_ref[:, s * d_in + d + n * bw:s * d_in + d + (n + 1) * bw]
            g_ref[:, s * d + n * bw:s * d + (n + 1) * bw] = (_gelu(y) * h[n]).astype(g_ref.dtype)
    for j in range(CONV_W - 1):
        cn_ref[:, j * d:(j + 1) * d] = hist[len(hist) - (CONV_W - 1) + j]
    for n in range(RNN_BLOCKS):
        hn_ref[:, n * bw:(n + 1) * bw] = h[n]


def _rnn_sample(z2, conv_w, conv_b, wgx, bgx, wga, bga, lam, conv_state, h0, *, d, steps):
    nb = z2.shape[0]
    d_in = z2.shape[1] // steps
    return pl.pallas_call(
        functools.partial(_rnn_sample_kernel, d=d, d_in=d_in, steps=steps),
        out_shape=[
            jax.ShapeDtypeStruct((nb, steps * d), MXU_DTYPE),
            jax.ShapeDtypeStruct((nb, (CONV_W - 1) * d), F32),
            jax.ShapeDtypeStruct((nb, d), F32),
        ],
        compiler_params=pltpu.CompilerParams(vmem_limit_bytes=VMEM_LIMIT_BYTES),
        name="rnn_sample",
    )(z2, conv_w, conv_b, wgx, bgx, wga, bga, lam, conv_state, h0)


def _attn_prompt_kernel(q_ref, kc_ref, kp_ref, vc_ref, vp_ref, o_ref, lse_ref, *, dil, hpb, nbs):
    nblk = pl.program_id(0)
    hchunk = pl.program_id(1)
    span = BAND * dil
    qi = lax.broadcasted_iota(jnp.int32, (BAND, 2 * BAND), 0)
    ki = lax.broadcasted_iota(jnp.int32, (BAND, 2 * BAND), 1)
    window = (ki >= qi) & (ki <= qi + BAND)
    first_key = jnp.where(nblk > 0, 0, BAND)
    lane = lax.broadcasted_iota(jnp.int32, (BAND, LANES), 1)
    scale = HEAD_DIM ** -0.5

    @pl.when(hchunk == 0)
    def _():
        lse_ref[...] = jnp.zeros_like(lse_ref)

    def band_rows(b, r):
        return pl.ds(b * span + r, BAND, stride=dil) if dil > 1 else slice(b * span, (b + 1) * span)

    for b in range(nbs):
        mask = window & (ki >= first_key) if b == 0 else window
        for r in range(dil):
            rows = band_rows(b, r)
            if b == 0:
                k_prev, v_prev, prows = kp_ref, vp_ref, band_rows(0, r)
            else:
                k_prev, v_prev, prows = kc_ref, vc_ref, band_rows(b - 1, r)
            lse_tile = lse_ref[rows, :]
            for hh in range(hpb):
                sl = slice(hh * HEAD_DIM, (hh + 1) * HEAD_DIM)
                qh = (q_ref[rows, sl] * scale).astype(MXU_DTYPE)
                kk = jnp.concatenate([k_prev[prows, sl], kc_ref[rows, sl]], axis=0).astype(MXU_DTYPE)
                vv = jnp.concatenate([v_prev[prows, sl], vc_ref[rows, sl]], axis=0).astype(MXU_DTYPE)
                s = lax.dot_general(qh, kk, (((1,), (1,)), ((), ())), preferred_element_type=F32)
                s = jnp.where(mask, s, NEG_INF)
                m = jnp.max(s, axis=-1, keepdims=True)
                p = jnp.exp(s - m)
                den = jnp.sum(p, axis=-1, keepdims=True)
                o = jnp.dot(p.astype(MXU_DTYPE), vv, preferred_element_type=F32)
                o_ref[rows, sl] = o / den
                lse_tile = jnp.where(lane == hchunk * hpb + hh, m + jnp.log(den), lse_tile)
            lse_ref[rows, :] = lse_tile


ATTN_SAMPLE_CHUNK_ROWS = 512
ATTN_BANDS_PER_STEP = 4
ATTN_MAX_BLOCK_ROWS = 4096


def _attn_prompt(z, gi, dil, *, col_q, col_k, col_v):
    t = z.shape[0]
    gw = HEADS * HEAD_DIM
    hpb = HEADS if dil == 1 else 1
    cw = hpb * HEAD_DIM
    span = BAND * dil
    nbs = math.gcd(ATTN_BANDS_PER_STEP, t // span)
    while nbs > 1 and nbs * span > ATTN_MAX_BLOCK_ROWS:
        nbs //= 2
    rb = nbs * span
    per = gw // cw
    cur = lambda c: (lambda n, h: (n, (c + gi) * per + h))
    prev = lambda c: (lambda n, h: (jnp.maximum(n * nbs - 1, 0), (c + gi) * per + h))
    return pl.pallas_call(
        functools.partial(_attn_prompt_kernel, dil=dil, hpb=hpb, nbs=nbs),
        grid=(t // rb, HEADS // hpb),
        in_specs=[
            pl.BlockSpec((rb, cw), cur(col_q)),
            pl.BlockSpec((rb, cw), cur(col_k)),
            pl.BlockSpec((span, cw), prev(col_k)),
            pl.BlockSpec((rb, cw), cur(col_v)),
            pl.BlockSpec((span, cw), prev(col_v)),
        ],
        out_specs=[
            pl.BlockSpec((rb, cw), lambda n, h: (n, h)),
            pl.BlockSpec((rb, LANES), lambda n, h: (n, 0)),
        ],
        out_shape=[
            jax.ShapeDtypeStruct((t, gw), F32),
            jax.ShapeDtypeStruct((t, LANES), F32),
        ],
        compiler_params=_params("parallel", "arbitrary"),
        name=f"attn_prompt_g{gi}",
    )(z, z, z, z, z)


def _attn_sample_kernel(q_ref, kn_ref, vn_ref, c_ref, nx_ref, o_ref, lse_ref, new_ref,
                        m_sc, den_sc, acc_sc, *, dil, steps):
    ci = pl.program_id(1)
    nchunks = pl.num_programs(1)
    rch = c_ref.shape[1]

    def attend(s, krows, vrows, valid, first):
        q = q_ref[0, s] * (HEAD_DIM ** -0.5)
        sc = jnp.sum(q[None] * krows, axis=-1, keepdims=True)
        if valid is not None:
            sc = jnp.where(valid, sc, NEG_INF)
        m_blk = jnp.max(sc, axis=0)
        if first:
            m_new = m_blk
            p = jnp.exp(sc - m_new[None])
            den = jnp.sum(p, axis=0)
            acc = jnp.sum(p * vrows, axis=0)
        else:
            m_old = m_sc[s][:, 0:1]
            m_new = jnp.maximum(m_old, m_blk)
            alpha = jnp.exp(m_old - m_new)
            p = jnp.exp(sc - m_new[None])
            den = alpha * den_sc[s][:, 0:1] + jnp.sum(p, axis=0)
            acc = alpha * acc_sc[s] + jnp.sum(p * vrows, axis=0)
        m_sc[s] = jnp.broadcast_to(m_new, (HEADS, HEAD_DIM))
        den_sc[s] = jnp.broadcast_to(den, (HEADS, HEAD_DIM))
        acc_sc[s] = acc

    @pl.when(ci == 0)
    def _():
        for s in range(steps):
            lo = 0 if dil == 1 else s
            attend(s, kn_ref[0, lo:s + 1], vn_ref[0, lo:s + 1], None, True)

    for s in range(steps):
        if dil == 1:
            ridx = lax.broadcasted_iota(jnp.int32, (rch, 1, 1), 0) + ci * rch
            attend(s, c_ref[0, :, 0], c_ref[0, :, 1], ridx >= s, False)
        else:
            n = rch // dil
            attend(s, c_ref[0, pl.ds(s, n, stride=dil), 0], c_ref[0, pl.ds(s, n, stride=dil), 1],
                   None, False)

    @pl.when(ci == nchunks - 1)
    def _():
        for s in range(steps):
            o_ref[0, s] = acc_sc[s] / den_sc[s]
            lse_ref[0, s] = m_sc[s] + jnp.log(den_sc[s])

    new_ref[0, 0:rch - steps] = c_ref[0, steps:rch]

    @pl.when(ci < nchunks - 1)
    def _():
        new_ref[0, rch - steps:rch] = nx_ref[0]

    @pl.when(ci == nchunks - 1)
    def _():
        new_ref[0, rch - steps:rch, 0] = kn_ref[0]
        new_ref[0, rch - steps:rch, 1] = vn_ref[0]


def _attn_sample(q, kn, vn, cache, dil, *, rch):
    nb, steps = q.shape[0], q.shape[1]
    lg = cache.shape[1]
    nchunks = lg // rch
    per = rch // steps
    last = lg // steps - 1
    small = pl.BlockSpec((1, steps, HEADS, HEAD_DIM), lambda b, c: (b, 0, 0, 0))
    big = pl.BlockSpec((1, rch, 2, HEADS, HEAD_DIM), lambda b, c: (b, c, 0, 0, 0))
    nxt = pl.BlockSpec((1, steps, 2, HEADS, HEAD_DIM),
                       lambda b, c: (b, jnp.minimum((c + 1) * per, last), 0, 0, 0))
    return pl.pallas_call(
        functools.partial(_attn_sample_kernel, dil=dil, steps=steps),
        grid=(nb, nchunks),
        in_specs=[small, small, small, big, nxt],
        out_specs=[small, small, big],
        out_shape=[
            jax.ShapeDtypeStruct(q.shape, F32),
            jax.ShapeDtypeStruct(q.shape, F32),
            jax.ShapeDtypeStruct(cache.shape, cache.dtype),
        ],
        scratch_shapes=[pltpu.VMEM((steps, HEADS, HEAD_DIM), F32)] * 3,
        compiler_params=_params("parallel", "arbitrary"),
        name=f"attn_sample_d{dil}",
    )(q, kn, vn, cache, cache)


def _merge_kernel(g_ref, o0_ref, o1_ref, o2_ref, l0_ref, l1_ref, l2_ref, ga_ref, gb_ref,
                  wr_ref, wa_ref, out_ref, att_sc):
    @pl.when(pl.program_id(1) == 0)
    def _():
        l0, l1, l2 = l0_ref[...], l1_ref[...], l2_ref[...]
        mx = jnp.maximum(jnp.maximum(l0, l1), l2)
        e0, e1, e2 = jnp.exp(l0 - mx), jnp.exp(l1 - mx), jnp.exp(l2 - mx)
        inv = 1.0 / (e0 + e1 + e2)
        w0, w1, w2 = e0 * inv, e1 * inv, e2 * inv
        for h in range(HEADS):
            sl = slice(h * HEAD_DIM, (h + 1) * HEAD_DIM)
            att = (w0[:, h:h + 1] * o0_ref[:, sl] + w1[:, h:h + 1] * o1_ref[:, sl]
                   + w2[:, h:h + 1] * o2_ref[:, sl])
            att_sc[:, sl] = att.astype(att_sc.dtype)

    rnn_out = jnp.dot(g_ref[...], wr_ref[...], preferred_element_type=F32)
    att_out = jnp.dot(att_sc[...], wa_ref[...], preferred_element_type=F32)
    mixed = jax.nn.sigmoid(ga_ref[...]) * rnn_out + jax.nn.sigmoid(gb_ref[...]) * att_out
    out_ref[...] = mixed.astype(out_ref.dtype)


def _merge(g, outs, lses, z, w_rnn, w_att, *, col_ga, col_gb, tm, tn):
    t, d = g.shape
    gw = HEADS * HEAD_DIM
    nj = d // tn
    row = lambda i, j: (i, 0)
    return pl.pallas_call(
        _merge_kernel,
        grid=(t // tm, nj),
        in_specs=[
            pl.BlockSpec((tm, d), row),
            pl.BlockSpec((tm, gw), row), pl.BlockSpec((tm, gw), row), pl.BlockSpec((tm, gw), row),
            pl.BlockSpec((tm, LANES), row), pl.BlockSpec((tm, LANES), row),
            pl.BlockSpec((tm, LANES), row),
            pl.BlockSpec((tm, tn), lambda i, j: (i, col_ga + j)),
            pl.BlockSpec((tm, tn), lambda i, j: (i, col_gb + j)),
            pl.BlockSpec((d, tn), lambda i, j: (0, j)),
            pl.BlockSpec((gw, tn), lambda i, j: (0, j)),
        ],
        out_specs=pl.BlockSpec((tm, tn), lambda i, j: (i, j)),
        out_shape=jax.ShapeDtypeStruct((t, d), MXU_DTYPE),
        scratch_shapes=[pltpu.VMEM((tm, gw), MXU_DTYPE)],
        compiler_params=_params("parallel", "arbitrary"),
        name="merge",
    )(g, *outs, *lses, z, z, w_rnn, w_att)


def _oproj_kernel(x_ref, mx_ref, wo_ref, gain_ref, h_ref, hn_ref, hnt_ref):
    h = x_ref[...] + jnp.dot(mx_ref[...], wo_ref[...], preferred_element_type=F32)
    h_ref[...] = h
    hn = _rms(h, gain_ref[...])
    hn_ref[...] = hn.astype(hn_ref.dtype)
    hnt_ref[...] = hn.T.astype(hnt_ref.dtype)


def _oproj(x, mixed, w_o, gain, *, tm):
    t, d = x.shape
    row = lambda i: (i, 0)
    return pl.pallas_call(
        _oproj_kernel,
        grid=(t // tm,),
        in_specs=[
            pl.BlockSpec((tm, d), row),
            pl.BlockSpec((tm, d), row),
            pl.BlockSpec((d, d), lambda i: (0, 0)),
            pl.BlockSpec((1, d), lambda i: (0, 0)),
        ],
        out_specs=[pl.BlockSpec((tm, d), row), pl.BlockSpec((tm, d), row),
                   pl.BlockSpec((d, tm), lambda i: (0, i))],
        out_shape=[jax.ShapeDtypeStruct((t, d), F32), jax.ShapeDtypeStruct((t, d), MXU_DTYPE),
                   jax.ShapeDtypeStruct((d, t), MXU_DTYPE)],
        compiler_params=_params("parallel"),
        name="oproj",
    )(x, mixed, w_o, gain)


def _all_sublanes(x, op):
    for sh in (4, 2, 1):
        x = op(x, pltpu.roll(x, sh, 0))
    return x


def _take_ranked(vals, order, count):
    rest = vals
    rank = jnp.full(vals.shape, float(count), F32)
    tops, firsts = [], []
    for k in range(count):
        m = _all_sublanes(jnp.max(rest, axis=0), jnp.maximum)
        first = _all_sublanes(jnp.min(jnp.where(rest == m[None], order, POS_INF), axis=0),
                              jnp.minimum)
        hit = order == first[None]
        rest = jnp.where(hit, NEG_INF, rest)
        rank = jnp.where(hit, float(k), rank)
        tops.append(m)
        firsts.append(first)
    return tops, firsts, rank


def _sort_pairs(n):
    pairs, p = [], 1
    while p < n:
        k = p
        while k >= 1:
            for j in range(k % p, n - k, 2 * k):
                for i in range(min(k, n - j - k)):
                    if (i + j) // (2 * p) == (i + j + k) // (2 * p):
                        pairs.append((i + j, i + j + k))
            k //= 2
        p *= 2
    return tuple(pairs)


def _bitonic_pairs(n):
    pairs, dist = [], n // 2
    while dist >= 1:
        pairs += [(i, i + dist) for i in range(n) if (i // dist) % 2 == 0]
        dist //= 2
    return tuple(pairs)


_SORT_TOPK = _sort_pairs(PEER_TOPK)
_MERGE_TOPK = _bitonic_pairs(PEER_TOPK)


def _top_values(vals):
    nt = PEER_TOPK
    v = [vals[k] for k in range(nt)]

    def exchange(pairs):
        for a, b in pairs:
            v[a], v[b] = jnp.maximum(v[a], v[b]), jnp.minimum(v[a], v[b])

    exchange(_SORT_TOPK)
    for sh in (4, 2, 1):
        other = [pltpu.roll(x, sh, 0) for x in v]
        v = [jnp.maximum(v[k], other[nt - 1 - k]) for k in range(nt)]
        exchange(_MERGE_TOPK)
    return v


def _route_lanes_by_value(s0, s1):
    nt = PEER_TOPK
    nk, ln = s0.shape
    nv = nk // SUBLANES
    assert nv == nt == 2 * SUBLANES
    s0 = s0.reshape(nv, SUBLANES, ln)
    s1 = s1.reshape(nv, SUBLANES, ln)
    top0 = _top_values(s0)
    top1 = _top_values(s1)
    sub = lax.broadcasted_iota(jnp.int32, (SUBLANES, ln), 0)

    def on_sublanes(rows):
        out = rows[0]
        for b in range(1, SUBLANES):
            out = jnp.where(sub == b, rows[b], out)
        return out

    def count(mask):
        return _all_sublanes(jnp.where(mask, 1.0, 0.0), jnp.add)

    t1lo, t1hi = on_sublanes(top1[:SUBLANES]), on_sublanes(top1[SUBLANES:])
    t0hi = on_sublanes(top0[SUBLANES:])
    pieces = [top0[0] + t1lo, top0[0] + t1hi]
    for a in range(1, SUBLANES):
        pieces.append(jnp.where(sub < nt // (a + 1), top0[a] + t1lo, NEG_INF))
    pieces.append(t0hi + top1[0])
    pad = jnp.full((SUBLANES, ln), NEG_INF, F32)
    best = _top_values(jnp.stack(pieces + [pad] * (nt - len(pieces))))
    cut = best[nt - 1]
    zsum = jnp.ones((SUBLANES, ln), F32)
    for k in range(1, nt):
        zsum = zsum + jnp.exp(best[k] - best[0])

    counts = [count(pieces[0] >= cut) + count(pieces[1] >= cut)]
    counts += [count(pieces[a + 1] >= cut) for a in range(1, SUBLANES)]
    total = count(pieces[SUBLANES + 1] >= cut)
    th_vals = []
    for a in range(SUBLANES):
        total = total + counts[a]
        tv = jnp.full((SUBLANES, ln), POS_INF, F32)
        for m in range(nt // (a + 1)):
            tv = jnp.where(counts[a] == float(m + 1), top1[m], tv)
        th_vals.append(tv)

    member0 = s0 >= top0[nt - 1][None]
    member1 = s1 >= top1[nt - 1][None]
    th = jnp.where(member0 & (s0 + top1[0][None] >= cut[None]), top1[0][None], POS_INF)
    for a in range(SUBLANES):
        th = jnp.where(s0 == top0[a][None], th_vals[a][None], th)
    c = jnp.exp(s0 - top0[0][None]) / zsum[None]
    r1 = jnp.where(member1, s1, NEG_INF)
    e1 = jnp.exp(s1 - top1[0][None])

    gap = top0[0] - top0[1]
    for t in (top0, top1):
        for k in range(nt - 1):
            gap = jnp.minimum(gap, t[k] - t[k + 1])
    n0 = _all_sublanes(jnp.sum(jnp.where(member0, 1.0, 0.0), axis=0), jnp.add)
    n1 = _all_sublanes(jnp.sum(jnp.where(member1, 1.0, 0.0), axis=0), jnp.add)
    tie = (jnp.where(gap <= 0.0, 1.0, 0.0) + jnp.abs(n0 - float(nt)) + jnp.abs(n1 - float(nt))
           + jnp.abs(total - float(nt)))
    return tuple(x.reshape(nk, ln) for x in (th, c, r1, e1)) + (tie,)


def _route_lanes(s0, s1):
    th, c, r1, e1, tie = _route_lanes_by_value(s0, s1)
    return lax.cond(jnp.max(tie) > 0.0, lambda: _route_lanes_by_rank(s0, s1),
                    lambda: (th, c, r1, e1))


def _route_lanes_by_rank(s0, s1):
    nt = PEER_TOPK
    nk, ln = s0.shape
    nv = nk // SUBLANES
    s0 = s0.reshape(nv, SUBLANES, ln)
    s1 = s1.reshape(nv, SUBLANES, ln)
    key = (lax.broadcasted_iota(jnp.int32, (nv, SUBLANES, ln), 0) * SUBLANES
           + lax.broadcasted_iota(jnp.int32, (nv, SUBLANES, ln), 1)).astype(F32)
    top0, _, rank0 = _take_ranked(s0, key, nt)
    top1, _, rank1 = _take_ranked(s1, key, nt)

    sub = lax.broadcasted_iota(jnp.int32, (SUBLANES, ln), 0)
    subf = sub.astype(F32)

    def on_sublanes(rows):
        out = rows[0]
        for b in range(1, SUBLANES):
            out = jnp.where(sub == b, rows[b], out)
        return out

    assert nt == 2 * SUBLANES
    t1lo, t1hi = on_sublanes(top1[:SUBLANES]), on_sublanes(top1[SUBLANES:])
    t0hi = on_sublanes(top0[SUBLANES:])
    vals = [top0[0] + t1lo, top0[0] + t1hi]
    poss = [subf, subf + SUBLANES]
    for a in range(1, SUBLANES):
        vals.append(jnp.where(sub < nt // (a + 1), top0[a] + t1lo, NEG_INF))
        poss.append(subf + a * nt)
    vals.append(t0hi + top1[0])
    poss.append((subf + SUBLANES) * nt)
    best, bpos, _ = _take_ranked(jnp.stack(vals), jnp.stack(poss), nt)

    zsum = jnp.zeros((SUBLANES, ln), F32)
    cnt_lo = jnp.zeros((SUBLANES, ln), F32)
    cnt_hi = jnp.zeros((SUBLANES, ln), F32)
    for k in range(nt):
        zsum = zsum + jnp.exp(best[k] - best[0])
        a_k = jnp.floor(bpos[k] * (1.0 / nt))
        cnt_lo = cnt_lo + jnp.where(subf == a_k, 1.0, 0.0)
        cnt_hi = cnt_hi + jnp.where(subf + SUBLANES == a_k, 1.0, 0.0)

    th = jnp.full((nv, SUBLANES, ln), POS_INF, F32)
    for a in range(nt):
        cnt = cnt_lo if a < SUBLANES else cnt_hi
        n_a = jnp.broadcast_to(cnt[a % SUBLANES:a % SUBLANES + 1], (SUBLANES, ln))
        th = jnp.where(rank0 == float(a), (1.0 - n_a)[None], th)
    c = jnp.exp(s0 - top0[0][None]) / zsum[None]
    r1 = jnp.where(rank1 < float(nt), -rank1, NEG_INF)
    e1 = jnp.exp(s1 - top1[0][None])
    return tuple(x.reshape(nk, ln) for x in (th, c, r1, e1))


def _route_kernel(hn_ref, wq_ref, sk_ref, th_ref, c_ref, r1_ref, e1_ref, q_sc):
    q = jnp.dot(hn_ref[...], wq_ref[...], preferred_element_type=F32)
    tm = hn_ref.shape[0]
    d_half = sk_ref.shape[2]
    for hp in range(2 * PEER_HEADS):
        q_sc[hp] = q[:, hp * d_half:(hp + 1) * d_half]

    def head(h, _):
        def scores(half):
            return lax.dot_general(sk_ref[2 * h + half], q_sc[2 * h + half],
                                   (((1,), (1,)), ((), ())),
                                   preferred_element_type=F32)

        s0, s1 = scores(0), scores(1)
        for lc in range(tm // LANES):
            ls = slice(lc * LANES, (lc + 1) * LANES)
            th, c, r1, e1 = _route_lanes(s0[:, ls], s1[:, ls])
            th_ref[h, :, ls] = th
            c_ref[h, :, ls] = c
            r1_ref[h, :, ls] = r1
            e1_ref[h, :, ls] = e1
        return 0

    lax.fori_loop(0, PEER_HEADS, head, 0)


def _route(hn, w_query, sub_keys, *, tm):
    t, d = hn.shape
    dq = w_query.shape[1]
    slab = jax.ShapeDtypeStruct((PEER_HEADS, N_KEYS, t), F32)
    slab_spec = pl.BlockSpec((PEER_HEADS, N_KEYS, tm), lambda i: (0, 0, i))
    return pl.pallas_call(
        _route_kernel,
        grid=(t // tm,),
        in_specs=[
            pl.BlockSpec((tm, d), lambda i: (i, 0)),
            pl.BlockSpec((d, dq), lambda i: (0, 0)),
            pl.BlockSpec(sub_keys.shape, lambda i: (0, 0, 0)),
        ],
        out_specs=[slab_spec] * 4,
        out_shape=[slab] * 4,
        scratch_shapes=[pltpu.VMEM((2 * PEER_HEADS, tm, sub_keys.shape[2]), F32)],
        compiler_params=_params("parallel"),
        name="peer_route",
    )(hn, w_query, sub_keys)


PEER_ROW_CHUNK = 32


def _peer_kernel(x_ref, u_ref, vt_ref, th_ref, c_ref, r1_ref, e1_ref, y_ref, act_sc, gate_sc,
                 coef_sc):
    e = pl.program_id(1)
    tm = x_ref.shape[1]
    ib = th_ref.shape[1]
    hk = ib // 2
    hr = hk * N_KEYS
    assert tm % LANES == 0 and N_KEYS % PEER_ROW_CHUNK == 0 and ib % 2 == 0

    @pl.when(e == 0)
    def _():
        y_ref[...] = jnp.zeros_like(y_ref)

    def first_matmul(half):
        rows = slice(half * hr, (half + 1) * hr)
        act_sc[rows, :] = jnp.dot(u_ref[rows, :], x_ref[...], preferred_element_type=F32)

    def routing_weights(half):
        for lc in range(tm // LANES):
            ls = slice(lc * LANES, (lc + 1) * LANES)
            for jc in range(N_KEYS // PEER_ROW_CHUNK):
                js = slice(jc * PEER_ROW_CHUNK, (jc + 1) * PEER_ROW_CHUNK)
                gates = [jnp.zeros((PEER_ROW_CHUNK, LANES), F32) for _ in range(hk)]
                for h in range(PEER_HEADS):
                    r1 = r1_ref[h, js, ls]
                    e1 = e1_ref[h, js, ls]
                    for k in range(hk):
                        ii = half * hk + k
                        th = th_ref[h, ii:ii + 1, ls]
                        ch = c_ref[h, ii:ii + 1, ls]
                        gates[k] = gates[k] + jnp.where(r1 >= th, e1 * ch, 0.0)
                for k in range(hk):
                    r0 = (half * hk + k) * N_KEYS + jc * PEER_ROW_CHUNK
                    gate_sc[r0:r0 + PEER_ROW_CHUNK, ls] = gates[k]

    def second_matmul(half):
        rows = slice(half * hr, (half + 1) * hr)
        coef_sc[rows, :] = (_gelu(act_sc[rows, :]) * gate_sc[rows, :]).astype(coef_sc.dtype)
        y_ref[...] += jnp.dot(vt_ref[:, rows], coef_sc[rows, :], preferred_element_type=F32)

    @pl.when(e >= 0)
    def _():
        first_matmul(0)
        routing_weights(0)

    @pl.when(e >= -1)
    def _():
        first_matmul(1)
        routing_weights(1)
        second_matmul(0)

    @pl.when(e >= -2)
    def _():
        second_matmul(1)


MXU_TILE = 256
PEER_CHAIN_ROWS = 16
PEER_CHAIN_KEYS = 4
PEER_CHAIN_BLOCK_ROWS = 128


def _zero_after(x):
    bits = pltpu.bitcast(x, jnp.uint32)
    z = lax.shift_right_logical(lax.shift_right_logical(bits, jnp.uint32(16)), jnp.uint32(16))
    return z[0:1, :].astype(F32).astype(MXU_DTYPE)


def _chained(blocks, pieces):
    nb, npc = len(blocks), len(pieces)
    out = []
    for b in range(nb):
        dep = None
        for p in range(b * npc // nb, (b + 1) * npc // nb):
            v = pieces[p]()
            dep = v if dep is None else dep + v
        blk = blocks[b]()
        if dep is not None:
            zero = _zero_after(dep)
            blk = blk + jnp.concatenate([zero] * (blk.shape[1] // LANES), axis=1)
        out.append(blk)
    return out


def _peer_chain_kernel(x_ref, u_ref, vt_ref, th_ref, c_ref, r1_ref, e1_ref, y_ref, gate_sc):
    e = pl.program_id(1)
    d, tm = x_ref.shape
    te = u_ref.shape[0]
    ib = th_ref.shape[1]
    nkt = d // MXU_TILE
    nj = N_KEYS // PEER_CHAIN_ROWS
    nchunks = (tm // LANES) * nj
    assert ib % PEER_CHAIN_KEYS == 0 and te % PEER_CHAIN_BLOCK_ROWS == 0

    @pl.when(e == 0)
    def _():
        y_ref[...] = jnp.zeros_like(y_ref)

    def gate_piece(ci, keys):
        def run():
            ls = slice((ci // nj) * LANES, (ci // nj + 1) * LANES)
            js = slice((ci % nj) * PEER_CHAIN_ROWS, (ci % nj + 1) * PEER_CHAIN_ROWS)
            gates = {ii: jnp.zeros((PEER_CHAIN_ROWS, LANES), F32) for ii in keys}
            for h in range(PEER_HEADS):
                r1 = r1_ref[h, js, ls]
                e1 = e1_ref[h, js, ls]
                for ii in keys:
                    th = th_ref[h, ii:ii + 1, ls]
                    ch = c_ref[h, ii:ii + 1, ls]
                    gates[ii] = gates[ii] + jnp.where(r1 >= th, e1 * ch, 0.0)
            total = None
            for ii in keys:
                r0 = ii * N_KEYS + (ci % nj) * PEER_CHAIN_ROWS
                gate_sc[r0:r0 + PEER_CHAIN_ROWS, ls] = gates[ii]
                total = gates[ii] if total is None else total + gates[ii]
            return jnp.sum(total.reshape(PEER_CHAIN_ROWS // SUBLANES, SUBLANES, LANES), axis=0)
        return run

    pieces = [gate_piece(ci, tuple(range(k0, k0 + PEER_CHAIN_KEYS)))
              for ci in range(nchunks) for k0 in range(0, ib, PEER_CHAIN_KEYS)]

    rows_per = PEER_CHAIN_BLOCK_ROWS
    nrb = te // rows_per
    u_blocks = [(lambda kt=kt, rb=rb: u_ref[rb * rows_per:(rb + 1) * rows_per,
                                            kt * MXU_TILE:(kt + 1) * MXU_TILE])
                for kt in range(nkt) for rb in range(nrb)]
    blocks = _chained(u_blocks, pieces)
    lhs = jnp.concatenate([jnp.concatenate(blocks[kt * nrb:(kt + 1) * nrb], axis=0)
                           for kt in range(nkt)], axis=1)
    act = jnp.dot(lhs, x_ref[...], preferred_element_type=F32)
    coef = (_gelu(act) * gate_sc[...]).astype(MXU_DTYPE)
    y_ref[...] += jnp.dot(vt_ref[...], coef, preferred_element_type=F32)


def _peer(hnt, u, vt, th, c, r1, e1, *, tm, ib):
    d, t = hnt.shape
    ne = u.shape[0]
    te = ib * N_KEYS
    slab_spec = pl.BlockSpec((PEER_HEADS, N_KEYS, tm), lambda i, e: (0, 0, i))
    first_spec = pl.BlockSpec((PEER_HEADS, ib, tm), lambda i, e: (0, e, i))
    if tm % (4 * LANES) == 0:
        return pl.pallas_call(
            _peer_chain_kernel,
            grid=(t // tm, ne // te),
            in_specs=[
                pl.BlockSpec((d, tm), lambda i, e: (0, i)),
                pl.BlockSpec((te, d), lambda i, e: (e, 0)),
                pl.BlockSpec((d, te), lambda i, e: (0, e)),
                first_spec, first_spec, slab_spec, slab_spec,
            ],
            out_specs=pl.BlockSpec((d, tm), lambda i, e: (0, i)),
            out_shape=jax.ShapeDtypeStruct((d, t), F32),
            scratch_shapes=[pltpu.VMEM((te, tm), F32)],
            compiler_params=_params("parallel", "arbitrary"),
            name="peer_experts_chain",
        )(hnt, u, vt, th, c, r1, e1)
    return pl.pallas_call(
        _peer_kernel,
        grid=(t // tm, ne // te),
        in_specs=[
            pl.BlockSpec((d, tm), lambda i, e: (0, i)),
            pl.BlockSpec((te, d), lambda i, e: (e, 0)),
            pl.BlockSpec((d, te), lambda i, e: (0, e)),
            first_spec, first_spec, slab_spec, slab_spec,
        ],
        out_specs=pl.BlockSpec((d, tm), lambda i, e: (0, i)),
        out_shape=jax.ShapeDtypeStruct((d, t), F32),
        scratch_shapes=[pltpu.VMEM((te, tm), F32), pltpu.VMEM((te, tm), F32),
                        pltpu.VMEM((te, tm), MXU_DTYPE)],
        compiler_params=_params("parallel", "arbitrary"),
        name="peer_experts",
    )(hnt, u, vt, th, c, r1, e1)


def _final_kernel(h_ref, yt_ref, gain_ref, out_ref):
    out_ref[...] = _rms(h_ref[...] + yt_ref[...].T, gain_ref[...])


def _final(h, yt, gain, *, tm):
    t, d = h.shape
    return pl.pallas_call(
        _final_kernel,
        grid=(t // tm,),
        in_specs=[pl.BlockSpec((tm, d), lambda i: (i, 0)), pl.BlockSpec((d, tm), lambda i: (0, i)),
                  pl.BlockSpec((1, d), lambda i: (0, 0))],
        out_specs=pl.BlockSpec((tm, d), lambda i: (i, 0)),
        out_shape=jax.ShapeDtypeStruct((t, d), F32),
        compiler_params=_params("parallel"),
        name="final_norm",
    )(h, yt, gain)


TOKEN_TILES = {"inproj": 1024, "rnn": 256, "merge": 512, "oproj": 512, "route": 256,
               "peer": 512, "final": 512}


def _tile(n, stage):
    return min(n, TOKEN_TILES[stage])


def _token_stages(x2, z, g, outs, lses, w, cols):
    t = x2.shape[0]
    mixed = _merge(g, outs, lses, z, w["w_rnn_out"], w["w_att_out"], col_ga=cols["ga"],
                   col_gb=cols["gb"], tm=_tile(t, "merge"), tn=HEADS * HEAD_DIM)
    h, hn, hnt = _oproj(x2, mixed, w["w_o"], w["norm_ffn"], tm=_tile(t, "oproj"))
    th, c, r1, e1 = _route(hn, w["w_query"], w["sub_keys"], tm=_tile(t, "route"))
    yt = _peer(hnt, w["expert_u"], w["expert_vt"], th, c, r1, e1, tm=_tile(t, "peer"),
               ib=SUBLANES)
    return _final(h, yt, w["norm_final"], tm=_tile(t, "final"))


def kernel(x_prompt, x_sample, state_conv, state_h, cache_kv_w128, cache_kv_w512, cache_kv_w2048,
           norm_mix, w_in, conv_w, conv_b, w_gate_x, b_gate_x, w_gate_a, b_gate_a, lru_lambda,
           w_rnn_out, w_att_out, w_o, norm_ffn, w_query, sub_keys, expert_u, expert_v, norm_final):
    depth = w_in.shape[0]
    assert depth == 1 and x_prompt.shape[0] == 1
    batch, seq, d = x_prompt.shape
    nb, steps, _ = x_sample.shape
    d_in = w_in.shape[2]
    gw = HEADS * HEAD_DIM
    d_qkv = N_GROUPS * gw
    assert d_in == 4 * d + 3 * d_qkv and d % gw == 0
    caches = (cache_kv_w128, cache_kv_w512, cache_kv_w2048)
    col_q, col_k, col_v = 2 * d // gw, (2 * d + d_qkv) // gw, (2 * d + 2 * d_qkv) // gw
    cols = {"ga": (2 * d + 3 * d_qkv) // gw, "gb": (3 * d + 3 * d_qkv) // gw}

    mx = lambda a: a.astype(MXU_DTYPE)
    row = lambda a: a.reshape(1, -1).astype(F32)
    w = {
        "w_in": mx(w_in[0]), "w_rnn_out": mx(w_rnn_out[0]), "w_att_out": mx(w_att_out[0]),
        "w_o": mx(w_o[0]), "w_query": mx(w_query[0]),
        "sub_keys": sub_keys[0].reshape(2 * PEER_HEADS, N_KEYS, -1),
        "expert_u": mx(expert_u[0]), "expert_vt": mx(expert_v[0]).T,
        "norm_ffn": row(norm_ffn[0]), "norm_final": row(norm_final),
    }
    rnn_w = (conv_w[0], row(conv_b[0]), mx(w_gate_x[0]), row(b_gate_x[0]), mx(w_gate_a[0]),
             row(b_gate_a[0]), row(lru_lambda[0]))
    gain_mix = row(norm_mix[0])

    xp = x_prompt.reshape(seq, d)
    zp = _inproj(xp, gain_mix, w["w_in"], tm=_tile(seq, "inproj"), tn=gw)
    g_p, past_p, h_p = _rnn_prompt(zp, *rnn_w, jnp.zeros((SUBLANES, d), F32),
                                   jnp.zeros((1, d), F32), d=d, tt=_tile(seq, "rnn"))
    outs, lses, kv_p = [], [], []
    for gi, (win, dil) in enumerate(ATT_GROUPS):
        o, lse = _attn_prompt(zp, gi, dil, col_q=col_q, col_k=col_k, col_v=col_v)
        outs.append(o)
        lses.append(lse)
        keep = min(win, seq)
        kcol = (col_k + gi) * gw
        vcol = (col_v + gi) * gw
        kv = jnp.stack([zp[seq - keep:, kcol:kcol + gw], zp[seq - keep:, vcol:vcol + gw]], axis=1)
        kv_p.append(kv.reshape(1, 1, keep, 2, HEADS, HEAD_DIM))
    y_p = _token_stages(xp, zp, g_p, outs, lses, w, cols).reshape(batch, seq, d)
    conv_p = past_p[SUBLANES - (CONV_W - 1):].reshape(1, 1, CONV_W - 1, d)
    h_p = h_p.reshape(1, 1, d)

    ts = nb * steps
    xs = x_sample.reshape(ts, d)
    zs = _inproj(xs, gain_mix, w["w_in"], tm=ts, tn=gw)
    g_s, conv_s, h_s = _rnn_sample(zs.reshape(nb, steps * d_in), *rnn_w,
                                   state_conv[0].reshape(nb, (CONV_W - 1) * d), state_h[0],
                                   d=d, steps=steps)
    g_s = g_s.reshape(ts, d)
    qkv = zs[:, 2 * d:2 * d + 3 * d_qkv].reshape(nb, steps, 3, N_GROUPS, HEADS, HEAD_DIM)
    outs, lses, kv_s = [], [], []
    for gi, (win, dil) in enumerate(ATT_GROUPS):
        cache = caches[gi][0]
        o, lse, new_cache = _attn_sample(qkv[:, :, 0, gi], qkv[:, :, 1, gi], qkv[:, :, 2, gi], cache,
                                         dil, rch=min(cache.shape[1], ATTN_SAMPLE_CHUNK_ROWS))
        outs.append(o.reshape(ts, gw))
        lse = lse[..., 0].reshape(ts, HEADS)
        lses.append(jnp.pad(lse, ((0, 0), (0, LANES - HEADS))))
        kv_s.append(new_cache[None])
    y_s = _token_stages(xs, zs, g_s, outs, lses, w, cols).reshape(nb, steps, d)
    conv_s = conv_s.reshape(1, nb, CONV_W - 1, d)
    h_s = h_s.reshape(1, nb, d)

    return (y_p, y_s, conv_p, h_p, kv_p[0], kv_p[1], kv_p[2],
            conv_s, h_s, kv_s[0], kv_s[1], kv_s[2])
```

```python
import functools
import math

import jax
import jax.numpy as jnp
from jax import lax
from jax.experimental import pallas as pl
from jax.experimental.pallas import tpu as pltpu

F32 = jnp.float32
MXU_DTYPE = jnp.bfloat16

RNN_BLOCKS = 8
CONV_W = 4
LRU_C = 8.0
ATT_GROUPS = ((128, 1), (512, 4), (2048, 16))
N_GROUPS = len(ATT_GROUPS)
HEADS = 8
HEAD_DIM = 128
BAND = 128
PEER_HEADS = 8
N_KEYS = 128
PEER_TOPK = 16
NORM_EPS = 1e-6

LANES = 128
SUBLANES = 8
VMEM_LIMIT_BYTES = 56 * 1024 * 1024

NEG_INF = float("-inf")
POS_INF = float("inf")


def _params(*sem):
    return pltpu.CompilerParams(dimension_semantics=sem, vmem_limit_bytes=VMEM_LIMIT_BYTES)


def _gelu(x):
    return 0.5 * x * (1.0 + lax.erf(x * math.sqrt(0.5)))


def _rms(x, gain):
    ms = jnp.mean(x * x, axis=-1, keepdims=True)
    return x * lax.rsqrt(ms + NORM_EPS) * gain


def _inproj_kernel(x_ref, g_ref, w_ref, z_ref, xn_sc):
    @pl.when(pl.program_id(1) == 0)
    def _():
        xn_sc[...] = _rms(x_ref[...], g_ref[...]).astype(MXU_DTYPE)

    z_ref[...] = jnp.dot(xn_sc[...], w_ref[...], preferred_element_type=F32)


def _inproj(x, gain, w, *, tm, tn):
    m, k = x.shape
    n = w.shape[1]
    return pl.pallas_call(
        _inproj_kernel,
        grid=(m // tm, n // tn),
        in_specs=[
            pl.BlockSpec((tm, k), lambda i, j: (i, 0)),
            pl.BlockSpec((1, k), lambda i, j: (0, 0)),
            pl.BlockSpec((k, tn), lambda i, j: (0, j)),
        ],
        out_specs=pl.BlockSpec((tm, tn), lambda i, j: (i, j)),
        out_shape=jax.ShapeDtypeStruct((m, n), F32),
        scratch_shapes=[pltpu.VMEM((tm, k), MXU_DTYPE)],
        compiler_params=_params("parallel", "arbitrary"),
        name="inproj",
    )(x, gain, w)


def _lru_gates(xc, wgx, bgx, wga, bga, log_sig):
    xb = xc.astype(MXU_DTYPE)
    gx = jax.nn.sigmoid(jnp.dot(xb, wgx, preferred_element_type=F32) + bgx)
    ga = jax.nn.sigmoid(jnp.dot(xb, wga, preferred_element_type=F32) + bga)
    a = jnp.exp(LRU_C * ga * log_sig)
    b = jnp.sqrt(1.0 - a * a) * gx * xc
    return a, b


def _log_sigmoid(x):
    return -(jnp.maximum(-x, 0.0) + jnp.log1p(jnp.exp(-jnp.abs(x))))


def _rnn_prompt_kernel(xr_ref, yr_ref, cw_ref, cb_ref, wgx_ref, bgx_ref, wga_ref, bga_ref, lam_ref,
                       p0_ref, h0_ref, g_ref, pn_ref, hn_ref, p_sc, h_sc, a_sc, b_sc):
    tt, d = xr_ref.shape
    bw = d // RNN_BLOCKS

    @pl.when(pl.program_id(0) == 0)
    def _():
        p_sc[...] = p0_ref[...]
        h_sc[...] = h0_ref[...]

    x = xr_ref[...]
    past = p_sc[...]
    row8 = lax.broadcasted_iota(jnp.int32, (SUBLANES, d), 0)
    xc = cb_ref[...] + cw_ref[0:1, :] * x
    for j in range(1, CONV_W):
        xs = pltpu.roll(x, j, 0)
        head = jnp.where(row8 < j, pltpu.roll(past, j, 0), xs[:SUBLANES])
        xs = jnp.concatenate([head, xs[SUBLANES:]], axis=0)
        xc = xc + cw_ref[j:j + 1, :] * xs
    p_sc[...] = x[tt - SUBLANES:]
    pn_ref[...] = x[tt - SUBLANES:]

    log_sig = _log_sigmoid(lam_ref[...])
    rowm = lax.broadcasted_iota(jnp.int32, (tt, bw), 0) & (SUBLANES - 1)
    for n in range(RNN_BLOCKS):
        sl = slice(n * bw, (n + 1) * bw)
        a, b = _lru_gates(xc[:, sl], wgx_ref[n], bgx_ref[:, sl], wga_ref[n], bga_ref[:, sl],
                          log_sig[:, sl])
        for sh in (1, 2, 4):
            ok = rowm >= sh
            a_sh = pltpu.roll(a, sh, 0)
            b_sh = pltpu.roll(b, sh, 0)
            b = jnp.where(ok, a * b_sh + b, b)
            a = jnp.where(ok, a * a_sh, a)
        a_sc[:, sl] = a
        b_sc[:, sl] = b

    def group(gi, h):
        rows = pl.ds(pl.multiple_of(gi * SUBLANES, SUBLANES), SUBLANES)
        hh = a_sc[rows, :] * h + b_sc[rows, :]
        b_sc[rows, :] = hh
        return hh[SUBLANES - 1:SUBLANES, :]

    h_last = lax.fori_loop(0, tt // SUBLANES, group, h_sc[...])
    h_sc[...] = h_last
    hn_ref[...] = h_last
    g_ref[...] = (_gelu(yr_ref[...]) * b_sc[...]).astype(g_ref.dtype)


def _rnn_prompt(z, conv_w, conv_b, wgx, bgx, wga, bga, lam, past8, h0, *, d, tt):
    t = z.shape[0]
    full2 = lambda i: (0, 0)
    full3 = lambda i: (0, 0, 0)
    bw = d // RNN_BLOCKS
    return pl.pallas_call(
        _rnn_prompt_kernel,
        grid=(t // tt,),
        in_specs=[
            pl.BlockSpec((tt, d), lambda i: (i, 0)),
            pl.BlockSpec((tt, d), lambda i: (i, 1)),
            pl.BlockSpec((CONV_W, d), full2),
            pl.BlockSpec((1, d), full2),
            pl.BlockSpec((RNN_BLOCKS, bw, bw), full3),
            pl.BlockSpec((1, d), full2),
            pl.BlockSpec((RNN_BLOCKS, bw, bw), full3),
            pl.BlockSpec((1, d), full2),
            pl.BlockSpec((1, d), full2),
            pl.BlockSpec((SUBLANES, d), full2),
            pl.BlockSpec((1, d), full2),
        ],
        out_specs=[
            pl.BlockSpec((tt, d), lambda i: (i, 0)),
            pl.BlockSpec((SUBLANES, d), full2),
            pl.BlockSpec((1, d), full2),
        ],
        out_shape=[
            jax.ShapeDtypeStruct((t, d), MXU_DTYPE),
            jax.ShapeDtypeStruct((SUBLANES, d), F32),
            jax.ShapeDtypeStruct((1, d), F32),
        ],
        scratch_shapes=[pltpu.VMEM((SUBLANES, d), F32), pltpu.VMEM((1, d), F32),
                        pltpu.VMEM((tt, d), F32), pltpu.VMEM((tt, d), F32)],
        compiler_params=_params("arbitrary"),
        name="rnn_prompt",
    )(z, z, conv_w, conv_b, wgx, bgx, wga, bga, lam, past8, h0)


def _rnn_sample_kernel(z_ref, cw_ref, cb_ref, wgx_ref, bgx_ref, wga_ref, bga_ref, lam_ref,
                       cs_ref, h0_ref, g_ref, cn_ref, hn_ref, *, d, d_in, steps):
    bw = d // RNN_BLOCKS
    log_sig = _log_sigmoid(lam_ref[...])
    hist = [cs_ref[:, j * d:(j + 1) * d] for j in range(CONV_W - 1)]
    h = [h0_ref[:, n * bw:(n + 1) * bw] for n in range(RNN_BLOCKS)]
    for s in range(steps):
        x = z_ref[:, s * d_in:s * d_in + d]
        hist.append(x)
        xc = cb_ref[...]
        for j in range(CONV_W):
            xc = xc + cw_ref[j:j + 1, :] * hist[-1 - j]
        for n in range(RNN_BLOCKS):
            sl = slice(n * bw, (n + 1) * bw)
            a, b = _lru_gates(xc[:, sl], wgx_ref[n], bgx_ref[:, sl], wga_ref[n], bga_ref[:, sl],
                              log_sig[:, sl])
            h[n] = a * h[n] + b
            y = z_ref[:, s * d_in + d + n * bw:s * d_in + d + (n + 1) * bw]
            g_ref[:, s * d + n * bw:s * d + (n + 1) * bw] = (_gelu(y) * h[n]).astype(g_ref.dtype)
    for j in range(CONV_W - 1):
        cn_ref[:, j * d:(j + 1) * d] = hist[len(hist) - (CONV_W - 1) + j]
    for n in range(RNN_BLOCKS):
        hn_ref[:, n * bw:(n + 1) * bw] = h[n]


def _rnn_sample(z2, conv_w, conv_b, wgx, bgx, wga, bga, lam, conv_state, h0, *, d, steps):
    nb = z2.shape[0]
    d_in = z2.shape[1] // steps
    return pl.pallas_call(
        functools.partial(_rnn_sample_kernel, d=d, d_in=d_in, steps=steps),
        out_shape=[
            jax.ShapeDtypeStruct((nb, steps * d), MXU_DTYPE),
            jax.ShapeDtypeStruct((nb, (CONV_W - 1) * d), F32),
            jax.ShapeDtypeStruct((nb, d), F32),
        ],
        compiler_params=pltpu.CompilerParams(vmem_limit_bytes=VMEM_LIMIT_BYTES),
        name="rnn_sample",
    )(z2, conv_w, conv_b, wgx, bgx, wga, bga, lam, conv_state, h0)


def _attn_prompt_kernel(q_ref, kc_ref, kp_ref, vc_ref, vp_ref, o_ref, lse_ref, *, dil, hpb, nbs):
    nblk = pl.program_id(0)
    hchunk = pl.program_id(1)
    span = BAND * dil
    qi = lax.broadcasted_iota(jnp.int32, (BAND, 2 * BAND), 0)
    ki = lax.broadcasted_iota(jnp.int32, (BAND, 2 * BAND), 1)
    window = (ki >= qi) & (ki <= qi + BAND)
    first_key = jnp.where(nblk > 0, 0, BAND)
    lane = lax.broadcasted_iota(jnp.int32, (BAND, LANES), 1)
    scale = HEAD_DIM ** -0.5

    @pl.when(hchunk == 0)
    def _():
        lse_ref[...] = jnp.zeros_like(lse_ref)

    def band_rows(b, r):
        return pl.ds(b * span + r, BAND, stride=dil) if dil > 1 else slice(b * span, (b + 1) * span)

    for b in range(nbs):
        mask = window & (ki >= first_key) if b == 0 else window
        for r in range(dil):
            rows = band_rows(b, r)
            if b == 0:
                k_prev, v_prev, prows = kp_ref, vp_ref, band_rows(0, r)
            else:
                k_prev, v_prev, prows = kc_ref, vc_ref, band_rows(b - 1, r)
            lse_tile = lse_ref[rows, :]
            for hh in range(hpb):
                sl = slice(hh * HEAD_DIM, (hh + 1) * HEAD_DIM)
                qh = (q_ref[rows, sl] * scale).astype(MXU_DTYPE)
                kk = jnp.concatenate([k_prev[prows, sl], kc_ref[rows, sl]], axis=0).astype(MXU_DTYPE)
                vv = jnp.concatenate([v_prev[prows, sl], vc_ref[rows, sl]], axis=0).astype(MXU_DTYPE)
                s = lax.dot_general(qh, kk, (((1,), (1,)), ((), ())), preferred_element_type=F32)
                s = jnp.where(mask, s, NEG_INF)
                m = jnp.max(s, axis=-1, keepdims=True)
                p = jnp.exp(s - m)
                den = jnp.sum(p, axis=-1, keepdims=True)
                o = jnp.dot(p.astype(MXU_DTYPE), vv, preferred_element_type=F32)
                o_ref[rows, sl] = o / den
                lse_tile = jnp.where(lane == hchunk * hpb + hh, m + jnp.log(den), lse_tile)
            lse_ref[rows, :] = lse_tile


ATTN_SAMPLE_CHUNK_ROWS = 512
ATTN_BANDS_PER_STEP = 4
ATTN_MAX_BLOCK_ROWS = 4096


def _attn_prompt(z, gi, dil, *, col_q, col_k, col_v):
    t = z.shape[0]
    gw = HEADS * HEAD_DIM
    hpb = HEADS if dil == 1 else 1
    cw = hpb * HEAD_DIM
    span = BAND * dil
    nbs = math.gcd(ATTN_BANDS_PER_STEP, t // span)
    while nbs > 1 and nbs * span > ATTN_MAX_BLOCK_ROWS:
        nbs //= 2
    rb = nbs * span
    per = gw // cw
    cur = lambda c: (lambda n, h: (n, (c + gi) * per + h))
    prev = lambda c: (lambda n, h: (jnp.maximum(n * nbs - 1, 0), (c + gi) * per + h))
    return pl.pallas_call(
        functools.partial(_attn_prompt_kernel, dil=dil, hpb=hpb, nbs=nbs),
        grid=(t // rb, HEADS // hpb),
        in_specs=[
            pl.BlockSpec((rb, cw), cur(col_q)),
            pl.BlockSpec((rb, cw), cur(col_k)),
            pl.BlockSpec((span, cw), prev(col_k)),
            pl.BlockSpec((rb, cw), cur(col_v)),
            pl.BlockSpec((span, cw), prev(col_v)),
        ],
        out_specs=[
            pl.BlockSpec((rb, cw), lambda n, h: (n, h)),
            pl.BlockSpec((rb, LANES), lambda n, h: (n, 0)),
        ],
        out_shape=[
            jax.ShapeDtypeStruct((t, gw), F32),
            jax.ShapeDtypeStruct((t, LANES), F32),
        ],
        compiler_params=_params("parallel", "arbitrary"),
        name=f"attn_prompt_g{gi}",
    )(z, z, z, z, z)


def _attn_sample_kernel(q_ref, kn_ref, vn_ref, c_ref, nx_ref, o_ref, lse_ref, new_ref,
                        m_sc, den_sc, acc_sc, *, dil, steps):
    ci = pl.program_id(1)
    nchunks = pl.num_programs(1)
    rch = c_ref.shape[1]

    def attend(s, krows, vrows, valid, first):
        q = q_ref[0, s] * (HEAD_DIM ** -0.5)
        sc = jnp.sum(q[None] * krows, axis=-1, keepdims=True)
        if valid is not None:
            sc = jnp.where(valid, sc, NEG_INF)
        m_blk = jnp.max(sc, axis=0)
        if first:
            m_new = m_blk
            p = jnp.exp(sc - m_new[None])
            den = jnp.sum(p, axis=0)
            acc = jnp.sum(p * vrows, axis=0)
        else:
            m_old = m_sc[s][:, 0:1]
            m_new = jnp.maximum(m_old, m_blk)
            alpha = jnp.exp(m_old - m_new)
            p = jnp.exp(sc - m_new[None])
            den = alpha * den_sc[s][:, 0:1] + jnp.sum(p, axis=0)
            acc = alpha * acc_sc[s] + jnp.sum(p * vrows, axis=0)
        m_sc[s] = jnp.broadcast_to(m_new, (HEADS, HEAD_DIM))
        den_sc[s] = jnp.broadcast_to(den, (HEADS, HEAD_DIM))
        acc_sc[s] = acc

    @pl.when(ci == 0)
    def _():
        for s in range(steps):
            lo = 0 if dil == 1 else s
            attend(s, kn_ref[0, lo:s + 1], vn_ref[0, lo:s + 1], None, True)

    for s in range(steps):
        if dil == 1:
            ridx = lax.broadcasted_iota(jnp.int32, (rch, 1, 1), 0) + ci * rch
            attend(s, c_ref[0, :, 0], c_ref[0, :, 1], ridx >= s, False)
        else:
            n = rch // dil
            attend(s, c_ref[0, pl.ds(s, n, stride=dil), 0], c_ref[0, pl.ds(s, n, stride=dil), 1],
                   None, False)

    @pl.when(ci == nchunks - 1)
    def _():
        for s in range(steps):
            o_ref[0, s] = acc_sc[s] / den_sc[s]
            lse_ref[0, s] = m_sc[s] + jnp.log(den_sc[s])

    new_ref[0, 0:rch - steps] = c_ref[0, steps:rch]

    @pl.when(ci < nchunks - 1)
    def _():
        new_ref[0, rch - steps:rch] = nx_ref[0]

    @pl.when(ci == nchunks - 1)
    def _():
        new_ref[0, rch - steps:rch, 0] = kn_ref[0]
        new_ref[0, rch - steps:rch, 1] = vn_ref[0]


def _attn_sample(q, kn, vn, cache, dil, *, rch):
    nb, steps = q.shape[0], q.shape[1]
    lg = cache.shape[1]
    nchunks = lg // rch
    per = rch // steps
    last = lg // steps - 1
    small = pl.BlockSpec((1, steps, HEADS, HEAD_DIM), lambda b, c: (b, 0, 0, 0))
    big = pl.BlockSpec((1, rch, 2, HEADS, HEAD_DIM), lambda b, c: (b, c, 0, 0, 0))
    nxt = pl.BlockSpec((1, steps, 2, HEADS, HEAD_DIM),
                       lambda b, c: (b, jnp.minimum((c + 1) * per, last), 0, 0, 0))
    return pl.pallas_call(
        functools.partial(_attn_sample_kernel, dil=dil, steps=steps),
        grid=(nb, nchunks),
        in_specs=[small, small, small, big, nxt],
        out_specs=[small, small, big],
        out_shape=[
            jax.ShapeDtypeStruct(q.shape, F32),
            jax.ShapeDtypeStruct(q.shape, F32),
            jax.ShapeDtypeStruct(cache.shape, cache.dtype),
        ],
        scratch_shapes=[pltpu.VMEM((steps, HEADS, HEAD_DIM), F32)] * 3,
        compiler_params=_params("parallel", "arbitrary"),
        name=f"attn_sample_d{dil}",
    )(q, kn, vn, cache, cache)


def _merge_kernel(g_ref, o0_ref, o1_ref, o2_ref, l0_ref, l1_ref, l2_ref, ga_ref, gb_ref,
                  wr_ref, wa_ref, out_ref, att_sc):
    @pl.when(pl.program_id(1) == 0)
    def _():
        l0, l1, l2 = l0_ref[...], l1_ref[...], l2_ref[...]
        mx = jnp.maximum(jnp.maximum(l0, l1), l2)
        e0, e1, e2 = jnp.exp(l0 - mx), jnp.exp(l1 - mx), jnp.exp(l2 - mx)
        inv = 1.0 / (e0 + e1 + e2)
        w0, w1, w2 = e0 * inv, e1 * inv, e2 * inv
        for h in range(HEADS):
            sl = slice(h * HEAD_DIM, (h + 1) * HEAD_DIM)
            att = (w0[:, h:h + 1] * o0_ref[:, sl] + w1[:, h:h + 1] * o1_ref[:, sl]
                   + w2[:, h:h + 1] * o2_ref[:, sl])
            att_sc[:, sl] = att.astype(att_sc.dtype)

    rnn_out = jnp.dot(g_ref[...], wr_ref[...], preferred_element_type=F32)
    att_out = jnp.dot(att_sc[...], wa_ref[...], preferred_element_type=F32)
    mixed = jax.nn.sigmoid(ga_ref[...]) * rnn_out + jax.nn.sigmoid(gb_ref[...]) * att_out
    out_ref[...] = mixed.astype(out_ref.dtype)


def _merge(g, outs, lses, z, w_rnn, w_att, *, col_ga, col_gb, tm, tn):
    t, d = g.shape
    gw = HEADS * HEAD_DIM
    nj = d // tn
    row = lambda i, j: (i, 0)
    return pl.pallas_call(
        _merge_kernel,
        grid=(t // tm, nj),
        in_specs=[
            pl.BlockSpec((tm, d), row),
            pl.BlockSpec((tm, gw), row), pl.BlockSpec((tm, gw), row), pl.BlockSpec((tm, gw), row),
            pl.BlockSpec((tm, LANES), row), pl.BlockSpec((tm, LANES), row),
            pl.BlockSpec((tm, LANES), row),
            pl.BlockSpec((tm, tn), lambda i, j: (i, col_ga + j)),
            pl.BlockSpec((tm, tn), lambda i, j: (i, col_gb + j)),
            pl.BlockSpec((d, tn), lambda i, j: (0, j)),
            pl.BlockSpec((gw, tn), lambda i, j: (0, j)),
        ],
        out_specs=pl.BlockSpec((tm, tn), lambda i, j: (i, j)),
        out_shape=jax.ShapeDtypeStruct((t, d), MXU_DTYPE),
        scratch_shapes=[pltpu.VMEM((tm, gw), MXU_DTYPE)],
        compiler_params=_params("parallel", "arbitrary"),
        name="merge",
    )(g, *outs, *lses, z, z, w_rnn, w_att)


def _oproj_kernel(x_ref, mx_ref, wo_ref, gain_ref, h_ref, hn_ref, hnt_ref):
    h = x_ref[...] + jnp.dot(mx_ref[...], wo_ref[...], preferred_element_type=F32)
    h_ref[...] = h
    hn = _rms(h, gain_ref[...])
    hn_ref[...] = hn.astype(hn_ref.dtype)
    hnt_ref[...] = hn.T.astype(hnt_ref.dtype)


def _oproj(x, mixed, w_o, gain, *, tm):
    t, d = x.shape
    row = lambda i: (i, 0)
    return pl.pallas_call(
        _oproj_kernel,
        grid=(t // tm,),
        in_specs=[
            pl.BlockSpec((tm, d), row),
            pl.BlockSpec((tm, d), row),
            pl.BlockSpec((d, d), lambda i: (0, 0)),
            pl.BlockSpec((1, d), lambda i: (0, 0)),
        ],
        out_specs=[pl.BlockSpec((tm, d), row), pl.BlockSpec((tm, d), row),
                   pl.BlockSpec((d, tm), lambda i: (0, i))],
        out_shape=[jax.ShapeDtypeStruct((t, d), F32), jax.ShapeDtypeStruct((t, d), MXU_DTYPE),
                   jax.ShapeDtypeStruct((d, t), MXU_DTYPE)],
        compiler_params=_params("parallel"),
        name="oproj",
    )(x, mixed, w_o, gain)


def _all_sublanes(x, op):
    for sh in (4, 2, 1):
        x = op(x, pltpu.roll(x, sh, 0))
    return x


def _take_ranked(vals, order, count):
    rest = vals
    rank = jnp.full(vals.shape, float(count), F32)
    tops, firsts = [], []
    for k in range(count):
        m = _all_sublanes(jnp.max(rest, axis=0), jnp.maximum)
        first = _all_sublanes(jnp.min(jnp.where(rest == m[None], order, POS_INF), axis=0),
                              jnp.minimum)
        hit = order == first[None]
        rest = jnp.where(hit, NEG_INF, rest)
        rank = jnp.where(hit, float(k), rank)
        tops.append(m)
        firsts.append(first)
    return tops, firsts, rank


def _sort_pairs(n):
    pairs, p = [], 1
    while p < n:
        k = p
        while k >= 1:
            for j in range(k % p, n - k, 2 * k):
                for i in range(min(k, n - j - k)):
                    if (i + j) // (2 * p) == (i + j + k) // (2 * p):
                        pairs.append((i + j, i + j + k))
            k //= 2
        p *= 2
    return tuple(pairs)


def _bitonic_pairs(n):
    pairs, dist = [], n // 2
    while dist >= 1:
        pairs += [(i, i + dist) for i in range(n) if (i // dist) % 2 == 0]
        dist //= 2
    return tuple(pairs)


_SORT_TOPK = _sort_pairs(PEER_TOPK)
_MERGE_TOPK = _bitonic_pairs(PEER_TOPK)


def _top_values(vals):
    nt = PEER_TOPK
    v = [vals[k] for k in range(nt)]

    def exchange(pairs):
        for a, b in pairs:
            v[a], v[b] = jnp.maximum(v[a], v[b]), jnp.minimum(v[a], v[b])

    exchange(_SORT_TOPK)
    for sh in (4, 2, 1):
        other = [pltpu.roll(x, sh, 0) for x in v]
        v = [jnp.maximum(v[k], other[nt - 1 - k]) for k in range(nt)]
        exchange(_MERGE_TOPK)
    return v


def _route_lanes_by_value(s0, s1):
    nt = PEER_TOPK
    nk, ln = s0.shape
    nv = nk // SUBLANES
    assert nv == nt == 2 * SUBLANES
    s0 = s0.reshape(nv, SUBLANES, ln)
    s1 = s1.reshape(nv, SUBLANES, ln)
    top0 = _top_values(s0)
    top1 = _top_values(s1)
    sub = lax.broadcasted_iota(jnp.int32, (SUBLANES, ln), 0)

    def on_sublanes(rows):
        out = rows[0]
        for b in range(1, SUBLANES):
            out = jnp.where(sub == b, rows[b], out)
        return out

    def count(mask):
        return _all_sublanes(jnp.where(mask, 1.0, 0.0), jnp.add)

    t1lo, t1hi = on_sublanes(top1[:SUBLANES]), on_sublanes(top1[SUBLANES:])
    t0hi = on_sublanes(top0[SUBLANES:])
    pieces = [top0[0] + t1lo, top0[0] + t1hi]
    for a in range(1, SUBLANES):
        pieces.append(jnp.where(sub < nt // (a + 1), top0[a] + t1lo, NEG_INF))
    pieces.append(t0hi + top1[0])
    pad = jnp.full((SUBLANES, ln), NEG_INF, F32)
    best = _top_values(jnp.stack(pieces + [pad] * (nt - len(pieces))))
    cut = best[nt - 1]
    zsum = jnp.ones((SUBLANES, ln), F32)
    for k in range(1, nt):
        zsum = zsum + jnp.exp(best[k] - best[0])

    counts = [count(pieces[0] >= cut) + count(pieces[1] >= cut)]
    counts += [count(pieces[a + 1] >= cut) for a in range(1, SUBLANES)]
    total = count(pieces[SUBLANES + 1] >= cut)
    th_vals = []
    for a in range(SUBLANES):
        total = total + counts[a]
        tv = jnp.full((SUBLANES, ln), POS_INF, F32)
        for m in range(nt // (a + 1)):
            tv = jnp.where(counts[a] == float(m + 1), top1[m], tv)
        th_vals.append(tv)

    member0 = s0 >= top0[nt - 1][None]
    member1 = s1 >= top1[nt - 1][None]
    th = jnp.where(member0 & (s0 + top1[0][None] >= cut[None]), top1[0][None], POS_INF)
    for a in range(SUBLANES):
        th = jnp.where(s0 == top0[a][None], th_vals[a][None], th)
    c = jnp.exp(s0 - top0[0][None]) / zsum[None]
    r1 = jnp.where(member1, s1, NEG_INF)
    e1 = jnp.exp(s1 - top1[0][None])

    gap = top0[0] - top0[1]
    for t in (top0, top1):
        for k in range(nt - 1):
            gap = jnp.minimum(gap, t[k] - t[k + 1])
    n0 = _all_sublanes(jnp.sum(jnp.where(member0, 1.0, 0.0), axis=0), jnp.add)
    n1 = _all_sublanes(jnp.sum(jnp.where(member1, 1.0, 0.0), axis=0), jnp.add)
    tie = (jnp.where(gap <= 0.0, 1.0, 0.0) + jnp.abs(n0 - float(nt)) + jnp.abs(n1 - float(nt))
           + jnp.abs(total - float(nt)))
    return tuple(x.reshape(nk, ln) for x in (th, c, r1, e1)) + (tie,)


def _route_lanes(s0, s1):
    th, c, r1, e1, tie = _route_lanes_by_value(s0, s1)
    return lax.cond(jnp.max(tie) > 0.0, lambda: _route_lanes_by_rank(s0, s1),
                    lambda: (th, c, r1, e1))


def _route_lanes_by_rank(s0, s1):
    nt = PEER_TOPK
    nk, ln = s0.shape
    nv = nk // SUBLANES
    s0 = s0.reshape(nv, SUBLANES, ln)
    s1 = s1.reshape(nv, SUBLANES, ln)
    key = (lax.broadcasted_iota(jnp.int32, (nv, SUBLANES, ln), 0) * SUBLANES
           + lax.broadcasted_iota(jnp.int32, (nv, SUBLANES, ln), 1)).astype(F32)
    top0, _, rank0 = _take_ranked(s0, key, nt)
    top1, _, rank1 = _take_ranked(s1, key, nt)

    sub = lax.broadcasted_iota(jnp.int32, (SUBLANES, ln), 0)
    subf = sub.astype(F32)

    def on_sublanes(rows):
        out = rows[0]
        for b in range(1, SUBLANES):
            out = jnp.where(sub == b, rows[b], out)
        return out

    assert nt == 2 * SUBLANES
    t1lo, t1hi = on_sublanes(top1[:SUBLANES]), on_sublanes(top1[SUBLANES:])
    t0hi = on_sublanes(top0[SUBLANES:])
    vals = [top0[0] + t1lo, top0[0] + t1hi]
    poss = [subf, subf + SUBLANES]
    for a in range(1, SUBLANES):
        vals.append(jnp.where(sub < nt // (a + 1), top0[a] + t1lo, NEG_INF))
        poss.append(subf + a * nt)
    vals.append(t0hi + top1[0])
    poss.append((subf + SUBLANES) * nt)
    best, bpos, _ = _take_ranked(jnp.stack(vals), jnp.stack(poss), nt)

    zsum = jnp.zeros((SUBLANES, ln), F32)
    cnt_lo = jnp.zeros((SUBLANES, ln), F32)
    cnt_hi = jnp.zeros((SUBLANES, ln), F32)
    for k in range(nt):
        zsum = zsum + jnp.exp(best[k] - best[0])
        a_k = jnp.floor(bpos[k] * (1.0 / nt))
        cnt_lo = cnt_lo + jnp.where(subf == a_k, 1.0, 0.0)
        cnt_hi = cnt_hi + jnp.where(subf + SUBLANES == a_k, 1.0, 0.0)

    th = jnp.full((nv, SUBLANES, ln), POS_INF, F32)
    for a in range(nt):
        cnt = cnt_lo if a < SUBLANES else cnt_hi
        n_a = jnp.broadcast_to(cnt[a % SUBLANES:a % SUBLANES + 1], (SUBLANES, ln))
        th = jnp.where(rank0 == float(a), (1.0 - n_a)[None], th)
    c = jnp.exp(s0 - top0[0][None]) / zsum[None]
    r1 = jnp.where(rank1 < float(nt), -rank1, NEG_INF)
    e1 = jnp.exp(s1 - top1[0][None])
    return tuple(x.reshape(nk, ln) for x in (th, c, r1, e1))


def _route_kernel(hn_ref, wq_ref, sk_ref, th_ref, c_ref, r1_ref, e1_ref, q_sc):
    q = jnp.dot(hn_ref[...], wq_ref[...], preferred_element_type=F32)
    tm = hn_ref.shape[0]
    d_half = sk_ref.shape[2]
    for hp in range(2 * PEER_HEADS):
        q_sc[hp] = q[:, hp * d_half:(hp + 1) * d_half]

    def head(h, _):
        def scores(half):
            return lax.dot_general(sk_ref[2 * h + half], q_sc[2 * h + half],
                                   (((1,), (1,)), ((), ())),
                                   preferred_element_type=F32)

        s0, s1 = scores(0), scores(1)
        for lc in range(tm // LANES):
            ls = slice(lc * LANES, (lc + 1) * LANES)
            th, c, r1, e1 = _route_lanes(s0[:, ls], s1[:, ls])
            th_ref[h, :, ls] = th
            c_ref[h, :, ls] = c
            r1_ref[h, :, ls] = r1
            e1_ref[h, :, ls] = e1
        return 0

    lax.fori_loop(0, PEER_HEADS, head, 0)


def _route(hn, w_query, sub_keys, *, tm):
    t, d = hn.shape
    dq = w_query.shape[1]
    slab = jax.ShapeDtypeStruct((PEER_HEADS, N_KEYS, t), F32)
    slab_spec = pl.BlockSpec((PEER_HEADS, N_KEYS, tm), lambda i: (0, 0, i))
    return pl.pallas_call(
        _route_kernel,
        grid=(t // tm,),
        in_specs=[
            pl.BlockSpec((tm, d), lambda i: (i, 0)),
            pl.BlockSpec((d, dq), lambda i: (0, 0)),
            pl.BlockSpec(sub_keys.shape, lambda i: (0, 0, 0)),
        ],
        out_specs=[slab_spec] * 4,
        out_shape=[slab] * 4,
        scratch_shapes=[pltpu.VMEM((2 * PEER_HEADS, tm, sub_keys.shape[2]), F32)],
        compiler_params=_params("parallel"),
        name="peer_route",
    )(hn, w_query, sub_keys)


PEER_ROW_CHUNK = 32


def _peer_kernel(x_ref, u_ref, vt_ref, th_ref, c_ref, r1_ref, e1_ref, y_ref, act_sc, gate_sc,
                 coef_sc):
    e = pl.program_id(1)
    tm = x_ref.shape[1]
    ib = th_ref.shape[1]
    hk = ib // 2
    hr = hk * N_KEYS
    assert tm % LANES == 0 and N_KEYS % PEER_ROW_CHUNK == 0 and ib % 2 == 0

    @pl.when(e == 0)
    def _():
        y_ref[...] = jnp.zeros_like(y_ref)

    def first_matmul(half):
        rows = slice(half * hr, (half + 1) * hr)
        act_sc[rows, :] = jnp.dot(u_ref[rows, :], x_ref[...], preferred_element_type=F32)

    def routing_weights(half):
        for lc in range(tm // LANES):
            ls = slice(lc * LANES, (lc + 1) * LANES)
            for jc in range(N_KEYS // PEER_ROW_CHUNK):
                js = slice(jc * PEER_ROW_CHUNK, (jc + 1) * PEER_ROW_CHUNK)
                gates = [jnp.zeros((PEER_ROW_CHUNK, LANES), F32) for _ in range(hk)]
                for h in range(PEER_HEADS):
                    r1 = r1_ref[h, js, ls]
                    e1 = e1_ref[h, js, ls]
                    for k in range(hk):
                        ii = half * hk + k
                        th = th_ref[h, ii:ii + 1, ls]
                        ch = c_ref[h, ii:ii + 1, ls]
                        gates[k] = gates[k] + jnp.where(r1 >= th, e1 * ch, 0.0)
                for k in range(hk):
                    r0 = (half * hk + k) * N_KEYS + jc * PEER_ROW_CHUNK
                    gate_sc[r0:r0 + PEER_ROW_CHUNK, ls] = gates[k]

    def second_matmul(half):
        rows = slice(half * hr, (half + 1) * hr)
        coef_sc[rows, :] = (_gelu(act_sc[rows, :]) * gate_sc[rows, :]).astype(coef_sc.dtype)
        y_ref[...] += jnp.dot(vt_ref[:, rows], coef_sc[rows, :], preferred_element_type=F32)

    @pl.when(e >= 0)
    def _():
        first_matmul(0)
        routing_weights(0)

    @pl.when(e >= -1)
    def _():
        first_matmul(1)
        routing_weights(1)
        second_matmul(0)

    @pl.when(e >= -2)
    def _():
        second_matmul(1)


MXU_TILE = 256
PEER_CHAIN_ROWS = 16
PEER_CHAIN_KEYS = 4
PEER_CHAIN_BLOCK_ROWS = 256
CHAIN_HEAD_ROWS = 16


def _zero_after(x):
    bits = pltpu.bitcast(x, jnp.uint32)
    z = lax.shift_right_logical(lax.shift_right_logical(bits, jnp.uint32(16)), jnp.uint32(16))
    return z[0:1, :].astype(F32).astype(MXU_DTYPE)


def _chained(blocks, pieces):
    nb, npc = len(blocks), len(pieces)
    out = []
    for b in range(nb):
        dep = None
        for p in range(b * npc // nb, (b + 1) * npc // nb):
            v = pieces[p]()
            dep = v if dep is None else dep + v
        blk = blocks[b]()
        if dep is not None:
            zero = _zero_after(dep)
            zero = jnp.concatenate([zero] * (blk.shape[1] // LANES), axis=1)
            blk = jnp.concatenate([blk[:CHAIN_HEAD_ROWS] + zero, blk[CHAIN_HEAD_ROWS:]], axis=0)
        out.append(blk)
    return out


def _peer_chain_kernel(x_ref, u_ref, vt_ref, th_ref, c_ref, r1_ref, e1_ref, y_ref, gate_sc):
    e = pl.program_id(1)
    d, tm = x_ref.shape
    te = u_ref.shape[0]
    ib = th_ref.shape[1]
    nkt = d // MXU_TILE
    nj = N_KEYS // PEER_CHAIN_ROWS
    nchunks = (tm // LANES) * nj
    assert ib % PEER_CHAIN_KEYS == 0 and te % PEER_CHAIN_BLOCK_ROWS == 0

    @pl.when(e == 0)
    def _():
        y_ref[...] = jnp.zeros_like(y_ref)

    def gate_piece(ci, keys):
        def run():
            ls = slice((ci // nj) * LANES, (ci // nj + 1) * LANES)
            js = slice((ci % nj) * PEER_CHAIN_ROWS, (ci % nj + 1) * PEER_CHAIN_ROWS)
            gates = {ii: jnp.zeros((PEER_CHAIN_ROWS, LANES), F32) for ii in keys}
            for h in range(PEER_HEADS):
                r1 = r1_ref[h, js, ls]
                e1 = e1_ref[h, js, ls]
                for ii in keys:
                    th = th_ref[h, ii:ii + 1, ls]
                    ch = c_ref[h, ii:ii + 1, ls]
                    gates[ii] = gates[ii] + jnp.where(r1 >= th, e1 * ch, 0.0)
            total = None
            for ii in keys:
                r0 = ii * N_KEYS + (ci % nj) * PEER_CHAIN_ROWS
                gate_sc[r0:r0 + PEER_CHAIN_ROWS, ls] = gates[ii]
                total = gates[ii] if total is None else total + gates[ii]
            return jnp.sum(total.reshape(PEER_CHAIN_ROWS // SUBLANES, SUBLANES, LANES), axis=0)
        return run

    pieces = [gate_piece(ci, tuple(range(k0, k0 + PEER_CHAIN_KEYS)))
              for ci in range(nchunks) for k0 in range(0, ib, PEER_CHAIN_KEYS)]

    rows_per = PEER_CHAIN_BLOCK_ROWS
    nrb = te // rows_per
    u_blocks = [(lambda kt=kt, rb=rb: u_ref[rb * rows_per:(rb + 1) * rows_per,
                                            kt * MXU_TILE:(kt + 1) * MXU_TILE])
                for kt in range(nkt) for rb in range(nrb)]
    blocks = _chained(u_blocks, pieces)
    lhs = jnp.concatenate([jnp.concatenate(blocks[kt * nrb:(kt + 1) * nrb], axis=0)
                           for kt in range(nkt)], axis=1)
    act = jnp.dot(lhs, x_ref[...], preferred_element_type=F32)
    coef = (_gelu(act) * gate_sc[...]).astype(MXU_DTYPE)
    y_ref[...] += jnp.dot(vt_ref[...], coef, preferred_element_type=F32)


def _peer(hnt, u, vt, th, c, r1, e1, *, tm, ib):
    d, t = hnt.shape
    ne = u.shape[0]
    te = ib * N_KEYS
    slab_spec = pl.BlockSpec((PEER_HEADS, N_KEYS, tm), lambda i, e: (0, 0, i))
    first_spec = pl.BlockSpec((PEER_HEADS, ib, tm), lambda i, e: (0, e, i))
    if tm % (4 * LANES) == 0:
        return pl.pallas_call(
            _peer_chain_kernel,
            grid=(t // tm, ne // te),
            in_specs=[
                pl.BlockSpec((d, tm), lambda i, e: (0, i)),
                pl.BlockSpec((te, d), lambda i, e: (e, 0)),
                pl.BlockSpec((d, te), lambda i, e: (0, e)),
                first_spec, first_spec, slab_spec, slab_spec,
            ],
            out_specs=pl.BlockSpec((d, tm), lambda i, e: (0, i)),
            out_shape=jax.ShapeDtypeStruct((d, t), F32),
            scratch_shapes=[pltpu.VMEM((te, tm), F32)],
            compiler_params=_params("parallel", "arbitrary"),
            name="peer_experts_chain",
        )(hnt, u, vt, th, c, r1, e1)
    return pl.pallas_call(
        _peer_kernel,
        grid=(t // tm, ne // te),
        in_specs=[
            pl.BlockSpec((d, tm), lambda i, e: (0, i)),
            pl.BlockSpec((te, d), lambda i, e: (e, 0)),
            pl.BlockSpec((d, te), lambda i, e: (0, e)),
            first_spec, first_spec, slab_spec, slab_spec,
        ],
        out_specs=pl.BlockSpec((d, tm), lambda i, e: (0, i)),
        out_shape=jax.ShapeDtypeStruct((d, t), F32),
        scratch_shapes=[pltpu.VMEM((te, tm), F32), pltpu.VMEM((te, tm), F32),
                        pltpu.VMEM((te, tm), MXU_DTYPE)],
        compiler_params=_params("parallel", "arbitrary"),
        name="peer_experts",
    )(hnt, u, vt, th, c, r1, e1)


def _final_kernel(h_ref, yt_ref, gain_ref, out_ref):
    out_ref[...] = _rms(h_ref[...] + yt_ref[...].T, gain_ref[...])


def _final(h, yt, gain, *, tm):
    t, d = h.shape
    return pl.pallas_call(
        _final_kernel,
        grid=(t // tm,),
        in_specs=[pl.BlockSpec((tm, d), lambda i: (i, 0)), pl.BlockSpec((d, tm), lambda i: (0, i)),
                  pl.BlockSpec((1, d), lambda i: (0, 0))],
        out_specs=pl.BlockSpec((tm, d), lambda i: (i, 0)),
        out_shape=jax.ShapeDtypeStruct((t, d), F32),
        compiler_params=_params("parallel"),
        name="final_norm",
    )(h, yt, gain)


TOKEN_TILES = {"inproj": 1024, "rnn": 256, "merge": 512, "oproj": 512, "route": 256,
               "peer": 512, "final": 512}


def _tile(n, stage):
    return min(n, TOKEN_TILES[stage])


def _token_stages(x2, z, g, outs, lses, w, cols):
    t = x2.shape[0]
    mixed = _merge(g, outs, lses, z, w["w_rnn_out"], w["w_att_out"], col_ga=cols["ga"],
                   col_gb=cols["gb"], tm=_tile(t, "merge"), tn=HEADS * HEAD_DIM)
    h, hn, hnt = _oproj(x2, mixed, w["w_o"], w["norm_ffn"], tm=_tile(t, "oproj"))
    th, c, r1, e1 = _route(hn, w["w_query"], w["sub_keys"], tm=_tile(t, "route"))
    yt = _peer(hnt, w["expert_u"], w["expert_vt"], th, c, r1, e1, tm=_tile(t, "peer"),
               ib=SUBLANES)
    return _final(h, yt, w["norm_final"], tm=_tile(t, "final"))


def kernel(x_prompt, x_sample, state_conv, state_h, cache_kv_w128, cache_kv_w512, cache_kv_w2048,
           norm_mix, w_in, conv_w, conv_b, w_gate_x, b_gate_x, w_gate_a, b_gate_a, lru_lambda,
           w_rnn_out, w_att_out, w_o, norm_ffn, w_query, sub_keys, expert_u, expert_v, norm_final):
    depth = w_in.shape[0]
    assert depth == 1 and x_prompt.shape[0] == 1
    batch, seq, d = x_prompt.shape
    nb, steps, _ = x_sample.shape
    d_in = w_in.shape[2]
    gw = HEADS * HEAD_DIM
    d_qkv = N_GROUPS * gw
    assert d_in == 4 * d + 3 * d_qkv and d % gw == 0
    caches = (cache_kv_w128, cache_kv_w512, cache_kv_w2048)
    col_q, col_k, col_v = 2 * d // gw, (2 * d + d_qkv) // gw, (2 * d + 2 * d_qkv) // gw
    cols = {"ga": (2 * d + 3 * d_qkv) // gw, "gb": (3 * d + 3 * d_qkv) // gw}

    mx = lambda a: a.astype(MXU_DTYPE)
    row = lambda a: a.reshape(1, -1).astype(F32)
    w = {
        "w_in": mx(w_in[0]), "w_rnn_out": mx(w_rnn_out[0]), "w_att_out": mx(w_att_out[0]),
        "w_o": mx(w_o[0]), "w_query": mx(w_query[0]),
        "sub_keys": sub_keys[0].reshape(2 * PEER_HEADS, N_KEYS, -1),
        "expert_u": mx(expert_u[0]), "expert_vt": mx(expert_v[0]).T,
        "norm_ffn": row(norm_ffn[0]), "norm_final": row(norm_final),
    }
    rnn_w = (conv_w[0], row(conv_b[0]), mx(w_gate_x[0]), row(b_gate_x[0]), mx(w_gate_a[0]),
             row(b_gate_a[0]), row(lru_lambda[0]))
    gain_mix = row(norm_mix[0])

    xp = x_prompt.reshape(seq, d)
    zp = _inproj(xp, gain_mix, w["w_in"], tm=_tile(seq, "inproj"), tn=gw)
    g_p, past_p, h_p = _rnn_prompt(zp, *rnn_w, jnp.zeros((SUBLANES, d), F32),
                                   jnp.zeros((1, d), F32), d=d, tt=_tile(seq, "rnn"))
    outs, lses, kv_p = [], [], []
    for gi, (win, dil) in enumerate(ATT_GROUPS):
        o, lse = _attn_prompt(zp, gi, dil, col_q=col_q, col_k=col_k, col_v=col_v)
        outs.append(o)
        lses.append(lse)
        keep = min(win, seq)
        kcol = (col_k + gi) * gw
        vcol = (col_v + gi) * gw
        kv = jnp.stack([zp[seq - keep:, kcol:kcol + gw], zp[seq - keep:, vcol:vcol + gw]], axis=1)
        kv_p.append(kv.reshape(1, 1, keep, 2, HEADS, HEAD_DIM))
    y_p = _token_stages(xp, zp, g_p, outs, lses, w, cols).reshape(batch, seq, d)
    conv_p = past_p[SUBLANES - (CONV_W - 1):].reshape(1, 1, CONV_W - 1, d)
    h_p = h_p.reshape(1, 1, d)

    ts = nb * steps
    xs = x_sample.reshape(ts, d)
    zs = _inproj(xs, gain_mix, w["w_in"], tm=ts, tn=gw)
    g_s, conv_s, h_s = _rnn_sample(zs.reshape(nb, steps * d_in), *rnn_w,
                                   state_conv[0].reshape(nb, (CONV_W - 1) * d), state_h[0],
                                   d=d, steps=steps)
    g_s = g_s.reshape(ts, d)
    qkv = zs[:, 2 * d:2 * d + 3 * d_qkv].reshape(nb, steps, 3, N_GROUPS, HEADS, HEAD_DIM)
    outs, lses, kv_s = [], [], []
    for gi, (win, dil) in enumerate(ATT_GROUPS):
        cache = caches[gi][0]
        o, lse, new_cache = _attn_sample(qkv[:, :, 0, gi], qkv[:, :, 1, gi], qkv[:, :, 2, gi], cache,
                                         dil, rch=min(cache.shape[1], ATTN_SAMPLE_CHUNK_ROWS))
        outs.append(o.reshape(ts, gw))
        lse = lse[..., 0].reshape(ts, HEADS)
        lses.append(jnp.pad(lse, ((0, 0), (0, LANES - HEADS))))
        kv_s.append(new_cache[None])
    y_s = _token_stages(xs, zs, g_s, outs, lses, w, cols).reshape(nb, steps, d)
    conv_s = conv_s.reshape(1, nb, CONV_W - 1, d)
    h_s = h_s.reshape(1, nb, d)

    return (y_p, y_s, conv_p, h_p, kv_p[0], kv_p[1], kv_p[2],
            conv_s, h_s, kv_s[0], kv_s[1], kv_s[2])
```

```python
import functools
import math

import jax
import jax.numpy as jnp
from jax import lax
from jax.experimental import pallas as pl
from jax.experimental.pallas import tpu as pltpu

F32 = jnp.float32
MXU_DTYPE = jnp.bfloat16

RNN_BLOCKS = 8
CONV_W = 4
LRU_C = 8.0
ATT_GROUPS = ((128, 1), (512, 4), (2048, 16))
N_GROUPS = len(ATT_GROUPS)
HEADS = 8
HEAD_DIM = 128
BAND = 128
PEER_HEADS = 8
N_KEYS = 128
PEER_TOPK = 16
NORM_EPS = 1e-6

LANES = 128
SUBLANES = 8
VMEM_LIMIT_BYTES = 56 * 1024 * 1024

NEG_INF = float("-inf")
POS_INF = float("inf")


def _params(*sem):
    return pltpu.CompilerParams(dimension_semantics=sem, vmem_limit_bytes=VMEM_LIMIT_BYTES)


def _gelu(x):
    return 0.5 * x * (1.0 + lax.erf(x * math.sqrt(0.5)))


def _rms(x, gain):
    ms = jnp.mean(x * x, axis=-1, keepdims=True)
    return x * lax.rsqrt(ms + NORM_EPS) * gain


def _inproj_kernel(x_ref, g_ref, w_ref, z_ref, xn_sc):
    @pl.when(pl.program_id(1) == 0)
    def _():
        xn_sc[...] = _rms(x_ref[...], g_ref[...]).astype(MXU_DTYPE)

    z_ref[...] = jnp.dot(xn_sc[...], w_ref[...], preferred_element_type=F32)


def _inproj(x, gain, w, *, tm, tn):
    m, k = x.shape
    n = w.shape[1]
    return pl.pallas_call(
        _inproj_kernel,
        grid=(m // tm, n // tn),
        in_specs=[
            pl.BlockSpec((tm, k), lambda i, j: (i, 0)),
            pl.BlockSpec((1, k), lambda i, j: (0, 0)),
            pl.BlockSpec((k, tn), lambda i, j: (0, j)),
        ],
        out_specs=pl.BlockSpec((tm, tn), lambda i, j: (i, j)),
        out_shape=jax.ShapeDtypeStruct((m, n), F32),
        scratch_shapes=[pltpu.VMEM((tm, k), MXU_DTYPE)],
        compiler_params=_params("parallel", "arbitrary"),
        name="inproj",
    )(x, gain, w)


def _lru_gates(xc, wgx, bgx, wga, bga, log_sig):
    xb = xc.astype(MXU_DTYPE)
    gx = jax.nn.sigmoid(jnp.dot(xb, wgx, preferred_element_type=F32) + bgx)
    ga = jax.nn.sigmoid(jnp.dot(xb, wga, preferred_element_type=F32) + bga)
    a = jnp.exp(LRU_C * ga * log_sig)
    b = jnp.sqrt(1.0 - a * a) * gx * xc
    return a, b


def _log_sigmoid(x):
    return -(jnp.maximum(-x, 0.0) + jnp.log1p(jnp.exp(-jnp.abs(x))))


def _rnn_prompt_kernel(xr_ref, yr_ref, cw_ref, cb_ref, wgx_ref, bgx_ref, wga_ref, bga_ref, lam_ref,
                       p0_ref, h0_ref, g_ref, pn_ref, hn_ref, p_sc, h_sc, a_sc, b_sc):
    tt, d = xr_ref.shape
    bw = d // RNN_BLOCKS

    @pl.when(pl.program_id(0) == 0)
    def _():
        p_sc[...] = p0_ref[...]
        h_sc[...] = h0_ref[...]

    x = xr_ref[...]
    past = p_sc[...]
    row8 = lax.broadcasted_iota(jnp.int32, (SUBLANES, d), 0)
    xc = cb_ref[...] + cw_ref[0:1, :] * x
    for j in range(1, CONV_W):
        xs = pltpu.roll(x, j, 0)
        head = jnp.where(row8 < j, pltpu.roll(past, j, 0), xs[:SUBLANES])
        xs = jnp.concatenate([head, xs[SUBLANES:]], axis=0)
        xc = xc + cw_ref[j:j + 1, :] * xs
    p_sc[...] = x[tt - SUBLANES:]
    pn_ref[...] = x[tt - SUBLANES:]

    log_sig = _log_sigmoid(lam_ref[...])
    rowm = lax.broadcasted_iota(jnp.int32, (tt, bw), 0) & (SUBLANES - 1)
    for n in range(RNN_BLOCKS):
        sl = slice(n * bw, (n + 1) * bw)
        a, b = _lru_gates(xc[:, sl], wgx_ref[n], bgx_ref[:, sl], wga_ref[n], bga_ref[:, sl],
                          log_sig[:, sl])
        for sh in (1, 2, 4):
            ok = rowm >= sh
            a_sh = pltpu.roll(a, sh, 0)
            b_sh = pltpu.roll(b, sh, 0)
            b = jnp.where(ok, a * b_sh + b, b)
            a = jnp.where(ok, a * a_sh, a)
        a_sc[:, sl] = a
        b_sc[:, sl] = b

    def group(gi, h):
        rows = pl.ds(pl.multiple_of(gi * SUBLANES, SUBLANES), SUBLANES)
        hh = a_sc[rows, :] * h + b_sc[rows, :]
        b_sc[rows, :] = hh
        return hh[SUBLANES - 1:SUBLANES, :]

    h_last = lax.fori_loop(0, tt // SUBLANES, group, h_sc[...])
    h_sc[...] = h_last
    hn_ref[...] = h_last
    g_ref[...] = (_gelu(yr_ref[...]) * b_sc[...]).astype(g_ref.dtype)


def _rnn_prompt(z, conv_w, conv_b, wgx, bgx, wga, bga, lam, past8, h0, *, d, tt):
    t = z.shape[0]
    full2 = lambda i: (0, 0)
    full3 = lambda i: (0, 0, 0)
    bw = d // RNN_BLOCKS
    return pl.pallas_call(
        _rnn_prompt_kernel,
        grid=(t // tt,),
        in_specs=[
            pl.BlockSpec((tt, d), lambda i: (i, 0)),
            pl.BlockSpec((tt, d), lambda i: (i, 1)),
            pl.BlockSpec((CONV_W, d), full2),
            pl.BlockSpec((1, d), full2),
            pl.BlockSpec((RNN_BLOCKS, bw, bw), full3),
            pl.BlockSpec((1, d), full2),
            pl.BlockSpec((RNN_BLOCKS, bw, bw), full3),
            pl.BlockSpec((1, d), full2),
            pl.BlockSpec((1, d), full2),
            pl.BlockSpec((SUBLANES, d), full2),
            pl.BlockSpec((1, d), full2),
        ],
        out_specs=[
            pl.BlockSpec((tt, d), lambda i: (i, 0)),
            pl.BlockSpec((SUBLANES, d), full2),
            pl.BlockSpec((1, d), full2),
        ],
        out_shape=[
            jax.ShapeDtypeStruct((t, d), MXU_DTYPE),
            jax.ShapeDtypeStruct((SUBLANES, d), F32),
            jax.ShapeDtypeStruct((1, d), F32),
        ],
        scratch_shapes=[pltpu.VMEM((SUBLANES, d), F32), pltpu.VMEM((1, d), F32),
                        pltpu.VMEM((tt, d), F32), pltpu.VMEM((tt, d), F32)],
        compiler_params=_params("arbitrary"),
        name="rnn_prompt",
    )(z, z, conv_w, conv_b, wgx, bgx, wga, bga, lam, past8, h0)


def _rnn_sample_kernel(z_ref, cw_ref, cb_ref, wgx_ref, bgx_ref, wga_ref, bga_ref, lam_ref,
                       cs_ref, h0_ref, g_ref, cn_ref, hn_ref, *, d, d_in, steps):
    bw = d // RNN_BLOCKS
    log_sig = _log_sigmoid(lam_ref[...])
    hist = [cs_ref[:, j * d:(j + 1) * d] for j in range(CONV_W - 1)]
    h = [h0_ref[:, n * bw:(n + 1) * bw] for n in range(RNN_BLOCKS)]
    for s in range(steps):
        x = z_ref[:, s * d_in:s * d_in + d]
        hist.append(x)
        xc = cb_ref[...]
        for j in range(CONV_W):
            xc = xc + cw_ref[j:j + 1, :] * hist[-1 - j]
        for n in range(RNN_BLOCKS):
            sl = slice(n * bw, (n + 1) * bw)
            a, b = _lru_gates(xc[:, sl], wgx_ref[n], bgx_ref[:, sl], wga_ref[n], bga_ref[:, sl],
                              log_sig[:, sl])
            h[n] = a * h[n] + b
            y = z_ref[:, s * d_in + d + n * bw:s * d_in + d + (n + 1) * bw]
            g_ref[:, s * d + n * bw:s * d + (n + 1) * bw] = (_gelu(y) * h[n]).astype(g_ref.dtype)
    for j in range(CONV_W - 1):
        cn_ref[:, j * d:(j + 1) * d] = hist[len(hist) - (CONV_W - 1) + j]
    for n in range(RNN_BLOCKS):
        hn_ref[:, n * bw:(n + 1) * bw] = h[n]


def _rnn_sample(z2, conv_w, conv_b, wgx, bgx, wga, bga, lam, conv_state, h0, *, d, steps):
    nb = z2.shape[0]
    d_in = z2.shape[1] // steps
    return pl.pallas_call(
        functools.partial(_rnn_sample_kernel, d=d, d_in=d_in, steps=steps),
        out_shape=[
            jax.ShapeDtypeStruct((nb, steps * d), MXU_DTYPE),
            jax.ShapeDtypeStruct((nb, (CONV_W - 1) * d), F32),
            jax.ShapeDtypeStruct((nb, d), F32),
        ],
        compiler_params=pltpu.CompilerParams(vmem_limit_bytes=VMEM_LIMIT_BYTES),
        name="rnn_sample",
    )(z2, conv_w, conv_b, wgx, bgx, wga, bga, lam, conv_state, h0)


def _attn_prompt_kernel(q_ref, kc_ref, kp_ref, vc_ref, vp_ref, o_ref, lse_ref, *, dil, hpb, nbs):
    nblk = pl.program_id(0)
    hchunk = pl.program_id(1)
    span = BAND * dil
    qi = lax.broadcasted_iota(jnp.int32, (BAND, 2 * BAND), 0)
    ki = lax.broadcasted_iota(jnp.int32, (BAND, 2 * BAND), 1)
    window = (ki >= qi) & (ki <= qi + BAND)
    first_key = jnp.where(nblk > 0, 0, BAND)
    lane = lax.broadcasted_iota(jnp.int32, (BAND, LANES), 1)
    scale = HEAD_DIM ** -0.5

    @pl.when(hchunk == 0)
    def _():
        lse_ref[...] = jnp.zeros_like(lse_ref)

    def band_rows(b, r):
        return pl.ds(b * span + r, BAND, stride=dil) if dil > 1 else slice(b * span, (b + 1) * span)

    for b in range(nbs):
        mask = window & (ki >= first_key) if b == 0 else window
        for r in range(dil):
            rows = band_rows(b, r)
            if b == 0:
                k_prev, v_prev, prows = kp_ref, vp_ref, band_rows(0, r)
            else:
                k_prev, v_prev, prows = kc_ref, vc_ref, band_rows(b - 1, r)
            lse_tile = lse_ref[rows, :]
            for hh in range(hpb):
                sl = slice(hh * HEAD_DIM, (hh + 1) * HEAD_DIM)
                qh = (q_ref[rows, sl] * scale).astype(MXU_DTYPE)
                kk = jnp.concatenate([k_prev[prows, sl], kc_ref[rows, sl]], axis=0).astype(MXU_DTYPE)
                vv = jnp.concatenate([v_prev[prows, sl], vc_ref[rows, sl]], axis=0).astype(MXU_DTYPE)
                s = lax.dot_general(qh, kk, (((1,), (1,)), ((), ())), preferred_element_type=F32)
                s = jnp.where(mask, s, NEG_INF)
                m = jnp.max(s, axis=-1, keepdims=True)
                p = jnp.exp(s - m)
                den = jnp.sum(p, axis=-1, keepdims=True)
                o = jnp.dot(p.astype(MXU_DTYPE), vv, preferred_element_type=F32)
                o_ref[rows, sl] = o / den
                lse_tile = jnp.where(lane == hchunk * hpb + hh, m + jnp.log(den), lse_tile)
            lse_ref[rows, :] = lse_tile


ATTN_SAMPLE_CHUNK_ROWS = 512
ATTN_BANDS_PER_STEP = 4
ATTN_MAX_BLOCK_ROWS = 4096


def _attn_prompt(z, gi, dil, *, col_q, col_k, col_v):
    t = z.shape[0]
    gw = HEADS * HEAD_DIM
    hpb = HEADS if dil == 1 else 1
    cw = hpb * HEAD_DIM
    span = BAND * dil
    nbs = math.gcd(ATTN_BANDS_PER_STEP, t // span)
    while nbs > 1 and nbs * span > ATTN_MAX_BLOCK_ROWS:
        nbs //= 2
    rb = nbs * span
    per = gw // cw
    cur = lambda c: (lambda n, h: (n, (c + gi) * per + h))
    prev = lambda c: (lambda n, h: (jnp.maximum(n * nbs - 1, 0), (c + gi) * per + h))
    return pl.pallas_call(
        functools.partial(_attn_prompt_kernel, dil=dil, hpb=hpb, nbs=nbs),
        grid=(t // rb, HEADS // hpb),
        in_specs=[
            pl.BlockSpec((rb, cw), cur(col_q)),
            pl.BlockSpec((rb, cw), cur(col_k)),
            pl.BlockSpec((span, cw), prev(col_k)),
            pl.BlockSpec((rb, cw), cur(col_v)),
            pl.BlockSpec((span, cw), prev(col_v)),
        ],
        out_specs=[
            pl.BlockSpec((rb, cw), lambda n, h: (n, h)),
            pl.BlockSpec((rb, LANES), lambda n, h: (n, 0)),
        ],
        out_shape=[
            jax.ShapeDtypeStruct((t, gw), F32),
            jax.ShapeDtypeStruct((t, LANES), F32),
        ],
        compiler_params=_params("parallel", "arbitrary"),
        name=f"attn_prompt_g{gi}",
    )(z, z, z, z, z)


def _attn_sample_kernel(q_ref, kn_ref, vn_ref, c_ref, nx_ref, o_ref, lse_ref, new_ref,
                        m_sc, den_sc, acc_sc, *, dil, steps):
    ci = pl.program_id(1)
    nchunks = pl.num_programs(1)
    rch = c_ref.shape[1]

    def attend(s, krows, vrows, valid, first):
        q = q_ref[0, s] * (HEAD_DIM ** -0.5)
        sc = jnp.sum(q[None] * krows, axis=-1, keepdims=True)
        if valid is not None:
            sc = jnp.where(valid, sc, NEG_INF)
        m_blk = jnp.max(sc, axis=0)
        if first:
            m_new = m_blk
            p = jnp.exp(sc - m_new[None])
            den = jnp.sum(p, axis=0)
            acc = jnp.sum(p * vrows, axis=0)
        else:
            m_old = m_sc[s][:, 0:1]
            m_new = jnp.maximum(m_old, m_blk)
            alpha = jnp.exp(m_old - m_new)
            p = jnp.exp(sc - m_new[None])
            den = alpha * den_sc[s][:, 0:1] + jnp.sum(p, axis=0)
            acc = alpha * acc_sc[s] + jnp.sum(p * vrows, axis=0)
        m_sc[s] = jnp.broadcast_to(m_new, (HEADS, HEAD_DIM))
        den_sc[s] = jnp.broadcast_to(den, (HEADS, HEAD_DIM))
        acc_sc[s] = acc

    @pl.when(ci == 0)
    def _():
        for s in range(steps):
            lo = 0 if dil == 1 else s
            attend(s, kn_ref[0, lo:s + 1], vn_ref[0, lo:s + 1], None, True)

    for s in range(steps):
        if dil == 1:
            ridx = lax.broadcasted_iota(jnp.int32, (rch, 1, 1), 0) + ci * rch
            attend(s, c_ref[0, :, 0], c_ref[0, :, 1], ridx >= s, False)
        else:
            n = rch // dil
            attend(s, c_ref[0, pl.ds(s, n, stride=dil), 0], c_ref[0, pl.ds(s, n, stride=dil), 1],
                   None, False)

    @pl.when(ci == nchunks - 1)
    def _():
        for s in range(steps):
            o_ref[0, s] = acc_sc[s] / den_sc[s]
            lse_ref[0, s] = m_sc[s] + jnp.log(den_sc[s])

    new_ref[0, 0:rch - steps] = c_ref[0, steps:rch]

    @pl.when(ci < nchunks - 1)
    def _():
        new_ref[0, rch - steps:rch] = nx_ref[0]

    @pl.when(ci == nchunks - 1)
    def _():
        new_ref[0, rch - steps:rch, 0] = kn_ref[0]
        new_ref[0, rch - steps:rch, 1] = vn_ref[0]


def _attn_sample(q, kn, vn, cache, dil, *, rch):
    nb, steps = q.shape[0], q.shape[1]
    lg = cache.shape[1]
    nchunks = lg // rch
    per = rch // steps
    last = lg // steps - 1
    small = pl.BlockSpec((1, steps, HEADS, HEAD_DIM), lambda b, c: (b, 0, 0, 0))
    big = pl.BlockSpec((1, rch, 2, HEADS, HEAD_DIM), lambda b, c: (b, c, 0, 0, 0))
    nxt = pl.BlockSpec((1, steps, 2, HEADS, HEAD_DIM),
                       lambda b, c: (b, jnp.minimum((c + 1) * per, last), 0, 0, 0))
    return pl.pallas_call(
        functools.partial(_attn_sample_kernel, dil=dil, steps=steps),
        grid=(nb, nchunks),
        in_specs=[small, small, small, big, nxt],
        out_specs=[small, small, big],
        out_shape=[
            jax.ShapeDtypeStruct(q.shape, F32),
            jax.ShapeDtypeStruct(q.shape, F32),
            jax.ShapeDtypeStruct(cache.shape, cache.dtype),
        ],
        scratch_shapes=[pltpu.VMEM((steps, HEADS, HEAD_DIM), F32)] * 3,
        compiler_params=_params("parallel", "arbitrary"),
        name=f"attn_sample_d{dil}",
    )(q, kn, vn, cache, cache)


def _merge_kernel(g_ref, o0_ref, o1_ref, o2_ref, l0_ref, l1_ref, l2_ref, ga_ref, gb_ref,
                  wr_ref, wa_ref, out_ref, att_sc):
    @pl.when(pl.program_id(1) == 0)
    def _():
        l0, l1, l2 = l0_ref[...], l1_ref[...], l2_ref[...]
        mx = jnp.maximum(jnp.maximum(l0, l1), l2)
        e0, e1, e2 = jnp.exp(l0 - mx), jnp.exp(l1 - mx), jnp.exp(l2 - mx)
        inv = 1.0 / (e0 + e1 + e2)
        w0, w1, w2 = e0 * inv, e1 * inv, e2 * inv
        for h in range(HEADS):
            sl = slice(h * HEAD_DIM, (h + 1) * HEAD_DIM)
            att = (w0[:, h:h + 1] * o0_ref[:, sl] + w1[:, h:h + 1] * o1_ref[:, sl]
                   + w2[:, h:h + 1] * o2_ref[:, sl])
            att_sc[:, sl] = att.astype(att_sc.dtype)

    rnn_out = jnp.dot(g_ref[...], wr_ref[...], preferred_element_type=F32)
    att_out = jnp.dot(att_sc[...], wa_ref[...], preferred_element_type=F32)
    mixed = jax.nn.sigmoid(ga_ref[...]) * rnn_out + jax.nn.sigmoid(gb_ref[...]) * att_out
    out_ref[...] = mixed.astype(out_ref.dtype)


def _merge(g, outs, lses, z, w_rnn, w_att, *, col_ga, col_gb, tm, tn):
    t, d = g.shape
    gw = HEADS * HEAD_DIM
    nj = d // tn
    row = lambda i, j: (i, 0)
    return pl.pallas_call(
        _merge_kernel,
        grid=(t // tm, nj),
        in_specs=[
            pl.BlockSpec((tm, d), row),
            pl.BlockSpec((tm, gw), row), pl.BlockSpec((tm, gw), row), pl.BlockSpec((tm, gw), row),
            pl.BlockSpec((tm, LANES), row), pl.BlockSpec((tm, LANES), row),
            pl.BlockSpec((tm, LANES), row),
            pl.BlockSpec((tm, tn), lambda i, j: (i, col_ga + j)),
            pl.BlockSpec((tm, tn), lambda i, j: (i, col_gb + j)),
            pl.BlockSpec((d, tn), lambda i, j: (0, j)),
            pl.BlockSpec((gw, tn), lambda i, j: (0, j)),
        ],
        out_specs=pl.BlockSpec((tm, tn), lambda i, j: (i, j)),
        out_shape=jax.ShapeDtypeStruct((t, d), MXU_DTYPE),
        scratch_shapes=[pltpu.VMEM((tm, gw), MXU_DTYPE)],
        compiler_params=_params("parallel", "arbitrary"),
        name="merge",
    )(g, *outs, *lses, z, z, w_rnn, w_att)


def _oproj_kernel(x_ref, mx_ref, wo_ref, gain_ref, h_ref, hn_ref, hnt_ref):
    h = x_ref[...] + jnp.dot(mx_ref[...], wo_ref[...], preferred_element_type=F32)
    h_ref[...] = h
    hn = _rms(h, gain_ref[...])
    hn_ref[...] = hn.astype(hn_ref.dtype)
    hnt_ref[...] = hn.T.astype(hnt_ref.dtype)


def _oproj(x, mixed, w_o, gain, *, tm):
    t, d = x.shape
    row = lambda i: (i, 0)
    return pl.pallas_call(
        _oproj_kernel,
        grid=(t // tm,),
        in_specs=[
            pl.BlockSpec((tm, d), row),
            pl.BlockSpec((tm, d), row),
            pl.BlockSpec((d, d), lambda i: (0, 0)),
            pl.BlockSpec((1, d), lambda i: (0, 0)),
        ],
        out_specs=[pl.BlockSpec((tm, d), row), pl.BlockSpec((tm, d), row),
                   pl.BlockSpec((d, tm), lambda i: (0, i))],
        out_shape=[jax.ShapeDtypeStruct((t, d), F32), jax.ShapeDtypeStruct((t, d), MXU_DTYPE),
                   jax.ShapeDtypeStruct((d, t), MXU_DTYPE)],
        compiler_params=_params("parallel"),
        name="oproj",
    )(x, mixed, w_o, gain)


def _all_sublanes(x, op):
    for sh in (4, 2, 1):
        x = op(x, pltpu.roll(x, sh, 0))
    return x


def _take_ranked(vals, order, count):
    rest = vals
    rank = jnp.full(vals.shape, float(count), F32)
    tops, firsts = [], []
    for k in range(count):
        m = _all_sublanes(jnp.max(rest, axis=0), jnp.maximum)
        first = _all_sublanes(jnp.min(jnp.where(rest == m[None], order, POS_INF), axis=0),
                              jnp.minimum)
        hit = order == first[None]
        rest = jnp.where(hit, NEG_INF, rest)
        rank = jnp.where(hit, float(k), rank)
        tops.append(m)
        firsts.append(first)
    return tops, firsts, rank


def _sort_pairs(n):
    pairs, p = [], 1
    while p < n:
        k = p
        while k >= 1:
            for j in range(k % p, n - k, 2 * k):
                for i in range(min(k, n - j - k)):
                    if (i + j) // (2 * p) == (i + j + k) // (2 * p):
                        pairs.append((i + j, i + j + k))
            k //= 2
        p *= 2
    return tuple(pairs)


def _bitonic_pairs(n):
    pairs, dist = [], n // 2
    while dist >= 1:
        pairs += [(i, i + dist) for i in range(n) if (i // dist) % 2 == 0]
        dist //= 2
    return tuple(pairs)


_SORT_TOPK = _sort_pairs(PEER_TOPK)
_MERGE_TOPK = _bitonic_pairs(PEER_TOPK)


def _top_values(vals):
    nt = PEER_TOPK
    v = [vals[k] for k in range(nt)]

    def exchange(pairs):
        for a, b in pairs:
            v[a], v[b] = jnp.maximum(v[a], v[b]), jnp.minimum(v[a], v[b])

    exchange(_SORT_TOPK)
    for sh in (4, 2, 1):
        other = [pltpu.roll(x, sh, 0) for x in v]
        v = [jnp.maximum(v[k], other[nt - 1 - k]) for k in range(nt)]
        exchange(_MERGE_TOPK)
    return v


def _route_lanes_by_value(s0, s1):
    nt = PEER_TOPK
    nk, ln = s0.shape
    nv = nk // SUBLANES
    assert nv == nt == 2 * SUBLANES
    s0 = s0.reshape(nv, SUBLANES, ln)
    s1 = s1.reshape(nv, SUBLANES, ln)
    top0 = _top_values(s0)
    top1 = _top_values(s1)
    sub = lax.broadcasted_iota(jnp.int32, (SUBLANES, ln), 0)

    def on_sublanes(rows):
        out = rows[0]
        for b in range(1, SUBLANES):
            out = jnp.where(sub == b, rows[b], out)
        return out

    def count(mask):
        return _all_sublanes(jnp.where(mask, 1.0, 0.0), jnp.add)

    t1lo, t1hi = on_sublanes(top1[:SUBLANES]), on_sublanes(top1[SUBLANES:])
    t0hi = on_sublanes(top0[SUBLANES:])
    pieces = [top0[0] + t1lo, top0[0] + t1hi]
    for a in range(1, SUBLANES):
        pieces.append(jnp.where(sub < nt // (a + 1), top0[a] + t1lo, NEG_INF))
    pieces.append(t0hi + top1[0])
    pad = jnp.full((SUBLANES, ln), NEG_INF, F32)
    best = _top_values(jnp.stack(pieces + [pad] * (nt - len(pieces))))
    cut = best[nt - 1]
    zsum = jnp.ones((SUBLANES, ln), F32)
    for k in range(1, nt):
        zsum = zsum + jnp.exp(best[k] - best[0])

    counts = [count(pieces[0] >= cut) + count(pieces[1] >= cut)]
    counts += [count(pieces[a + 1] >= cut) for a in range(1, SUBLANES)]
    total = count(pieces[SUBLANES + 1] >= cut)
    th_vals = []
    for a in range(SUBLANES):
        total = total + counts[a]
        tv = jnp.full((SUBLANES, ln), POS_INF, F32)
        for m in range(nt // (a + 1)):
            tv = jnp.where(counts[a] == float(m + 1), top1[m], tv)
        th_vals.append(tv)

    member0 = s0 >= top0[nt - 1][None]
    member1 = s1 >= top1[nt - 1][None]
    th = jnp.where(member0 & (s0 + top1[0][None] >= cut[None]), top1[0][None], POS_INF)
    for a in range(SUBLANES):
        th = jnp.where(s0 == top0[a][None], th_vals[a][None], th)
    c = jnp.exp(s0 - top0[0][None]) / zsum[None]
    r1 = jnp.where(member1, s1, NEG_INF)
    e1 = jnp.exp(s1 - top1[0][None])

    gap = top0[0] - top0[1]
    for t in (top0, top1):
        for k in range(nt - 1):
            gap = jnp.minimum(gap, t[k] - t[k + 1])
    n0 = _all_sublanes(jnp.sum(jnp.where(member0, 1.0, 0.0), axis=0), jnp.add)
    n1 = _all_sublanes(jnp.sum(jnp.where(member1, 1.0, 0.0), axis=0), jnp.add)
    tie = (jnp.where(gap <= 0.0, 1.0, 0.0) + jnp.abs(n0 - float(nt)) + jnp.abs(n1 - float(nt))
           + jnp.abs(total - float(nt)))
    return tuple(x.reshape(nk, ln) for x in (th, c, r1, e1)) + (tie,)


def _route_lanes(s0, s1):
    th, c, r1, e1, tie = _route_lanes_by_value(s0, s1)
    return lax.cond(jnp.max(tie) > 0.0, lambda: _route_lanes_by_rank(s0, s1),
                    lambda: (th, c, r1, e1))


def _route_lanes_by_rank(s0, s1):
    nt = PEER_TOPK
    nk, ln = s0.shape
    nv = nk // SUBLANES
    s0 = s0.reshape(nv, SUBLANES, ln)
    s1 = s1.reshape(nv, SUBLANES, ln)
    key = (lax.broadcasted_iota(jnp.int32, (nv, SUBLANES, ln), 0) * SUBLANES
           + lax.broadcasted_iota(jnp.int32, (nv, SUBLANES, ln), 1)).astype(F32)
    top0, _, rank0 = _take_ranked(s0, key, nt)
    top1, _, rank1 = _take_ranked(s1, key, nt)

    sub = lax.broadcasted_iota(jnp.int32, (SUBLANES, ln), 0)
    subf = sub.astype(F32)

    def on_sublanes(rows):
        out = rows[0]
        for b in range(1, SUBLANES):
            out = jnp.where(sub == b, rows[b], out)
        return out

    assert nt == 2 * SUBLANES
    t1lo, t1hi = on_sublanes(top1[:SUBLANES]), on_sublanes(top1[SUBLANES:])
    t0hi = on_sublanes(top0[SUBLANES:])
    vals = [top0[0] + t1lo, top0[0] + t1hi]
    poss = [subf, subf + SUBLANES]
    for a in range(1, SUBLANES):
        vals.append(jnp.where(sub < nt // (a + 1), top0[a] + t1lo, NEG_INF))
        poss.append(subf + a * nt)
    vals.append(t0hi + top1[0])
    poss.append((subf + SUBLANES) * nt)
    best, bpos, _ = _take_ranked(jnp.stack(vals), jnp.stack(poss), nt)

    zsum = jnp.zeros((SUBLANES, ln), F32)
    cnt_lo = jnp.zeros((SUBLANES, ln), F32)
    cnt_hi = jnp.zeros((SUBLANES, ln), F32)
    for k in range(nt):
        zsum = zsum + jnp.exp(best[k] - best[0])
        a_k = jnp.floor(bpos[k] * (1.0 / nt))
        cnt_lo = cnt_lo + jnp.where(subf == a_k, 1.0, 0.0)
        cnt_hi = cnt_hi + jnp.where(subf + SUBLANES == a_k, 1.0, 0.0)

    th = jnp.full((nv, SUBLANES, ln), POS_INF, F32)
    for a in range(nt):
        cnt = cnt_lo if a < SUBLANES else cnt_hi
        n_a = jnp.broadcast_to(cnt[a % SUBLANES:a % SUBLANES + 1], (SUBLANES, ln))
        th = jnp.where(rank0 == float(a), (1.0 - n_a)[None], th)
    c = jnp.exp(s0 - top0[0][None]) / zsum[None]
    r1 = jnp.where(rank1 < float(nt), -rank1, NEG_INF)
    e1 = jnp.exp(s1 - top1[0][None])
    return tuple(x.reshape(nk, ln) for x in (th, c, r1, e1))


def _route_kernel(hn_ref, wq_ref, sk_ref, th_ref, c_ref, r1_ref, e1_ref, q_sc):
    q = jnp.dot(hn_ref[...], wq_ref[...], preferred_element_type=F32)
    tm = hn_ref.shape[0]
    d_half = sk_ref.shape[2]
    for hp in range(2 * PEER_HEADS):
        q_sc[hp] = q[:, hp * d_half:(hp + 1) * d_half]

    def head(h, _):
        def scores(half):
            return lax.dot_general(sk_ref[2 * h + half], q_sc[2 * h + half],
                                   (((1,), (1,)), ((), ())),
                                   preferred_element_type=F32)

        s0, s1 = scores(0), scores(1)
        for lc in range(tm // LANES):
            ls = slice(lc * LANES, (lc + 1) * LANES)
            th, c, r1, e1 = _route_lanes(s0[:, ls], s1[:, ls])
            th_ref[h, :, ls] = th
            c_ref[h, :, ls] = c
            r1_ref[h, :, ls] = r1
            e1_ref[h, :, ls] = e1
        return 0

    lax.fori_loop(0, PEER_HEADS, head, 0)


def _route(hn, w_query, sub_keys, *, tm):
    t, d = hn.shape
    dq = w_query.shape[1]
    slab = jax.ShapeDtypeStruct((PEER_HEADS, N_KEYS, t), F32)
    slab_spec = pl.BlockSpec((PEER_HEADS, N_KEYS, tm), lambda i: (0, 0, i))
    return pl.pallas_call(
        _route_kernel,
        grid=(t // tm,),
        in_specs=[
            pl.BlockSpec((tm, d), lambda i: (i, 0)),
            pl.BlockSpec((d, dq), lambda i: (0, 0)),
            pl.BlockSpec(sub_keys.shape, lambda i: (0, 0, 0)),
        ],
        out_specs=[slab_spec] * 4,
        out_shape=[slab] * 4,
        scratch_shapes=[pltpu.VMEM((2 * PEER_HEADS, tm, sub_keys.shape[2]), F32)],
        compiler_params=_params("parallel"),
        name="peer_route",
    )(hn, w_query, sub_keys)


PEER_ROW_CHUNK = 32


def _peer_kernel(x_ref, u_ref, vt_ref, th_ref, c_ref, r1_ref, e1_ref, y_ref, act_sc, gate_sc,
                 coef_sc):
    e = pl.program_id(1)
    tm = x_ref.shape[1]
    ib = th_ref.shape[1]
    hk = ib // 2
    hr = hk * N_KEYS
    assert tm % LANES == 0 and N_KEYS % PEER_ROW_CHUNK == 0 and ib % 2 == 0

    @pl.when(e == 0)
    def _():
        y_ref[...] = jnp.zeros_like(y_ref)

    def first_matmul(half):
        rows = slice(half * hr, (half + 1) * hr)
        act_sc[rows, :] = jnp.dot(u_ref[rows, :], x_ref[...], preferred_element_type=F32)

    def routing_weights(half):
        for lc in range(tm // LANES):
            ls = slice(lc * LANES, (lc + 1) * LANES)
            for jc in range(N_KEYS // PEER_ROW_CHUNK):
                js = slice(jc * PEER_ROW_CHUNK, (jc + 1) * PEER_ROW_CHUNK)
                gates = [jnp.zeros((PEER_ROW_CHUNK, LANES), F32) for _ in range(hk)]
                for h in range(PEER_HEADS):
                    r1 = r1_ref[h, js, ls]
                    e1 = e1_ref[h, js, ls]
                    for k in range(hk):
                        ii = half * hk + k
                        th = th_ref[h, ii:ii + 1, ls]
                        ch = c_ref[h, ii:ii + 1, ls]
                        gates[k] = gates[k] + jnp.where(r1 >= th, e1 * ch, 0.0)
                for k in range(hk):
                    r0 = (half * hk + k) * N_KEYS + jc * PEER_ROW_CHUNK
                    gate_sc[r0:r0 + PEER_ROW_CHUNK, ls] = gates[k]

    def second_matmul(half):
        rows = slice(half * hr, (half + 1) * hr)
        coef_sc[rows, :] = (_gelu(act_sc[rows, :]) * gate_sc[rows, :]).astype(coef_sc.dtype)
        y_ref[...] += jnp.dot(vt_ref[:, rows], coef_sc[rows, :], preferred_element_type=F32)

    @pl.when(e >= 0)
    def _():
        first_matmul(0)
        routing_weights(0)

    @pl.when(e >= -1)
    def _():
        first_matmul(1)
        routing_weights(1)
        second_matmul(0)

    @pl.when(e >= -2)
    def _():
        second_matmul(1)


MXU_TILE = 256
PEER_CHAIN_ROWS = 16
PEER_CHAIN_KEYS = 4
PEER_CHAIN_BLOCK_ROWS = 256


def _zero_after(x):
    bits = pltpu.bitcast(x, jnp.uint32)
    z = lax.shift_right_logical(lax.shift_right_logical(bits, jnp.uint32(16)), jnp.uint32(16))
    return z[0:1, :].astype(F32).astype(MXU_DTYPE)


def _chained(blocks, pieces):
    nb, npc = len(blocks), len(pieces)
    out = []
    for b in range(nb):
        dep = None
        for p in range(b * npc // nb, (b + 1) * npc // nb):
            v = pieces[p]()
            dep = v if dep is None else dep + v
        blk = blocks[b]()
        if dep is not None:
            zero = _zero_after(dep)
            blk = blk + jnp.concatenate([zero] * (blk.shape[1] // LANES), axis=1)
        out.append(blk)
    return out


def _peer_chain_kernel(x_ref, u_ref, vt_ref, th_ref, c_ref, r1_ref, e1_ref, y_ref, gate_sc):
    e = pl.program_id(1)
    d, tm = x_ref.shape
    te = u_ref.shape[0]
    ib = th_ref.shape[1]
    nkt = d // MXU_TILE
    nj = N_KEYS // PEER_CHAIN_ROWS
    nchunks = (tm // LANES) * nj
    assert ib % PEER_CHAIN_KEYS == 0 and te % PEER_CHAIN_BLOCK_ROWS == 0

    @pl.when(e == 0)
    def _():
        y_ref[...] = jnp.zeros_like(y_ref)

    def gate_piece(ci, keys):
        def run():
            ls = slice((ci // nj) * LANES, (ci // nj + 1) * LANES)
            js = slice((ci % nj) * PEER_CHAIN_ROWS, (ci % nj + 1) * PEER_CHAIN_ROWS)
            gates = {ii: jnp.zeros((PEER_CHAIN_ROWS, LANES), F32) for ii in keys}
            for h in range(PEER_HEADS):
                r1 = r1_ref[h, js, ls]
                e1 = e1_ref[h, js, ls]
                for ii in keys:
                    th = th_ref[h, ii:ii + 1, ls]
                    ch = c_ref[h, ii:ii + 1, ls]
                    gates[ii] = gates[ii] + jnp.where(r1 >= th, e1 * ch, 0.0)
            total = None
            for ii in keys:
                r0 = ii * N_KEYS + (ci % nj) * PEER_CHAIN_ROWS
                gate_sc[r0:r0 + PEER_CHAIN_ROWS, ls] = gates[ii]
                total = gates[ii] if total is None else total + gates[ii]
            return jnp.sum(total.reshape(PEER_CHAIN_ROWS // SUBLANES, SUBLANES, LANES), axis=0)
        return run

    pieces = [gate_piece(ci, tuple(range(k0, k0 + PEER_CHAIN_KEYS)))
              for ci in range(nchunks) for k0 in range(0, ib, PEER_CHAIN_KEYS)]

    rows_per = PEER_CHAIN_BLOCK_ROWS
    nrb = te // rows_per
    u_blocks = [(lambda kt=kt, rb=rb: u_ref[rb * rows_per:(rb + 1) * rows_per,
                                            kt * MXU_TILE:(kt + 1) * MXU_TILE])
                for kt in range(nkt) for rb in range(nrb)]
    blocks = _chained(u_blocks, pieces)
    lhs = jnp.concatenate([jnp.concatenate(blocks[kt * nrb:(kt + 1) * nrb], axis=0)
                           for kt in range(nkt)], axis=1)
    act = jnp.dot(lhs, x_ref[...], preferred_element_type=F32)
    coef = (_gelu(act) * gate_sc[...]).astype(MXU_DTYPE)
    y_ref[...] += jnp.dot(vt_ref[...], coef, preferred_element_type=F32)


def _peer(hnt, u, vt, th, c, r1, e1, *, tm, ib):
    d, t = hnt.shape
    ne = u.shape[0]
    te = ib * N_KEYS
    slab_spec = pl.BlockSpec((PEER_HEADS, N_KEYS, tm), lambda i, e: (0, 0, i))
    first_spec = pl.BlockSpec((PEER_HEADS, ib, tm), lambda i, e: (0, e, i))
    if tm % (4 * LANES) == 0:
        return pl.pallas_call(
            _peer_chain_kernel,
            grid=(t // tm, ne // te),
            in_specs=[
                pl.BlockSpec((d, tm), lambda i, e: (0, i)),
                pl.BlockSpec((te, d), lambda i, e: (e, 0)),
                pl.BlockSpec((d, te), lambda i, e: (0, e)),
                first_spec, first_spec, slab_spec, slab_spec,
            ],
            out_specs=pl.BlockSpec((d, tm), lambda i, e: (0, i)),
            out_shape=jax.ShapeDtypeStruct((d, t), F32),
            scratch_shapes=[pltpu.VMEM((te, tm), F32)],
            compiler_params=_params("parallel", "arbitrary"),
            name="peer_experts_chain",
        )(hnt, u, vt, th, c, r1, e1)
    return pl.pallas_call(
        _peer_kernel,
        grid=(t // tm, ne // te),
        in_specs=[
            pl.BlockSpec((d, tm), lambda i, e: (0, i)),
            pl.BlockSpec((te, d), lambda i, e: (e, 0)),
            pl.BlockSpec((d, te), lambda i, e: (0, e)),
            first_spec, first_spec, slab_spec, slab_spec,
        ],
        out_specs=pl.BlockSpec((d, tm), lambda i, e: (0, i)),
        out_shape=jax.ShapeDtypeStruct((d, t), F32),
        scratch_shapes=[pltpu.VMEM((te, tm), F32), pltpu.VMEM((te, tm), F32),
                        pltpu.VMEM((te, tm), MXU_DTYPE)],
        compiler_params=_params("parallel", "arbitrary"),
        name="peer_experts",
    )(hnt, u, vt, th, c, r1, e1)


def _final_kernel(h_ref, yt_ref, gain_ref, out_ref):
    out_ref[...] = _rms(h_ref[...] + yt_ref[...].T, gain_ref[...])


def _final(h, yt, gain, *, tm):
    t, d = h.shape
    return pl.pallas_call(
        _final_kernel,
        grid=(t // tm,),
        in_specs=[pl.BlockSpec((tm, d), lambda i: (i, 0)), pl.BlockSpec((d, tm), lambda i: (0, i)),
                  pl.BlockSpec((1, d), lambda i: (0, 0))],
        out_specs=pl.BlockSpec((tm, d), lambda i: (i, 0)),
        out_shape=jax.ShapeDtypeStruct((t, d), F32),
        compiler_params=_params("parallel"),
        name="final_norm",
    )(h, yt, gain)


TOKEN_TILES = {"inproj": 1024, "rnn": 256, "merge": 512, "oproj": 512, "route": 256,
               "peer": 512, "final": 512}


def _tile(n, stage):
    return min(n, TOKEN_TILES[stage])


def _token_stages(x2, z, g, outs, lses, w, cols):
    t = x2.shape[0]
    mixed = _merge(g, outs, lses, z, w["w_rnn_out"], w["w_att_out"], col_ga=cols["ga"],
                   col_gb=cols["gb"], tm=_tile(t, "merge"), tn=HEADS * HEAD_DIM)
    h, hn, hnt = _oproj(x2, mixed, w["w_o"], w["norm_ffn"], tm=_tile(t, "oproj"))
    th, c, r1, e1 = _route(hn, w["w_query"], w["sub_keys"], tm=_tile(t, "route"))
    yt = _peer(hnt, w["expert_u"], w["expert_vt"], th, c, r1, e1, tm=_tile(t, "peer"),
               ib=SUBLANES)
    return _final(h, yt, w["norm_final"], tm=_tile(t, "final"))


def kernel(x_prompt, x_sample, state_conv, state_h, cache_kv_w128, cache_kv_w512, cache_kv_w2048,
           norm_mix, w_in, conv_w, conv_b, w_gate_x, b_gate_x, w_gate_a, b_gate_a, lru_lambda,
           w_rnn_out, w_att_out, w_o, norm_ffn, w_query, sub_keys, expert_u, expert_v, norm_final):
    depth = w_in.shape[0]
    assert depth == 1 and x_prompt.shape[0] == 1
    batch, seq, d = x_prompt.shape
    nb, steps, _ = x_sample.shape
    d_in = w_in.shape[2]
    gw = HEADS * HEAD_DIM
    d_qkv = N_GROUPS * gw
    assert d_in == 4 * d + 3 * d_qkv and d % gw == 0
    caches = (cache_kv_w128, cache_kv_w512, cache_kv_w2048)
    col_q, col_k, col_v = 2 * d // gw, (2 * d + d_qkv) // gw, (2 * d + 2 * d_qkv) // gw
    cols = {"ga": (2 * d + 3 * d_qkv) // gw, "gb": (3 * d + 3 * d_qkv) // gw}

    mx = lambda a: a.astype(MXU_DTYPE)
    row = lambda a: a.reshape(1, -1).astype(F32)
    w = {
        "w_in": mx(w_in[0]), "w_rnn_out": mx(w_rnn_out[0]), "w_att_out": mx(w_att_out[0]),
        "w_o": mx(w_o[0]), "w_query": mx(w_query[0]),
        "sub_keys": sub_keys[0].reshape(2 * PEER_HEADS, N_KEYS, -1),
        "expert_u": mx(expert_u[0]), "expert_vt": mx(expert_v[0]).T,
        "norm_ffn": row(norm_ffn[0]), "norm_final": row(norm_final),
    }
    rnn_w = (conv_w[0], row(conv_b[0]), mx(w_gate_x[0]), row(b_gate_x[0]), mx(w_gate_a[0]),
             row(b_gate_a[0]), row(lru_lambda[0]))
    gain_mix = row(norm_mix[0])

    xp = x_prompt.reshape(seq, d)
    zp = _inproj(xp, gain_mix, w["w_in"], tm=_tile(seq, "inproj"), tn=gw)
    g_p, past_p, h_p = _rnn_prompt(zp, *rnn_w, jnp.zeros((SUBLANES, d), F32),
                                   jnp.zeros((1, d), F32), d=d, tt=_tile(seq, "rnn"))
    outs, lses, kv_p = [], [], []
    for gi, (win, dil) in enumerate(ATT_GROUPS):
        o, lse = _attn_prompt(zp, gi, dil, col_q=col_q, col_k=col_k, col_v=col_v)
        outs.append(o)
        lses.append(lse)
        keep = min(win, seq)
        kcol = (col_k + gi) * gw
        vcol = (col_v + gi) * gw
        kv = jnp.stack([zp[seq - keep:, kcol:kcol + gw], zp[seq - keep:, vcol:vcol + gw]], axis=1)
        kv_p.append(kv.reshape(1, 1, keep, 2, HEADS, HEAD_DIM))
    y_p = _token_stages(xp, zp, g_p, outs, lses, w, cols).reshape(batch, seq, d)
    conv_p = past_p[SUBLANES - (CONV_W - 1):].reshape(1, 1, CONV_W - 1, d)
    h_p = h_p.reshape(1, 1, d)

    ts = nb * steps
    xs = x_sample.reshape(ts, d)
    zs = _inproj(xs, gain_mix, w["w_in"], tm=ts, tn=gw)
    g_s, conv_s, h_s = _rnn_sample(zs.reshape(nb, steps * d_in), *rnn_w,
                                   state_conv[0].reshape(nb, (CONV_W - 1) * d), state_h[0],
                                   d=d, steps=steps)
    g_s = g_s.reshape(ts, d)
    qkv = zs[:, 2 * d:2 * d + 3 * d_qkv].reshape(nb, steps, 3, N_GROUPS, HEADS, HEAD_DIM)
    outs, lses, kv_s = [], [], []
    for gi, (win, dil) in enumerate(ATT_GROUPS):
        cache = caches[gi][0]
        o, lse, new_cache = _attn_sample(qkv[:, :, 0, gi], qkv[:, :, 1, gi], qkv[:, :, 2, gi], cache,
                                         dil, rch=min(cache.shape[1], ATTN_SAMPLE_CHUNK_ROWS))
        outs.append(o.reshape(ts, gw))
        lse = lse[..., 0].reshape(ts, HEADS)
        lses.append(jnp.pad(lse, ((0, 0), (0, LANES - HEADS))))
        kv_s.append(new_cache[None])
    y_s = _token_stages(xs, zs, g_s, outs, lses, w, cols).reshape(nb, steps, d)
    conv_s = conv_s.reshape(1, nb, CONV_W - 1, d)
    h_s = h_s.reshape(1, nb, d)

    return (y_p, y_s, conv_p, h_p, kv_p[0], kv_p[1], kv_p[2],
            conv_s, h_s, kv_s[0], kv_s[1], kv_s[2])
```
